```python
import jax, jax.numpy as jnp
from jax import lax
import numpy as np

D_MODEL = 1024
BATCH = 2
SEQ = 16384
DEPTH = 2

HEAD_DIM = 64
FOX_HEADS = 16
SWA_Q_HEADS = 16
SWA_KV_HEADS = 2
SWA_GROUP = SWA_Q_HEADS // SWA_KV_HEADS
WINDOW = 128
BLOCK = 128
D_FF = 4 * D_MODEL
ROPE_THETA = 10000.0
NORM_EPS = 1e-6
N_MIXERS = 2
N_FOX = (DEPTH + 1) // 2
N_SWA = DEPTH // 2
FOX_IN = 3 * FOX_HEADS * HEAD_DIM + FOX_HEADS
SWA_IN = (SWA_Q_HEADS + 2 * SWA_KV_HEADS) * HEAD_DIM

kernel_name = "fox_swa_sink_hybrid_sandwich"


def rms_norm(x, g):
    xf = x.astype(jnp.float32)
    y = xf * lax.rsqrt(jnp.mean(xf * xf, axis=-1, keepdims=True) + NORM_EPS)
    return (y * g.astype(jnp.float32)).astype(x.dtype)


def rope(x, positions):
    half = HEAD_DIM // 2
    inv_freq = 1.0 / (ROPE_THETA ** (jnp.arange(0, HEAD_DIM, 2, dtype=jnp.float32) / HEAD_DIM))
    ang = positions.astype(jnp.float32)[..., None] * inv_freq
    cos = jnp.cos(ang)[:, :, None, :].astype(x.dtype)
    sin = jnp.sin(ang)[:, :, None, :].astype(x.dtype)
    x1, x2 = x[..., :half], x[..., half:]
    return jnp.concatenate([x1 * cos - x2 * sin, x2 * cos + x1 * sin], axis=-1)


def fox_attention(h, w_in, b_f, w_out):
    B, S, _ = h.shape
    nb = S // BLOCK
    hd_all = FOX_HEADS * HEAD_DIM
    proj = h @ w_in
    scale = HEAD_DIM ** -0.5
    q = proj[..., :hd_all].reshape(B, S, FOX_HEADS, HEAD_DIM) * scale
    k = proj[..., hd_all:2 * hd_all].reshape(B, S, FOX_HEADS, HEAD_DIM)
    v = proj[..., 2 * hd_all:3 * hd_all].reshape(B, S, FOX_HEADS, HEAD_DIM)
    log_f = jax.nn.log_sigmoid((proj[..., 3 * hd_all:] + b_f).astype(jnp.float32))
    c = jnp.cumsum(log_f, axis=1).transpose(0, 2, 1)
    qb = q.transpose(0, 2, 1, 3).reshape(B, FOX_HEADS, nb, BLOCK, HEAD_DIM).transpose(2, 0, 1, 3, 4)
    cq = c.reshape(B, FOX_HEADS, nb, BLOCK).transpose(2, 0, 1, 3)
    kt = k.transpose(0, 2, 1, 3)
    vt = v.transpose(0, 2, 1, 3)
    key_pos = jnp.arange(S)

    def query_block(args):
        q_blk, c_blk, i = args
        s = jnp.einsum('bhqd,bhkd->bhqk', q_blk, kt, preferred_element_type=jnp.float32)
        s = s + c_blk[..., None] - c[:, :, None, :]
        q_pos = i * BLOCK + jnp.arange(BLOCK)
        s = jnp.where(key_pos[None, :] <= q_pos[:, None], s, -jnp.inf)
        p = jax.nn.softmax(s, axis=-1).astype(vt.dtype)
        return jnp.einsum('bhqk,bhkd->bhqd', p, vt)

    o = lax.map(query_block, (qb, cq, jnp.arange(nb)))
    o = o.transpose(1, 0, 3, 2, 4).reshape(B, S, hd_all)
    return o @ w_out


def swa_sink_attention(h, positions, w_in, sinks, w_out):
    B, S, _ = h.shape
    nb = S // BLOCK
    qd = SWA_Q_HEADS * HEAD_DIM
    kvd = SWA_KV_HEADS * HEAD_DIM
    proj = h @ w_in
    scale = HEAD_DIM ** -0.5
    q = rope(proj[..., :qd].reshape(B, S, SWA_Q_HEADS, HEAD_DIM), positions) * scale
    k = rope(proj[..., qd:qd + kvd].reshape(B, S, SWA_KV_HEADS, HEAD_DIM), positions)
    v = proj[..., qd + kvd:].reshape(B, S, SWA_KV_HEADS, HEAD_DIM)
    q = q.reshape(B, nb, BLOCK, SWA_KV_HEADS, SWA_GROUP, HEAD_DIM)
    k = k.reshape(B, nb, BLOCK, SWA_KV_HEADS, HEAD_DIM)
    v = v.reshape(B, nb, BLOCK, SWA_KV_HEADS, HEAD_DIM)

    def band(t):
        prev = jnp.pad(t, ((0, 0), (1, 0), (0, 0), (0, 0), (0, 0)))[:, :-1]
        return jnp.concatenate([prev, t], axis=2)

    kb, vb = band(k), band(v)
    s = jnp.einsum('bnqhgd,bnshd->bhgnqs', q, kb, preferred_element_type=jnp.float32)
    qi = jnp.arange(BLOCK)[:, None]
    kj = jnp.arange(2 * BLOCK)[None, :]
    diff = qi + BLOCK - kj
    in_window = (diff >= 0) & (diff < WINDOW)
    key_exists = (jnp.arange(nb)[:, None, None] > 0) | (kj >= BLOCK)[None]
    mask = in_window[None] & key_exists
    s = jnp.where(mask, s, -jnp.inf)
    sink = sinks.astype(jnp.float32).reshape(SWA_KV_HEADS, SWA_GROUP)[None, :, :, None, None, None]
    m = jnp.maximum(jnp.max(s, axis=-1, keepdims=True), sink)
    e = jnp.exp(s - m)
    denom = jnp.sum(e, axis=-1, keepdims=True) + jnp.exp(sink - m)
    p = (e / denom).astype(vb.dtype)
    o = jnp.einsum('bhgnqs,bnshd->bnqhgd', p, vb).reshape(B, S, qd)
    return o @ w_out


def squared_relu_mlp(h, w_up, w_down):
    return jnp.square(jax.nn.relu(h @ w_up)) @ w_down


def setup_inputs(seed: int = 0) -> dict:
    key = jax.random.key(seed)
    ks = jax.random.split(key, 14)
    f32 = jnp.float32
    nrm = lambda k, shape, s: jax.random.normal(k, shape, f32) * s
    return {
        "x": nrm(ks[0], (BATCH, SEQ, D_MODEL), 1.0),
        "positions": jnp.broadcast_to(jnp.arange(SEQ, dtype=jnp.int32)[None, :], (BATCH, SEQ)),
        "fox_w_in": nrm(ks[1], (N_FOX, D_MODEL, FOX_IN), D_MODEL ** -0.5),
        "fox_b_f": 3.0 + nrm(ks[2], (N_FOX, FOX_HEADS), 1.0),
        "fox_w_out": nrm(ks[3], (N_FOX, FOX_HEADS * HEAD_DIM, D_MODEL), (FOX_HEADS * HEAD_DIM) ** -0.5),
        "swa_w_in": nrm(ks[4], (N_SWA, D_MODEL, SWA_IN), D_MODEL ** -0.5),
        "swa_sinks": nrm(ks[5], (N_SWA, SWA_Q_HEADS), 0.5),
        "swa_w_out": nrm(ks[6], (N_SWA, SWA_Q_HEADS * HEAD_DIM, D_MODEL), (SWA_Q_HEADS * HEAD_DIM) ** -0.5),
        "norm_pre_mix": 1.0 + nrm(ks[7], (DEPTH, D_MODEL), 0.02),
        "norm_post_mix": 1.0 + nrm(ks[8], (DEPTH, D_MODEL), 0.02),
        "norm_pre_ffn": 1.0 + nrm(ks[9], (DEPTH, D_MODEL), 0.02),
        "norm_post_ffn": 1.0 + nrm(ks[10], (DEPTH, D_MODEL), 0.02),
        "mlp_w_up": nrm(ks[11], (DEPTH, D_MODEL, D_FF), D_MODEL ** -0.5),
        "mlp_w_down": nrm(ks[12], (DEPTH, D_FF, D_MODEL), D_FF ** -0.5),
    }


def reference(x, positions, fox_w_in, fox_b_f, fox_w_out, swa_w_in, swa_sinks, swa_w_out,
              norm_pre_mix, norm_post_mix, norm_pre_ffn, norm_post_ffn, mlp_w_up, mlp_w_down):
    for i in range(DEPTH):
        j = i // N_MIXERS
        h = rms_norm(x, norm_pre_mix[i])
        if i % N_MIXERS == 0:
            h = fox_attention(h, fox_w_in[j], fox_b_f[j], fox_w_out[j])
        else:
            h = swa_sink_attention(h, positions, swa_w_in[j], swa_sinks[j], swa_w_out[j])
        x = x + rms_norm(h, norm_post_mix[i])
        h = rms_norm(x, norm_pre_ffn[i])
        h = squared_relu_mlp(h, mlp_w_up[i], mlp_w_down[i])
        x = x + rms_norm(h, norm_post_ffn[i])
    return x
```

```python
import functools
import math

import jax
import jax.numpy as jnp
import numpy as np
from jax import lax
from jax.experimental import pallas as pl
from jax.experimental.pallas import tpu as pltpu

D_MODEL = 1024
HEAD_DIM = 64
FOX_HEADS = 16
SWA_Q_HEADS = 16
SWA_KV_HEADS = 2
SWA_GROUP = SWA_Q_HEADS // SWA_KV_HEADS
WINDOW = 128
D_FF = 4 * D_MODEL
ROPE_THETA = 10000.0
NORM_EPS = 1e-6

LANES = 128
HEAD_AUG = LANES
LOG2E = 1.4426950408889634
QK_SCALE = HEAD_DIM ** -0.5
NEG_BIG = -1e30
VMEM_LIMIT = 56 * 1024 * 1024

DEC0 = HEAD_DIM
N_SPLIT = 3
ONES_LANE = N_SPLIT * FOX_HEADS

PROJ_TM = 512
FLASH_BQ = 512
MLP_CHUNK = 1024


def _rms(x, g):
    return x * lax.rsqrt(jnp.mean(x * x, axis=-1, keepdims=True) + NORM_EPS) * g


def _dot(a, b):
    return jnp.dot(a, b, preferred_element_type=jnp.float32)


def _dot_nt(a, b):
    return lax.dot_general(a, b, (((1,), (1,)), ((), ())), preferred_element_type=jnp.float32)


def _split3(x):
    hi = x.astype(jnp.bfloat16)
    r = x - hi.astype(jnp.float32)
    mid = r.astype(jnp.bfloat16)
    lo = (r - mid.astype(jnp.float32)).astype(jnp.bfloat16)
    return hi, mid, lo


def _const_spec(shape):
    nd = len(shape)
    return pl.BlockSpec(shape, lambda *_: (0,) * nd, pipeline_mode=pl.Buffered(1))


def _fox_proj_kernel(x_ref, g_ref, wq_ref, wk_ref, wv_ref, wf_ref, b3_ref, place_ref, vone_ref,
                     tri_ref, q_out, k_out, v_out, carry_ref, *, tm, chunk):
    @pl.when(pl.program_id(1) == 0)
    def _():
        carry_ref[...] = jnp.zeros_like(carry_ref)

    h = _rms(x_ref[0], g_ref[...]).astype(jnp.bfloat16)

    z = _dot(h, wf_ref[...]) + b3_ref[...]
    logf = jnp.minimum(z, 0.0) - jnp.log1p(jnp.exp(-jnp.abs(z)))

    tri = tri_ref[...]
    hi, mid, lo = _split3(logf)
    c = _dot(tri, hi) + _dot(tri, mid) + _dot(tri, lo) + carry_ref[...]
    carry_ref[...] = c[tm - 1:tm, :]

    chi, cmid, clo = (p.astype(jnp.float32) for p in _split3(c * LOG2E))
    lane = lax.broadcasted_iota(jnp.int32, (tm, LANES), 1)
    parts = jnp.where(lane < FOX_HEADS, chi,
                      jnp.where(lane < 2 * FOX_HEADS, cmid,
                                jnp.where(lane < ONES_LANE, clo,
                                          jnp.where(lane == ONES_LANE, 1.0, 0.0))))
    parts = parts.astype(jnp.bfloat16)

    width = FOX_HEADS * HEAD_AUG
    for n in range(width // chunk):
        cols = slice(n * chunk, (n + 1) * chunk)
        kcols = slice(width + n * chunk, width + (n + 1) * chunk)
        q = _dot(h, wq_ref[:, cols]) * (QK_SCALE * LOG2E) + _dot(parts, place_ref[:, cols])
        q_out[0, :, cols] = q.astype(jnp.bfloat16)
        k = _dot(h, wk_ref[:, cols]) + _dot(parts, place_ref[:, kcols])
        k_out[0, :, cols] = k.astype(jnp.bfloat16)
        v = _dot(h, wv_ref[:, cols]) + vone_ref[:, cols]
        v_out[0, :, cols] = v.astype(jnp.bfloat16)


def _fox_placement():
    width = FOX_HEADS * HEAD_AUG
    p = np.zeros((LANES, 2 * width), np.float32)
    for h in range(FOX_HEADS):
        base = h * HEAD_AUG + DEC0
        for s in range(N_SPLIT):
            p[s * FOX_HEADS + h, base + s] = 1.0
            p[ONES_LANE, base + N_SPLIT + s] = 1.0
            p[ONES_LANE, width + base + s] = 1.0
            p[s * FOX_HEADS + h, width + base + N_SPLIT + s] = -1.0
    return p


def _pad_heads(w, heads):
    d = w.shape[0]
    w = w.reshape(d, heads, HEAD_DIM)
    w = jnp.pad(w, ((0, 0), (0, 0), (0, HEAD_AUG - HEAD_DIM)))
    return w.reshape(d, heads * HEAD_AUG)


def _fox_proj(x, g, w_in, b_f):
    B, S, D = x.shape
    tm = PROJ_TM
    hd_all = FOX_HEADS * HEAD_DIM
    width = FOX_HEADS * HEAD_AUG
    wq = _pad_heads(w_in[:, :hd_all], FOX_HEADS).astype(jnp.bfloat16)
    wk = _pad_heads(w_in[:, hd_all:2 * hd_all], FOX_HEADS).astype(jnp.bfloat16)
    wv = _pad_heads(w_in[:, 2 * hd_all:3 * hd_all], FOX_HEADS).astype(jnp.bfloat16)
    wf = w_in[:, 3 * hd_all:]
    wf3 = jnp.concatenate([wf] * N_SPLIT + [jnp.zeros((D, LANES - ONES_LANE), wf.dtype)], axis=1)
    b3 = jnp.concatenate([b_f] * N_SPLIT + [jnp.zeros((LANES - ONES_LANE,), b_f.dtype)])[None, :]
    place = jnp.asarray(_fox_placement(), jnp.bfloat16)
    vone = np.zeros((1, width), np.float32)
    vone[0, HEAD_DIM::HEAD_AUG] = 1.0
    tri = jnp.asarray(np.tril(np.ones((tm, tm), np.float32)), jnp.bfloat16)

    out_sds = jax.ShapeDtypeStruct((B, S, width), jnp.bfloat16)
    out_spec = pl.BlockSpec((1, tm, width), lambda b, t: (b, t, 0))
    return pl.pallas_call(
        functools.partial(_fox_proj_kernel, tm=tm, chunk=512),
        out_shape=(out_sds, out_sds, out_sds),
        grid=(B, S // tm),
        in_specs=[
            pl.BlockSpec((1, tm, D), lambda b, t: (b, t, 0)),
            _const_spec((1, D)),
            _const_spec((D, width)), _const_spec((D, width)), _const_spec((D, width)),
            _const_spec((D, LANES)), _const_spec((1, LANES)),
            _const_spec((LANES, 2 * width)), _const_spec((1, width)),
            _const_spec((tm, tm)),
        ],
        out_specs=(out_spec, out_spec, out_spec),
        scratch_shapes=[pltpu.VMEM((1, LANES), jnp.float32)],
        compiler_params=pltpu.CompilerParams(
            dimension_semantics=("arbitrary", "arbitrary"), vmem_limit_bytes=VMEM_LIMIT),
        name="fox_proj",
    )(x, g[None, :], wq, wk, wv, wf3.astype(jnp.bfloat16), b3, place, jnp.asarray(vone), tri)


def _fox_flash_kernel(q_ref, k_ref, v_ref, o_ref, m_ref, acc_ref, *, bq):
    qi = pl.program_id(2)
    heads = 2
    m_ref[...] = jnp.full_like(m_ref, NEG_BIG)
    acc_ref[...] = jnp.zeros_like(acc_ref)

    def step(hh, j, masked):
        cols = slice(hh * HEAD_AUG, (hh + 1) * HEAD_AUG)
        start = pl.multiple_of(j * bq, bq)
        q = q_ref[0, :, cols]
        k = k_ref[0, pl.ds(start, bq), cols]
        v = v_ref[0, pl.ds(start, bq), cols]
        s = _dot_nt(q, k)
        if masked:
            row = lax.broadcasted_iota(jnp.int32, (bq, bq), 0)
            col = lax.broadcasted_iota(jnp.int32, (bq, bq), 1)
            s = jnp.where(row >= col, s, NEG_BIG)
        m_prev = m_ref[hh]
        m_new = jnp.maximum(m_prev, jnp.max(s, axis=1, keepdims=True))
        alpha = jnp.exp2(m_prev - m_new)
        p = jnp.exp2(s - m_new).astype(jnp.bfloat16)
        acc_ref[hh] = alpha * acc_ref[hh] + _dot(p, v)
        m_ref[hh] = m_new

    def body(j, carry):
        for hh in range(heads):
            step(hh, j, False)
        return carry

    lax.fori_loop(0, qi, body, 0)
    for hh in range(heads):
        step(hh, qi, True)

    outs = []
    for hh in range(heads):
        acc = acc_ref[hh]
        outs.append(acc * (1.0 / acc[:, HEAD_DIM:HEAD_DIM + 1]))
    lane = lax.broadcasted_iota(jnp.int32, (bq, LANES), 1)
    o = jnp.where(lane < HEAD_DIM, outs[0], pltpu.roll(outs[1], HEAD_DIM, 1))
    o_ref[0] = o.astype(jnp.bfloat16)


def _fox_flash(q, k, v):
    B, S, width = q.shape
    bq = FLASH_BQ
    pair = 2 * HEAD_AUG
    return pl.pallas_call(
        functools.partial(_fox_flash_kernel, bq=bq),
        out_shape=jax.ShapeDtypeStruct((B, S, FOX_HEADS * HEAD_DIM), jnp.bfloat16),
        grid=(B, FOX_HEADS // 2, S // bq),
        in_specs=[
            pl.BlockSpec((1, bq, pair), lambda b, hp, i: (b, i, hp)),
            pl.BlockSpec((1, S, pair), lambda b, hp, i: (b, 0, hp)),
            pl.BlockSpec((1, S, pair), lambda b, hp, i: (b, 0, hp)),
        ],
        out_specs=pl.BlockSpec((1, bq, 2 * HEAD_DIM), lambda b, hp, i: (b, i, hp)),
        scratch_shapes=[pltpu.VMEM((2, bq, 1), jnp.float32),
                        pltpu.VMEM((2, bq, HEAD_AUG), jnp.float32)],
        compiler_params=pltpu.CompilerParams(
            dimension_semantics=("arbitrary", "arbitrary", "arbitrary"),
            vmem_limit_bytes=VMEM_LIMIT),
        name="fox_flash",
    )(q, k, v)


def _out_mlp_kernel(o_ref, x_ref, wo_ref, g_post_ref, g_pre_ref, wu_ref, wd_ref, g_ffn_ref, y_ref):
    a = _dot(o_ref[...], wo_ref[...])
    x1 = x_ref[...] + _rms(a, g_post_ref[...])
    h = _rms(x1, g_pre_ref[...]).astype(jnp.bfloat16)
    acc = jnp.zeros_like(x1)
    for c in range(D_FF // MLP_CHUNK):
        cols = slice(c * MLP_CHUNK, (c + 1) * MLP_CHUNK)
        u = jnp.maximum(_dot(h, wu_ref[:, cols]), 0.0)
        acc = acc + _dot((u * u).astype(jnp.bfloat16), wd_ref[cols, :])
    y_ref[...] = x1 + _rms(acc, g_ffn_ref[...])


def _out_mlp(o, x, w_out, g_post, g_pre, w_up, w_down, g_ffn):
    T, D = x.shape
    tm = PROJ_TM
    row = lambda width: pl.BlockSpec((tm, width), lambda t: (t, 0))
    return pl.pallas_call(
        _out_mlp_kernel,
        out_shape=jax.ShapeDtypeStruct((T, D), jnp.float32),
        grid=(T // tm,),
        in_specs=[row(o.shape[1]), row(D),
                  _const_spec(w_out.shape), _const_spec((1, D)), _const_spec((1, D)),
                  _const_spec(w_up.shape), _const_spec(w_down.shape), _const_spec((1, D))],
        out_specs=row(D),
        compiler_params=pltpu.CompilerParams(
            dimension_semantics=("arbitrary",), vmem_limit_bytes=VMEM_LIMIT),
        name="out_mlp",
    )(o, x, w_out.astype(jnp.bfloat16), g_post[None, :], g_pre[None, :],
      w_up.astype(jnp.bfloat16), w_down.astype(jnp.bfloat16), g_ffn[None, :])


def _swa_proj_kernel(x_ref, pos_ref, g_ref, wq_ref, wqr_ref, wk_ref, wkr_ref, wv_ref, freq_ref, sign_ref,
                     q_out, k_out, v_out):
    h = _rms(x_ref[...], g_ref[...]).astype(jnp.bfloat16)
    ang = pos_ref[...].astype(jnp.float32) * freq_ref[...]
    cos = jnp.cos(ang)
    sin = jnp.sin(ang) * sign_ref[...]
    for c in range(q_out.shape[1] // LANES):
        cols = slice(c * LANES, (c + 1) * LANES)
        q = _dot(h, wq_ref[:, cols]) * cos + _dot(h, wqr_ref[:, cols]) * sin
        q_out[:, cols] = (q * (QK_SCALE * LOG2E)).astype(jnp.bfloat16)
    for c in range(k_out.shape[1] // LANES):
        cols = slice(c * LANES, (c + 1) * LANES)
        k = _dot(h, wk_ref[:, cols]) * cos + _dot(h, wkr_ref[:, cols]) * sin
        k_out[:, cols] = k.astype(jnp.bfloat16)
        v_out[:, cols] = _dot(h, wv_ref[:, cols]).astype(jnp.bfloat16)


def _rot_half_cols(w, heads):
    d = w.shape[0]
    w = w.reshape(d, heads, 2, HEAD_DIM // 2)
    return w[:, :, ::-1, :].reshape(d, heads * HEAD_DIM)


def _dup_heads(w, heads):
    d = w.shape[0]
    w = w.reshape(d, heads, 1, HEAD_DIM)
    return jnp.broadcast_to(w, (d, heads, 2, HEAD_DIM)).reshape(d, heads * 2 * HEAD_DIM)


def _swa_proj(x, pos, g, w_in):
    T, D = x.shape
    tm = PROJ_TM
    qd = SWA_Q_HEADS * HEAD_DIM
    kvd = SWA_KV_HEADS * HEAD_DIM
    kw = SWA_KV_HEADS * 2 * HEAD_DIM
    bf = lambda a: a.astype(jnp.bfloat16)
    wq = w_in[:, :qd]
    wk = w_in[:, qd:qd + kvd]
    wv = w_in[:, qd + kvd:]
    half = HEAD_DIM // 2
    inv_freq = 1.0 / (ROPE_THETA ** (jnp.arange(0, HEAD_DIM, 2, dtype=jnp.float32) / HEAD_DIM))
    freq = jnp.tile(inv_freq, LANES // half)[None, :]
    sign = np.where((np.arange(LANES) % HEAD_DIM) < half, -1.0, 1.0).astype(np.float32)[None, :]
    row = lambda width: pl.BlockSpec((tm, width), lambda t: (t, 0))
    return pl.pallas_call(
        _swa_proj_kernel,
        out_shape=(jax.ShapeDtypeStruct((T, qd), jnp.bfloat16),
                   jax.ShapeDtypeStruct((T, kw), jnp.bfloat16),
                   jax.ShapeDtypeStruct((T, kw), jnp.bfloat16)),
        grid=(T // tm,),
        in_specs=[row(D), row(1), _const_spec((1, D)),
                  _const_spec((D, qd)), _const_spec((D, qd)),
                  _const_spec((D, kw)), _const_spec((D, kw)), _const_spec((D, kw)),
                  _const_spec((1, LANES)), _const_spec((1, LANES))],
        out_specs=(row(qd), row(kw), row(kw)),
        compiler_params=pltpu.CompilerParams(
            dimension_semantics=("arbitrary",), vmem_limit_bytes=VMEM_LIMIT),
        name="swa_proj",
    )(x, pos, g[None, :], bf(wq), bf(_rot_half_cols(wq, SWA_Q_HEADS)),
      bf(_dup_heads(wk, SWA_KV_HEADS)), bf(_dup_heads(_rot_half_cols(wk, SWA_KV_HEADS), SWA_KV_HEADS)),
      bf(_dup_heads(wv, SWA_KV_HEADS)), freq, jnp.asarray(sign))


def _swa_attn_kernel(sink_ref, q_ref, kp_ref, kc_ref, vp_ref, vc_ref, o_ref):
    n = pl.program_id(1)
    blk = WINDOW
    kb = jnp.concatenate([kp_ref[0], kc_ref[0]], axis=0)
    vb = jnp.concatenate([vp_ref[0], vc_ref[0]], axis=0)
    qi = lax.broadcasted_iota(jnp.int32, (blk, 2 * blk), 0)
    kj = lax.broadcasted_iota(jnp.int32, (blk, 2 * blk), 1)
    diff = qi + blk - kj
    first_key = jnp.where(n > 0, 0, blk)
    mask = (diff >= 0) & (diff < WINDOW) & (kj >= first_key)
    lane = lax.broadcasted_iota(jnp.int32, (blk, LANES), 1)
    low = lane < HEAD_DIM
    for c in range(SWA_Q_HEADS // 2):
        qcol = q_ref[0, :, c * LANES:(c + 1) * LANES].astype(jnp.float32)
        outs = []
        for par in range(2):
            head = 2 * c + par
            g = head // SWA_GROUP
            gcols = slice(g * LANES, (g + 1) * LANES)
            keep = low if par == 0 else jnp.logical_not(low)
            qm = jnp.where(keep, qcol, 0.0).astype(jnp.bfloat16)
            s = jnp.where(mask, _dot_nt(qm, kb[:, gcols]), NEG_BIG)
            sink = sink_ref[head] * LOG2E
            m = jnp.maximum(jnp.max(s, axis=1, keepdims=True), sink)
            e = jnp.exp2(s - m)
            denom = jnp.sum(e, axis=1, keepdims=True) + jnp.exp2(sink - m)
            p = (e * (1.0 / denom)).astype(jnp.bfloat16)
            outs.append(_dot(p, vb[:, gcols]))
        o_ref[0, :, c * LANES:(c + 1) * LANES] = jnp.where(low, outs[0], outs[1]).astype(jnp.bfloat16)


def _swa_attn(q, k, v, sinks):
    B, S, qd = q.shape
    kw = k.shape[2]
    blk = WINDOW
    cur = lambda b, n: (b, n, 0)
    prev = lambda b, n: (b, jnp.maximum(n - 1, 0), 0)
    return pl.pallas_call(
        _swa_attn_kernel,
        out_shape=jax.ShapeDtypeStruct((B, S, qd), jnp.bfloat16),
        grid=(B, S // blk),
        in_specs=[pl.BlockSpec(memory_space=pltpu.SMEM),
                  pl.BlockSpec((1, blk, qd), cur),
                  pl.BlockSpec((1, blk, kw), prev), pl.BlockSpec((1, blk, kw), cur),
                  pl.BlockSpec((1, blk, kw), prev), pl.BlockSpec((1, blk, kw), cur)],
        out_specs=pl.BlockSpec((1, blk, qd), cur),
        compiler_params=pltpu.CompilerParams(
            dimension_semantics=("arbitrary", "arbitrary"), vmem_limit_bytes=VMEM_LIMIT),
        name="swa_attn",
    )(sinks, q, k, k, v, v)


def kernel(x, positions, fox_w_in, fox_b_f, fox_w_out, swa_w_in, swa_sinks, swa_w_out,
           norm_pre_mix, norm_post_mix, norm_pre_ffn, norm_post_ffn, mlp_w_up, mlp_w_down):
    B, S, D = x.shape
    T = B * S
    q, k, v = _fox_proj(x, norm_pre_mix[0], fox_w_in[0], fox_b_f[0])
    o = _fox_flash(q, k, v)
    x1 = _out_mlp(o.reshape(T, -1), x.reshape(T, D), fox_w_out[0], norm_post_mix[0], norm_pre_ffn[0],
                  mlp_w_up[0], mlp_w_down[0], norm_post_ffn[0])
    q, k, v = _swa_proj(x1, positions.reshape(T, 1), norm_pre_mix[1], swa_w_in[0])
    o = _swa_attn(q.reshape(B, S, -1), k.reshape(B, S, -1), v.reshape(B, S, -1), swa_sinks[0])
    x2 = _out_mlp(o.reshape(T, -1), x1, swa_w_out[0], norm_post_mix[1], norm_pre_ffn[1],
                  mlp_w_up[1], mlp_w_down[1], norm_post_ffn[1])
    return x2.reshape(B, S, D)
```

```python
import functools
import math

import jax
import jax.numpy as jnp
import numpy as np
from jax import lax
from jax.experimental import pallas as pl
from jax.experimental.pallas import tpu as pltpu

D_MODEL = 1024
HEAD_DIM = 64
FOX_HEADS = 16
SWA_Q_HEADS = 16
SWA_KV_HEADS = 2
SWA_GROUP = SWA_Q_HEADS // SWA_KV_HEADS
WINDOW = 128
D_FF = 4 * D_MODEL
ROPE_THETA = 10000.0
NORM_EPS = 1e-6

LANES = 128
HEAD_AUG = LANES
LOG2E = 1.4426950408889634
QK_SCALE = HEAD_DIM ** -0.5
NEG_BIG = -1e30
VMEM_LIMIT = 56 * 1024 * 1024

DEC0 = HEAD_DIM
N_SPLIT = 3
ONES_LANE = N_SPLIT * FOX_HEADS

PROJ_TM = 512
FLASH_BQ = 512
MLP_CHUNK = 1024


def _rms(x, g):
    return x * lax.rsqrt(jnp.mean(x * x, axis=-1, keepdims=True) + NORM_EPS) * g


def _dot(a, b):
    return jnp.dot(a, b, preferred_element_type=jnp.float32)


def _dot_nt(a, b):
    return lax.dot_general(a, b, (((1,), (1,)), ((), ())), preferred_element_type=jnp.float32)


def _split3(x):
    hi = x.astype(jnp.bfloat16)
    r = x - hi.astype(jnp.float32)
    mid = r.astype(jnp.bfloat16)
    lo = (r - mid.astype(jnp.float32)).astype(jnp.bfloat16)
    return hi, mid, lo


def _const_spec(shape):
    nd = len(shape)
    return pl.BlockSpec(shape, lambda *_: (0,) * nd, pipeline_mode=pl.Buffered(1))


def _fox_proj_kernel(x_ref, g_ref, wq_ref, wk_ref, wv_ref, wf_ref, b3_ref, place_ref, vone_ref,
                     tri_ref, q_out, k_out, v_out, carry_ref, *, tm, chunk):
    @pl.when(pl.program_id(1) == 0)
    def _():
        carry_ref[...] = jnp.zeros_like(carry_ref)

    h = _rms(x_ref[0], g_ref[...]).astype(jnp.bfloat16)

    z = _dot(h, wf_ref[...]) + b3_ref[...]
    logf = jnp.minimum(z, 0.0) - jnp.log1p(jnp.exp(-jnp.abs(z)))

    tri = tri_ref[...]
    hi, mid, lo = _split3(logf)
    c = _dot(tri, hi) + _dot(tri, mid) + _dot(tri, lo) + carry_ref[...]
    carry_ref[...] = c[tm - 1:tm, :]

    chi, cmid, clo = (p.astype(jnp.float32) for p in _split3(c * LOG2E))
    lane = lax.broadcasted_iota(jnp.int32, (tm, LANES), 1)
    parts = jnp.where(lane < FOX_HEADS, chi,
                      jnp.where(lane < 2 * FOX_HEADS, cmid,
                                jnp.where(lane < ONES_LANE, clo,
                                          jnp.where(lane == ONES_LANE, 1.0, 0.0))))
    parts = parts.astype(jnp.bfloat16)

    width = FOX_HEADS * HEAD_AUG
    for n in range(width // chunk):
        cols = slice(n * chunk, (n + 1) * chunk)
        kcols = slice(width + n * chunk, width + (n + 1) * chunk)
        q = _dot(h, wq_ref[:, cols]) * (QK_SCALE * LOG2E) + _dot(parts, place_ref[:, cols])
        q_out[0, :, cols] = q.astype(jnp.bfloat16)
        k = _dot(h, wk_ref[:, cols]) + _dot(parts, place_ref[:, kcols])
        k_out[0, :, cols] = k.astype(jnp.bfloat16)
        v = _dot(h, wv_ref[:, cols]) + vone_ref[:, cols]
        v_out[0, :, cols] = v.astype(jnp.bfloat16)


def _fox_placement():
    width = FOX_HEADS * HEAD_AUG
    p = np.zeros((LANES, 2 * width), np.float32)
    for h in range(FOX_HEADS):
        base = h * HEAD_AUG + DEC0
        for s in range(N_SPLIT):
            p[s * FOX_HEADS + h, base + s] = 1.0
            p[ONES_LANE, base + N_SPLIT + s] = 1.0
            p[ONES_LANE, width + base + s] = 1.0
            p[s * FOX_HEADS + h, width + base + N_SPLIT + s] = -1.0
    return p


def _pad_heads(w, heads):
    d = w.shape[0]
    w = w.reshape(d, heads, HEAD_DIM)
    w = jnp.pad(w, ((0, 0), (0, 0), (0, HEAD_AUG - HEAD_DIM)))
    return w.reshape(d, heads * HEAD_AUG)


def _fox_proj(x, g, w_in, b_f):
    B, S, D = x.shape
    tm = PROJ_TM
    hd_all = FOX_HEADS * HEAD_DIM
    width = FOX_HEADS * HEAD_AUG
    wq = _pad_heads(w_in[:, :hd_all], FOX_HEADS).astype(jnp.bfloat16)
    wk = _pad_heads(w_in[:, hd_all:2 * hd_all], FOX_HEADS).astype(jnp.bfloat16)
    wv = _pad_heads(w_in[:, 2 * hd_all:3 * hd_all], FOX_HEADS).astype(jnp.bfloat16)
    wf = w_in[:, 3 * hd_all:]
    wf3 = jnp.concatenate([wf] * N_SPLIT + [jnp.zeros((D, LANES - ONES_LANE), wf.dtype)], axis=1)
    b3 = jnp.concatenate([b_f] * N_SPLIT + [jnp.zeros((LANES - ONES_LANE,), b_f.dtype)])[None, :]
    place = jnp.asarray(_fox_placement(), jnp.bfloat16)
    vone = np.zeros((1, width), np.float32)
    vone[0, HEAD_DIM::HEAD_AUG] = 1.0
    tri = jnp.asarray(np.tril(np.ones((tm, tm), np.float32)), jnp.bfloat16)

    out_sds = jax.ShapeDtypeStruct((B, S, width), jnp.bfloat16)
    out_spec = pl.BlockSpec((1, tm, width), lambda b, t: (b, t, 0))
    return pl.pallas_call(
        functools.partial(_fox_proj_kernel, tm=tm, chunk=512),
        out_shape=(out_sds, out_sds, out_sds),
        grid=(B, S // tm),
        in_specs=[
            pl.BlockSpec((1, tm, D), lambda b, t: (b, t, 0)),
            _const_spec((1, D)),
            _const_spec((D, width)), _const_spec((D, width)), _const_spec((D, width)),
            _const_spec((D, LANES)), _const_spec((1, LANES)),
            _const_spec((LANES, 2 * width)), _const_spec((1, width)),
            _const_spec((tm, tm)),
        ],
        out_specs=(out_spec, out_spec, out_spec),
        scratch_shapes=[pltpu.VMEM((1, LANES), jnp.float32)],
        compiler_params=pltpu.CompilerParams(
            dimension_semantics=("arbitrary", "arbitrary"), vmem_limit_bytes=VMEM_LIMIT),
        name="fox_proj",
    )(x, g[None, :], wq, wk, wv, wf3.astype(jnp.bfloat16), b3, place, jnp.asarray(vone), tri)


def _fox_flash_kernel(q_ref, k_ref, v_ref, o_ref, vt_ref, qt_ref, m_ref, acc_ref, s_ref, p_ref, mb_ref,
                      *, bq, seq):
    qi = pl.program_id(2)
    heads = 2

    @pl.when(qi == 0)
    def _():
        for c in range(seq // bq):
            rows = slice(c * bq, (c + 1) * bq)
            vt_ref[:, rows] = v_ref[0, rows, :].T

    qt_ref[...] = q_ref[0].T
    m_ref[...] = jnp.full_like(m_ref, NEG_BIG)
    acc_ref[...] = jnp.zeros_like(acc_ref)

    def scores(j, slot, masked):
        start = pl.multiple_of(j * bq, bq)
        for hh in range(heads):
            cols = slice(hh * HEAD_AUG, (hh + 1) * HEAD_AUG)
            st = _dot(k_ref[0, pl.ds(start, bq), cols], qt_ref[cols, :])
            if masked:
                key = lax.broadcasted_iota(jnp.int32, (bq, bq), 0)
                qry = lax.broadcasted_iota(jnp.int32, (bq, bq), 1)
                st = jnp.where(key - qry <= (qi - j) * bq, st, NEG_BIG)
            s_ref[2 * slot + hh] = st
            mb_ref[2 * slot + hh] = jnp.max(st, axis=0, keepdims=True)

    def consume(j, slot):
        start = pl.multiple_of(j * bq, bq)
        alpha = []
        for hh in range(heads):
            m_prev = m_ref[hh]
            m_new = jnp.maximum(m_prev, mb_ref[2 * slot + hh])
            m_ref[hh] = m_new
            p_ref[hh] = jnp.exp2(s_ref[2 * slot + hh] - m_new).astype(jnp.bfloat16)
            alpha.append(jnp.exp2(m_prev - m_new))
        for hh in range(heads):
            cols = slice(hh * HEAD_AUG, (hh + 1) * HEAD_AUG)
            vt = vt_ref[cols, pl.ds(start, bq)]
            acc_ref[hh] = alpha[hh] * acc_ref[hh] + _dot(vt, p_ref[hh])

    def half_step(j, slot, masked):
        scores(j + 1, 1 - slot, masked)
        consume(j, slot)

    pairs = jnp.maximum(qi - 1, 0) // 2
    rest = qi - 2 * pairs
    scores(0, 0, True)

    def body(t, carry):
        half_step(2 * t, 0, False)
        half_step(2 * t + 1, 1, False)
        return carry

    lax.fori_loop(0, pairs, body, 0)

    @pl.when(rest >= 1)
    def _():
        half_step(2 * pairs, 0, True)

    @pl.when(rest == 2)
    def _():
        half_step(2 * pairs + 1, 1, True)

    @pl.when(rest == 1)
    def _():
        consume(qi, 1)

    @pl.when(rest != 1)
    def _():
        consume(qi, 0)

    outs = []
    for hh in range(heads):
        acc = acc_ref[hh]
        outs.append((acc * (1.0 / acc[HEAD_DIM:HEAD_DIM + 1, :])).T)
    lane = lax.broadcasted_iota(jnp.int32, (bq, LANES), 1)
    o = jnp.where(lane < HEAD_DIM, outs[0], pltpu.roll(outs[1], HEAD_DIM, 1))
    o_ref[0] = o.astype(jnp.bfloat16)


def _fox_flash(q, k, v):
    B, S, width = q.shape
    bq = FLASH_BQ
    pair = 2 * HEAD_AUG
    return pl.pallas_call(
        functools.partial(_fox_flash_kernel, bq=bq, seq=S),
        out_shape=jax.ShapeDtypeStruct((B, S, FOX_HEADS * HEAD_DIM), jnp.bfloat16),
        grid=(B, FOX_HEADS // 2, S // bq),
        in_specs=[
            pl.BlockSpec((1, bq, pair), lambda b, hp, i: (b, i, hp)),
            pl.BlockSpec((1, S, pair), lambda b, hp, i: (b, 0, hp)),
            pl.BlockSpec((1, S, pair), lambda b, hp, i: (b, 0, hp), pipeline_mode=pl.Buffered(1)),
        ],
        out_specs=pl.BlockSpec((1, bq, 2 * HEAD_DIM), lambda b, hp, i: (b, i, hp)),
        scratch_shapes=[pltpu.VMEM((pair, S), jnp.bfloat16),
                        pltpu.VMEM((pair, bq), jnp.bfloat16),
                        pltpu.VMEM((2, 1, bq), jnp.float32),
                        pltpu.VMEM((2, HEAD_AUG, bq), jnp.float32),
                        pltpu.VMEM((4, bq, bq), jnp.float32),
                        pltpu.VMEM((2, bq, bq), jnp.bfloat16),
                        pltpu.VMEM((4, 1, bq), jnp.float32)],
        compiler_params=pltpu.CompilerParams(
            dimension_semantics=("arbitrary", "arbitrary", "arbitrary"),
            vmem_limit_bytes=VMEM_LIMIT),
        name="fox_flash",
    )(q, k, v)


def _out_mlp_kernel(o_ref, x_ref, wo_ref, g_post_ref, g_pre_ref, wu_ref, wd_ref, g_ffn_ref, y_ref):
    a = _dot(o_ref[...], wo_ref[...])
    x1 = x_ref[...] + _rms(a, g_post_ref[...])
    h = _rms(x1, g_pre_ref[...]).astype(jnp.bfloat16)
    acc = jnp.zeros_like(x1)
    for c in range(D_FF // MLP_CHUNK):
        cols = slice(c * MLP_CHUNK, (c + 1) * MLP_CHUNK)
        u = jnp.maximum(_dot(h, wu_ref[:, cols]), 0.0)
        acc = acc + _dot((u * u).astype(jnp.bfloat16), wd_ref[cols, :])
    y_ref[...] = x1 + _rms(acc, g_ffn_ref[...])


def _out_mlp(o, x, w_out, g_post, g_pre, w_up, w_down, g_ffn):
    T, D = x.shape
    tm = PROJ_TM
    row = lambda width: pl.BlockSpec((tm, width), lambda t: (t, 0))
    return pl.pallas_call(
        _out_mlp_kernel,
        out_shape=jax.ShapeDtypeStruct((T, D), jnp.float32),
        grid=(T // tm,),
        in_specs=[row(o.shape[1]), row(D),
                  _const_spec(w_out.shape), _const_spec((1, D)), _const_spec((1, D)),
                  _const_spec(w_up.shape), _const_spec(w_down.shape), _const_spec((1, D))],
        out_specs=row(D),
        compiler_params=pltpu.CompilerParams(
            dimension_semantics=("arbitrary",), vmem_limit_bytes=VMEM_LIMIT),
        name="out_mlp",
    )(o, x, w_out.astype(jnp.bfloat16), g_post[None, :], g_pre[None, :],
      w_up.astype(jnp.bfloat16), w_down.astype(jnp.bfloat16), g_ffn[None, :])


def _swa_proj_kernel(x_ref, pos_ref, g_ref, wq_ref, wqr_ref, wk_ref, wkr_ref, wv_ref, freq_ref, sign_ref,
                     q_out, k_out, v_out):
    h = _rms(x_ref[...], g_ref[...]).astype(jnp.bfloat16)
    ang = pos_ref[...].astype(jnp.float32) * freq_ref[...]
    cos = jnp.cos(ang)
    sin = jnp.sin(ang) * sign_ref[...]
    for c in range(q_out.shape[1] // LANES):
        cols = slice(c * LANES, (c + 1) * LANES)
        q = _dot(h, wq_ref[:, cols]) * cos + _dot(h, wqr_ref[:, cols]) * sin
        q_out[:, cols] = (q * (QK_SCALE * LOG2E)).astype(jnp.bfloat16)
    for c in range(k_out.shape[1] // LANES):
        cols = slice(c * LANES, (c + 1) * LANES)
        k = _dot(h, wk_ref[:, cols]) * cos + _dot(h, wkr_ref[:, cols]) * sin
        k_out[:, cols] = k.astype(jnp.bfloat16)
        v_out[:, cols] = _dot(h, wv_ref[:, cols]).astype(jnp.bfloat16)


def _rot_half_cols(w, heads):
    d = w.shape[0]
    w = w.reshape(d, heads, 2, HEAD_DIM // 2)
    return w[:, :, ::-1, :].reshape(d, heads * HEAD_DIM)


def _dup_heads(w, heads):
    d = w.shape[0]
    w = w.reshape(d, heads, 1, HEAD_DIM)
    return jnp.broadcast_to(w, (d, heads, 2, HEAD_DIM)).reshape(d, heads * 2 * HEAD_DIM)


def _swa_proj(x, pos, g, w_in):
    T, D = x.shape
    tm = PROJ_TM
    qd = SWA_Q_HEADS * HEAD_DIM
    kvd = SWA_KV_HEADS * HEAD_DIM
    kw = SWA_KV_HEADS * 2 * HEAD_DIM
    bf = lambda a: a.astype(jnp.bfloat16)
    wq = w_in[:, :qd]
    wk = w_in[:, qd:qd + kvd]
    wv = w_in[:, qd + kvd:]
    half = HEAD_DIM // 2
    inv_freq = 1.0 / (ROPE_THETA ** (jnp.arange(0, HEAD_DIM, 2, dtype=jnp.float32) / HEAD_DIM))
    freq = jnp.tile(inv_freq, LANES // half)[None, :]
    sign = np.where((np.arange(LANES) % HEAD_DIM) < half, -1.0, 1.0).astype(np.float32)[None, :]
    row = lambda width: pl.BlockSpec((tm, width), lambda t: (t, 0))
    return pl.pallas_call(
        _swa_proj_kernel,
        out_shape=(jax.ShapeDtypeStruct((T, qd), jnp.bfloat16),
                   jax.ShapeDtypeStruct((T, kw), jnp.bfloat16),
                   jax.ShapeDtypeStruct((T, kw), jnp.bfloat16)),
        grid=(T // tm,),
        in_specs=[row(D), row(1), _const_spec((1, D)),
                  _const_spec((D, qd)), _const_spec((D, qd)),
                  _const_spec((D, kw)), _const_spec((D, kw)), _const_spec((D, kw)),
                  _const_spec((1, LANES)), _const_spec((1, LANES))],
        out_specs=(row(qd), row(kw), row(kw)),
        compiler_params=pltpu.CompilerParams(
            dimension_semantics=("arbitrary",), vmem_limit_bytes=VMEM_LIMIT),
        name="swa_proj",
    )(x, pos, g[None, :], bf(wq), bf(_rot_half_cols(wq, SWA_Q_HEADS)),
      bf(_dup_heads(wk, SWA_KV_HEADS)), bf(_dup_heads(_rot_half_cols(wk, SWA_KV_HEADS), SWA_KV_HEADS)),
      bf(_dup_heads(wv, SWA_KV_HEADS)), freq, jnp.asarray(sign))


def _swa_attn_kernel(sink_ref, q_ref, kp_ref, kc_ref, vp_ref, vc_ref, o_ref):
    n = pl.program_id(1)
    blk = WINDOW
    kb = jnp.concatenate([kp_ref[0], kc_ref[0]], axis=0)
    vb = jnp.concatenate([vp_ref[0], vc_ref[0]], axis=0)
    qi = lax.broadcasted_iota(jnp.int32, (blk, 2 * blk), 0)
    kj = lax.broadcasted_iota(jnp.int32, (blk, 2 * blk), 1)
    diff = qi + blk - kj
    first_key = jnp.where(n > 0, 0, blk)
    mask = (diff >= 0) & (diff < WINDOW) & (kj >= first_key)
    lane = lax.broadcasted_iota(jnp.int32, (blk, LANES), 1)
    low = lane < HEAD_DIM
    for c in range(SWA_Q_HEADS // 2):
        qcol = q_ref[0, :, c * LANES:(c + 1) * LANES].astype(jnp.float32)
        outs = []
        for par in range(2):
            head = 2 * c + par
            g = head // SWA_GROUP
            gcols = slice(g * LANES, (g + 1) * LANES)
            keep = low if par == 0 else jnp.logical_not(low)
            qm = jnp.where(keep, qcol, 0.0).astype(jnp.bfloat16)
            s = jnp.where(mask, _dot_nt(qm, kb[:, gcols]), NEG_BIG)
            sink = sink_ref[head] * LOG2E
            m = jnp.maximum(jnp.max(s, axis=1, keepdims=True), sink)
            e = jnp.exp2(s - m)
            denom = jnp.sum(e, axis=1, keepdims=True) + jnp.exp2(sink - m)
            p = (e * (1.0 / denom)).astype(jnp.bfloat16)
            outs.append(_dot(p, vb[:, gcols]))
        o_ref[0, :, c * LANES:(c + 1) * LANES] = jnp.where(low, outs[0], outs[1]).astype(jnp.bfloat16)


def _swa_attn(q, k, v, sinks):
    B, S, qd = q.shape
    kw = k.shape[2]
    blk = WINDOW
    cur = lambda b, n: (b, n, 0)
    prev = lambda b, n: (b, jnp.maximum(n - 1, 0), 0)
    return pl.pallas_call(
        _swa_attn_kernel,
        out_shape=jax.ShapeDtypeStruct((B, S, qd), jnp.bfloat16),
        grid=(B, S // blk),
        in_specs=[pl.BlockSpec(memory_space=pltpu.SMEM),
                  pl.BlockSpec((1, blk, qd), cur),
                  pl.BlockSpec((1, blk, kw), prev), pl.BlockSpec((1, blk, kw), cur),
                  pl.BlockSpec((1, blk, kw), prev), pl.BlockSpec((1, blk, kw), cur)],
        out_specs=pl.BlockSpec((1, blk, qd), cur),
        compiler_params=pltpu.CompilerParams(
            dimension_semantics=("arbitrary", "arbitrary"), vmem_limit_bytes=VMEM_LIMIT),
        name="swa_attn",
    )(sinks, q, k, k, v, v)


def kernel(x, positions, fox_w_in, fox_b_f, fox_w_out, swa_w_in, swa_sinks, swa_w_out,
           norm_pre_mix, norm_post_mix, norm_pre_ffn, norm_post_ffn, mlp_w_up, mlp_w_down):
    B, S, D = x.shape
    T = B * S
    q, k, v = _fox_proj(x, norm_pre_mix[0], fox_w_in[0], fox_b_f[0])
    o = _fox_flash(q, k, v)
    x1 = _out_mlp(o.reshape(T, -1), x.reshape(T, D), fox_w_out[0], norm_post_mix[0], norm_pre_ffn[0],
                  mlp_w_up[0], mlp_w_down[0], norm_post_ffn[0])
    q, k, v = _swa_proj(x1, positions.reshape(T, 1), norm_pre_mix[1], swa_w_in[0])
    o = _swa_attn(q.reshape(B, S, -1), k.reshape(B, S, -1), v.reshape(B, S, -1), swa_sinks[0])
    x2 = _out_mlp(o.reshape(T, -1), x1, swa_w_out[0], norm_post_mix[1], norm_pre_ffn[1],
                  mlp_w_up[1], mlp_w_down[1], norm_post_ffn[1])
    return x2.reshape(B, S, D)
```

```python
import functools
import math

import jax
import jax.numpy as jnp
import numpy as np
from jax import lax
from jax.experimental import pallas as pl
from jax.experimental.pallas import tpu as pltpu

D_MODEL = 1024
HEAD_DIM = 64
FOX_HEADS = 16
SWA_Q_HEADS = 16
SWA_KV_HEADS = 2
SWA_GROUP = SWA_Q_HEADS // SWA_KV_HEADS
WINDOW = 128
D_FF = 4 * D_MODEL
ROPE_THETA = 10000.0
NORM_EPS = 1e-6

LANES = 128
HEAD_AUG = LANES
LOG2E = 1.4426950408889634
QK_SCALE = HEAD_DIM ** -0.5
NEG_BIG = -1e30
SKIP_LOG2 = 170.0
NORM_SLACK = 1.02
VMEM_LIMIT = 56 * 1024 * 1024

DEC0 = HEAD_DIM
N_SPLIT = 3
ONES_LANE = N_SPLIT * FOX_HEADS

PROJ_TM = 512
FLASH_BQ = 512
MLP_CHUNK = 1024


def _rms(x, g):
    return x * lax.rsqrt(jnp.mean(x * x, axis=-1, keepdims=True) + NORM_EPS) * g


def _dot(a, b):
    return jnp.dot(a, b, preferred_element_type=jnp.float32)


def _dot_nt(a, b):
    return lax.dot_general(a, b, (((1,), (1,)), ((), ())), preferred_element_type=jnp.float32)


def _split3(x):
    hi = x.astype(jnp.bfloat16)
    r = x - hi.astype(jnp.float32)
    mid = r.astype(jnp.bfloat16)
    lo = (r - mid.astype(jnp.float32)).astype(jnp.bfloat16)
    return hi, mid, lo


def _const_spec(shape):
    nd = len(shape)
    return pl.BlockSpec(shape, lambda *_: (0,) * nd, pipeline_mode=pl.Buffered(1))


def _fox_proj_kernel(x_ref, g_ref, wq_ref, wk_ref, wv_ref, wf_ref, b3_ref, place_ref, vone_ref,
                     tri_ref, sel_ref, q_out, k_out, v_out, stats_out, carry_ref, *, tm, chunk):
    @pl.when(pl.program_id(1) == 0)
    def _():
        carry_ref[...] = jnp.zeros_like(carry_ref)

    h = _rms(x_ref[0], g_ref[...]).astype(jnp.bfloat16)

    z = _dot(h, wf_ref[...]) + b3_ref[...]
    logf = jnp.minimum(z, 0.0) - jnp.log1p(jnp.exp(-jnp.abs(z)))

    tri = tri_ref[...]
    hi, mid, lo = _split3(logf)
    c = _dot(tri, hi) + _dot(tri, mid) + _dot(tri, lo) + carry_ref[...]
    carry_ref[...] = c[tm - 1:tm, :]

    chi, cmid, clo = (p.astype(jnp.float32) for p in _split3(c * LOG2E))
    lane = lax.broadcasted_iota(jnp.int32, (tm, LANES), 1)
    parts = jnp.where(lane < FOX_HEADS, chi,
                      jnp.where(lane < 2 * FOX_HEADS, cmid,
                                jnp.where(lane < ONES_LANE, clo,
                                          jnp.where(lane == ONES_LANE, 1.0, 0.0))))
    parts = parts.astype(jnp.bfloat16)

    width = FOX_HEADS * HEAD_AUG
    qn2 = jnp.zeros((tm, LANES), jnp.float32)
    kn2 = jnp.zeros((tm, LANES), jnp.float32)
    for n in range(width // chunk):
        cols = slice(n * chunk, (n + 1) * chunk)
        kcols = slice(width + n * chunk, width + (n + 1) * chunk)
        q = _dot(h, wq_ref[:, cols]) * (QK_SCALE * LOG2E)
        qn2 = qn2 + _dot((q * q).astype(jnp.bfloat16), sel_ref[cols, :])
        q_out[0, :, cols] = (q + _dot(parts, place_ref[:, cols])).astype(jnp.bfloat16)
        k = _dot(h, wk_ref[:, cols])
        kn2 = kn2 + _dot((k * k).astype(jnp.bfloat16), sel_ref[cols, :])
        k_out[0, :, cols] = (k + _dot(parts, place_ref[:, kcols])).astype(jnp.bfloat16)
        v = _dot(h, wv_ref[:, cols]) + vone_ref[:, cols]
        v_out[0, :, cols] = v.astype(jnp.bfloat16)

    c2 = c * LOG2E
    stats_out[0, 0] = jnp.zeros((8, LANES), jnp.float32)
    stats_out[0, 0, 0:1, :] = jnp.sqrt(jnp.max(qn2, axis=0, keepdims=True))
    stats_out[0, 0, 1:2, :] = jnp.sqrt(jnp.max(kn2, axis=0, keepdims=True))
    stats_out[0, 0, 2:3, :] = c2[0:1, :]
    stats_out[0, 0, 3:4, :] = c2[tm - 1:tm, :]


def _fox_placement():
    width = FOX_HEADS * HEAD_AUG
    p = np.zeros((LANES, 2 * width), np.float32)
    for h in range(FOX_HEADS):
        base = h * HEAD_AUG + DEC0
        for s in range(N_SPLIT):
            p[s * FOX_HEADS + h, base + s] = 1.0
            p[ONES_LANE, base + N_SPLIT + s] = 1.0
            p[ONES_LANE, width + base + s] = 1.0
            p[s * FOX_HEADS + h, width + base + N_SPLIT + s] = -1.0
    return p


def _pad_heads(w, heads):
    d = w.shape[0]
    w = w.reshape(d, heads, HEAD_DIM)
    w = jnp.pad(w, ((0, 0), (0, 0), (0, HEAD_AUG - HEAD_DIM)))
    return w.reshape(d, heads * HEAD_AUG)


def _fox_proj(x, g, w_in, b_f):
    B, S, D = x.shape
    tm = PROJ_TM
    hd_all = FOX_HEADS * HEAD_DIM
    width = FOX_HEADS * HEAD_AUG
    wq = _pad_heads(w_in[:, :hd_all], FOX_HEADS).astype(jnp.bfloat16)
    wk = _pad_heads(w_in[:, hd_all:2 * hd_all], FOX_HEADS).astype(jnp.bfloat16)
    wv = _pad_heads(w_in[:, 2 * hd_all:3 * hd_all], FOX_HEADS).astype(jnp.bfloat16)
    wf = w_in[:, 3 * hd_all:]
    wf3 = jnp.concatenate([wf] * N_SPLIT + [jnp.zeros((D, LANES - ONES_LANE), wf.dtype)], axis=1)
    b3 = jnp.concatenate([b_f] * N_SPLIT + [jnp.zeros((LANES - ONES_LANE,), b_f.dtype)])[None, :]
    place = jnp.asarray(_fox_placement(), jnp.bfloat16)
    vone = np.zeros((1, width), np.float32)
    vone[0, HEAD_DIM::HEAD_AUG] = 1.0
    tri = jnp.asarray(np.tril(np.ones((tm, tm), np.float32)), jnp.bfloat16)
    sel = np.zeros((width, LANES), np.float32)
    for hd in range(FOX_HEADS):
        sel[hd * HEAD_AUG:hd * HEAD_AUG + HEAD_DIM, hd] = 1.0

    out_sds = jax.ShapeDtypeStruct((B, S, width), jnp.bfloat16)
    out_spec = pl.BlockSpec((1, tm, width), lambda b, t: (b, t, 0))
    stats_sds = jax.ShapeDtypeStruct((B, S // tm, 8, LANES), jnp.float32)
    stats_spec = pl.BlockSpec((1, 1, 8, LANES), lambda b, t: (b, t, 0, 0))
    return pl.pallas_call(
        functools.partial(_fox_proj_kernel, tm=tm, chunk=512),
        out_shape=(out_sds, out_sds, out_sds, stats_sds),
        grid=(B, S // tm),
        in_specs=[
            pl.BlockSpec((1, tm, D), lambda b, t: (b, t, 0)),
            _const_spec((1, D)),
            _const_spec((D, width)), _const_spec((D, width)), _const_spec((D, width)),
            _const_spec((D, LANES)), _const_spec((1, LANES)),
            _const_spec((LANES, 2 * width)), _const_spec((1, width)),
            _const_spec((tm, tm)), _const_spec((width, LANES)),
        ],
        out_specs=(out_spec, out_spec, out_spec, stats_spec),
        scratch_shapes=[pltpu.VMEM((1, LANES), jnp.float32)],
        compiler_params=pltpu.CompilerParams(
            dimension_semantics=("arbitrary", "arbitrary"), vmem_limit_bytes=VMEM_LIMIT),
        name="fox_proj",
    )(x, g[None, :], wq, wk, wv, wf3.astype(jnp.bfloat16), b3, place, jnp.asarray(vone), tri,
      jnp.asarray(sel, jnp.bfloat16))


def _fox_flash_kernel(first_ref, q_ref, k_ref, v_ref, o_ref, vt_ref, qt_ref, m_ref, acc_ref, s_ref, p_ref,
                      mb_ref, *, bq, seq):
    qi = pl.program_id(2)
    heads = 2
    nq = seq // bq
    j0 = first_ref[(pl.program_id(0) * pl.num_programs(1) + pl.program_id(1)) * nq + qi]

    @pl.when(qi == 0)
    def _():
        for c in range(seq // bq):
            rows = slice(c * bq, (c + 1) * bq)
            vt_ref[:, rows] = v_ref[0, rows, :].T

    qt_ref[...] = q_ref[0].T
    m_ref[...] = jnp.full_like(m_ref, NEG_BIG)
    acc_ref[...] = jnp.zeros_like(acc_ref)

    def scores(j, slot, masked):
        start = pl.multiple_of(j * bq, bq)
        for hh in range(heads):
            cols = slice(hh * HEAD_AUG, (hh + 1) * HEAD_AUG)
            st = _dot(k_ref[0, pl.ds(start, bq), cols], qt_ref[cols, :])
            if masked:
                key = lax.broadcasted_iota(jnp.int32, (bq, bq), 0)
                qry = lax.broadcasted_iota(jnp.int32, (bq, bq), 1)
                st = jnp.where(key - qry <= (qi - j) * bq, st, NEG_BIG)
            s_ref[2 * slot + hh] = st
            mb_ref[2 * slot + hh] = jnp.max(st, axis=0, keepdims=True)

    def consume(j, slot):
        start = pl.multiple_of(j * bq, bq)
        alpha = []
        for hh in range(heads):
            m_prev = m_ref[hh]
            m_new = jnp.maximum(m_prev, mb_ref[2 * slot + hh])
            m_ref[hh] = m_new
            p_ref[hh] = jnp.exp2(s_ref[2 * slot + hh] - m_new).astype(jnp.bfloat16)
            alpha.append(jnp.exp2(m_prev - m_new))
        for hh in range(heads):
            cols = slice(hh * HEAD_AUG, (hh + 1) * HEAD_AUG)
            vt = vt_ref[cols, pl.ds(start, bq)]
            acc_ref[hh] = alpha[hh] * acc_ref[hh] + _dot(vt, p_ref[hh])

    def half_step(j, slot, masked):
        scores(j + 1, 1 - slot, masked)
        consume(j, slot)

    span = qi - j0
    pairs = jnp.maximum(span - 1, 0) // 2
    rest = span - 2 * pairs
    scores(j0, 0, True)

    def body(t, carry):
        half_step(j0 + 2 * t, 0, False)
        half_step(j0 + 2 * t + 1, 1, False)
        return carry

    lax.fori_loop(0, pairs, body, 0)

    @pl.when(rest >= 1)
    def _():
        half_step(j0 + 2 * pairs, 0, True)

    @pl.when(rest == 2)
    def _():
        half_step(j0 + 2 * pairs + 1, 1, True)

    @pl.when(rest == 1)
    def _():
        consume(qi, 1)

    @pl.when(rest != 1)
    def _():
        consume(qi, 0)

    outs = []
    for hh in range(heads):
        acc = acc_ref[hh]
        outs.append((acc * (1.0 / acc[HEAD_DIM:HEAD_DIM + 1, :])).T)
    lane = lax.broadcasted_iota(jnp.int32, (bq, LANES), 1)
    o = jnp.where(lane < HEAD_DIM, outs[0], pltpu.roll(outs[1], HEAD_DIM, 1))
    o_ref[0] = o.astype(jnp.bfloat16)


def _first_key_block(stats):
    qmax, kmax = stats[:, :, 0, :FOX_HEADS], stats[:, :, 1, :FOX_HEADS]
    c_first, c_last = stats[:, :, 2, :FOX_HEADS], stats[:, :, 3, :FOX_HEADS]
    kall = jnp.max(kmax, axis=1, keepdims=True)
    thr = c_first + SKIP_LOG2 + NORM_SLACK * 2.0 * qmax * kall
    skippable = c_last[:, None, :, :] > thr[:, :, None, :]
    first = jnp.sum(skippable.astype(jnp.int32), axis=2)
    nq = stats.shape[1]
    first = jnp.minimum(first, jnp.arange(nq, dtype=jnp.int32)[None, :, None])
    first = jnp.min(first.reshape(first.shape[0], nq, FOX_HEADS // 2, 2), axis=-1)
    return first.transpose(0, 2, 1).reshape(-1)


def _fox_flash(q, k, v, stats):
    B, S, width = q.shape
    bq = FLASH_BQ
    pair = 2 * HEAD_AUG
    grid_spec = pltpu.PrefetchScalarGridSpec(
        num_scalar_prefetch=1,
        grid=(B, FOX_HEADS // 2, S // bq),
        in_specs=[
            pl.BlockSpec((1, bq, pair), lambda b, hp, i, first: (b, i, hp)),
            pl.BlockSpec((1, S, pair), lambda b, hp, i, first: (b, 0, hp)),
            pl.BlockSpec((1, S, pair), lambda b, hp, i, first: (b, 0, hp), pipeline_mode=pl.Buffered(1)),
        ],
        out_specs=pl.BlockSpec((1, bq, 2 * HEAD_DIM), lambda b, hp, i, first: (b, i, hp)),
        scratch_shapes=[pltpu.VMEM((pair, S), jnp.bfloat16),
                        pltpu.VMEM((pair, bq), jnp.bfloat16),
                        pltpu.VMEM((2, 1, bq), jnp.float32),
                        pltpu.VMEM((2, HEAD_AUG, bq), jnp.float32),
                        pltpu.VMEM((4, bq, bq), jnp.float32),
                        pltpu.VMEM((2, bq, bq), jnp.bfloat16),
                        pltpu.VMEM((4, 1, bq), jnp.float32)])
    return pl.pallas_call(
        functools.partial(_fox_flash_kernel, bq=bq, seq=S),
        out_shape=jax.ShapeDtypeStruct((B, S, FOX_HEADS * HEAD_DIM), jnp.bfloat16),
        grid_spec=grid_spec,
        compiler_params=pltpu.CompilerParams(
            dimension_semantics=("arbitrary", "arbitrary", "arbitrary"),
            vmem_limit_bytes=VMEM_LIMIT),
        name="fox_flash",
    )(_first_key_block(stats), q, k, v)


def _out_mlp_kernel(o_ref, x_ref, wo_ref, g_post_ref, g_pre_ref, wu_ref, wd_ref, g_ffn_ref, y_ref):
    a = _dot(o_ref[...], wo_ref[...])
    x1 = x_ref[...] + _rms(a, g_post_ref[...])
    h = _rms(x1, g_pre_ref[...]).astype(jnp.bfloat16)
    acc = jnp.zeros_like(x1)
    for c in range(D_FF // MLP_CHUNK):
        cols = slice(c * MLP_CHUNK, (c + 1) * MLP_CHUNK)
        u = jnp.maximum(_dot(h, wu_ref[:, cols]), 0.0)
        acc = acc + _dot((u * u).astype(jnp.bfloat16), wd_ref[cols, :])
    y_ref[...] = x1 + _rms(acc, g_ffn_ref[...])


def _out_mlp(o, x, w_out, g_post, g_pre, w_up, w_down, g_ffn):
    T, D = x.shape
    tm = PROJ_TM
    row = lambda width: pl.BlockSpec((tm, width), lambda t: (t, 0))
    return pl.pallas_call(
        _out_mlp_kernel,
        out_shape=jax.ShapeDtypeStruct((T, D), jnp.float32),
        grid=(T // tm,),
        in_specs=[row(o.shape[1]), row(D),
                  _const_spec(w_out.shape), _const_spec((1, D)), _const_spec((1, D)),
                  _const_spec(w_up.shape), _const_spec(w_down.shape), _const_spec((1, D))],
        out_specs=row(D),
        compiler_params=pltpu.CompilerParams(
            dimension_semantics=("arbitrary",), vmem_limit_bytes=VMEM_LIMIT),
        name="out_mlp",
    )(o, x, w_out.astype(jnp.bfloat16), g_post[None, :], g_pre[None, :],
      w_up.astype(jnp.bfloat16), w_down.astype(jnp.bfloat16), g_ffn[None, :])


def _swa_proj_kernel(x_ref, pos_ref, g_ref, wq_ref, wqr_ref, wk_ref, wkr_ref, wv_ref, freq_ref, sign_ref,
                     q_out, k_out, v_out):
    h = _rms(x_ref[...], g_ref[...]).astype(jnp.bfloat16)
    ang = pos_ref[...].astype(jnp.float32) * freq_ref[...]
    cos = jnp.cos(ang)
    sin = jnp.sin(ang) * sign_ref[...]
    for c in range(q_out.shape[1] // LANES):
        cols = slice(c * LANES, (c + 1) * LANES)
        q = _dot(h, wq_ref[:, cols]) * cos + _dot(h, wqr_ref[:, cols]) * sin
        q_out[:, cols] = (q * (QK_SCALE * LOG2E)).astype(jnp.bfloat16)
    for c in range(k_out.shape[1] // LANES):
        cols = slice(c * LANES, (c + 1) * LANES)
        k = _dot(h, wk_ref[:, cols]) * cos + _dot(h, wkr_ref[:, cols]) * sin
        k_out[:, cols] = k.astype(jnp.bfloat16)
        v_out[:, cols] = _dot(h, wv_ref[:, cols]).astype(jnp.bfloat16)


def _rot_half_cols(w, heads):
    d = w.shape[0]
    w = w.reshape(d, heads, 2, HEAD_DIM // 2)
    return w[:, :, ::-1, :].reshape(d, heads * HEAD_DIM)


def _dup_heads(w, heads):
    d = w.shape[0]
    w = w.reshape(d, heads, 1, HEAD_DIM)
    return jnp.broadcast_to(w, (d, heads, 2, HEAD_DIM)).reshape(d, heads * 2 * HEAD_DIM)


def _swa_proj(x, pos, g, w_in):
    T, D = x.shape
    tm = PROJ_TM
    qd = SWA_Q_HEADS * HEAD_DIM
    kvd = SWA_KV_HEADS * HEAD_DIM
    kw = SWA_KV_HEADS * 2 * HEAD_DIM
    bf = lambda a: a.astype(jnp.bfloat16)
    wq = w_in[:, :qd]
    wk = w_in[:, qd:qd + kvd]
    wv = w_in[:, qd + kvd:]
    half = HEAD_DIM // 2
    inv_freq = 1.0 / (ROPE_THETA ** (jnp.arange(0, HEAD_DIM, 2, dtype=jnp.float32) / HEAD_DIM))
    freq = jnp.tile(inv_freq, LANES // half)[None, :]
    sign = np.where((np.arange(LANES) % HEAD_DIM) < half, -1.0, 1.0).astype(np.float32)[None, :]
    row = lambda width: pl.BlockSpec((tm, width), lambda t: (t, 0))
    return pl.pallas_call(
        _swa_proj_kernel,
        out_shape=(jax.ShapeDtypeStruct((T, qd), jnp.bfloat16),
                   jax.ShapeDtypeStruct((T, kw), jnp.bfloat16),
                   jax.ShapeDtypeStruct((T, kw), jnp.bfloat16)),
        grid=(T // tm,),
        in_specs=[row(D), row(1), _const_spec((1, D)),
                  _const_spec((D, qd)), _const_spec((D, qd)),
                  _const_spec((D, kw)), _const_spec((D, kw)), _const_spec((D, kw)),
                  _const_spec((1, LANES)), _const_spec((1, LANES))],
        out_specs=(row(qd), row(kw), row(kw)),
        compiler_params=pltpu.CompilerParams(
            dimension_semantics=("arbitrary",), vmem_limit_bytes=VMEM_LIMIT),
        name="swa_proj",
    )(x, pos, g[None, :], bf(wq), bf(_rot_half_cols(wq, SWA_Q_HEADS)),
      bf(_dup_heads(wk, SWA_KV_HEADS)), bf(_dup_heads(_rot_half_cols(wk, SWA_KV_HEADS), SWA_KV_HEADS)),
      bf(_dup_heads(wv, SWA_KV_HEADS)), freq, jnp.asarray(sign))


def _swa_attn_kernel(sink_ref, q_ref, kp_ref, kc_ref, vp_ref, vc_ref, o_ref):
    n = pl.program_id(1)
    blk = WINDOW
    kb = jnp.concatenate([kp_ref[0], kc_ref[0]], axis=0)
    vb = jnp.concatenate([vp_ref[0], vc_ref[0]], axis=0)
    qi = lax.broadcasted_iota(jnp.int32, (blk, 2 * blk), 0)
    kj = lax.broadcasted_iota(jnp.int32, (blk, 2 * blk), 1)
    diff = qi + blk - kj
    first_key = jnp.where(n > 0, 0, blk)
    mask = (diff >= 0) & (diff < WINDOW) & (kj >= first_key)
    lane = lax.broadcasted_iota(jnp.int32, (blk, LANES), 1)
    low = lane < HEAD_DIM
    for c in range(SWA_Q_HEADS // 2):
        qcol = q_ref[0, :, c * LANES:(c + 1) * LANES].astype(jnp.float32)
        outs = []
        for par in range(2):
            head = 2 * c + par
            g = head // SWA_GROUP
            gcols = slice(g * LANES, (g + 1) * LANES)
            keep = low if par == 0 else jnp.logical_not(low)
            qm = jnp.where(keep, qcol, 0.0).astype(jnp.bfloat16)
            s = jnp.where(mask, _dot_nt(qm, kb[:, gcols]), NEG_BIG)
            sink = sink_ref[head] * LOG2E
            m = jnp.maximum(jnp.max(s, axis=1, keepdims=True), sink)
            e = jnp.exp2(s - m)
            denom = jnp.sum(e, axis=1, keepdims=True) + jnp.exp2(sink - m)
            p = (e * (1.0 / denom)).astype(jnp.bfloat16)
            outs.append(_dot(p, vb[:, gcols]))
        o_ref[0, :, c * LANES:(c + 1) * LANES] = jnp.where(low, outs[0], outs[1]).astype(jnp.bfloat16)


def _swa_attn(q, k, v, sinks):
    B, S, qd = q.shape
    kw = k.shape[2]
    blk = WINDOW
    cur = lambda b, n: (b, n, 0)
    prev = lambda b, n: (b, jnp.maximum(n - 1, 0), 0)
    return pl.pallas_call(
        _swa_attn_kernel,
        out_shape=jax.ShapeDtypeStruct((B, S, qd), jnp.bfloat16),
        grid=(B, S // blk),
        in_specs=[pl.BlockSpec(memory_space=pltpu.SMEM),
                  pl.BlockSpec((1, blk, qd), cur),
                  pl.BlockSpec((1, blk, kw), prev), pl.BlockSpec((1, blk, kw), cur),
                  pl.BlockSpec((1, blk, kw), prev), pl.BlockSpec((1, blk, kw), cur)],
        out_specs=pl.BlockSpec((1, blk, qd), cur),
        compiler_params=pltpu.CompilerParams(
            dimension_semantics=("arbitrary", "arbitrary"), vmem_limit_bytes=VMEM_LIMIT),
        name="swa_attn",
    )(sinks, q, k, k, v, v)


def kernel(x, positions, fox_w_in, fox_b_f, fox_w_out, swa_w_in, swa_sinks, swa_w_out,
           norm_pre_mix, norm_post_mix, norm_pre_ffn, norm_post_ffn, mlp_w_up, mlp_w_down):
    B, S, D = x.shape
    T = B * S
    q, k, v, stats = _fox_proj(x, norm_pre_mix[0], fox_w_in[0], fox_b_f[0])
    o = _fox_flash(q, k, v, stats)
    x1 = _out_mlp(o.reshape(T, -1), x.reshape(T, D), fox_w_out[0], norm_post_mix[0], norm_pre_ffn[0],
                  mlp_w_up[0], mlp_w_down[0], norm_post_ffn[0])
    q, k, v = _swa_proj(x1, positions.reshape(T, 1), norm_pre_mix[1], swa_w_in[0])
    o = _swa_attn(q.reshape(B, S, -1), k.reshape(B, S, -1), v.reshape(B, S, -1), swa_sinks[0])
    x2 = _out_mlp(o.reshape(T, -1), x1, swa_w_out[0], norm_post_mix[1], norm_pre_ffn[1],
                  mlp_w_up[1], mlp_w_down[1], norm_post_ffn[1])
    return x2.reshape(B, S, D)
```

```python
import functools
import math

import jax
import jax.numpy as jnp
import numpy as np
from jax import lax
from jax.experimental import pallas as pl
from jax.experimental.pallas import tpu as pltpu

D_MODEL = 1024
HEAD_DIM = 64
FOX_HEADS = 16
SWA_Q_HEADS = 16
SWA_KV_HEADS = 2
SWA_GROUP = SWA_Q_HEADS // SWA_KV_HEADS
WINDOW = 128
D_FF = 4 * D_MODEL
ROPE_THETA = 10000.0
NORM_EPS = 1e-6

LANES = 128
HEAD_AUG = LANES
LOG2E = 1.4426950408889634
QK_SCALE = HEAD_DIM ** -0.5
NEG_BIG = -1e30
SKIP_LOG2 = 170.0
NORM_SLACK = 1.02
VMEM_LIMIT = 56 * 1024 * 1024

DEC0 = HEAD_DIM
N_SPLIT = 3
ONES_LANE = N_SPLIT * FOX_HEADS

PROJ_TM = 512
FLASH_BQ = 512
MLP_CHUNK = 1024


def _rms(x, g):
    return x * lax.rsqrt(jnp.mean(x * x, axis=-1, keepdims=True) + NORM_EPS) * g


def _dot(a, b):
    return jnp.dot(a, b, preferred_element_type=jnp.float32)


def _dot_nt(a, b):
    return lax.dot_general(a, b, (((1,), (1,)), ((), ())), preferred_element_type=jnp.float32)


def _split3(x):
    hi = x.astype(jnp.bfloat16)
    r = x - hi.astype(jnp.float32)
    mid = r.astype(jnp.bfloat16)
    lo = (r - mid.astype(jnp.float32)).astype(jnp.bfloat16)
    return hi, mid, lo


def _const_spec(shape):
    nd = len(shape)
    return pl.BlockSpec(shape, lambda *_: (0,) * nd, pipeline_mode=pl.Buffered(1))


def _fox_proj_kernel(x_ref, g_ref, wq_ref, wk_ref, wv_ref, wf_ref, b3_ref, place_ref,
                     tri_ref, sel_ref, q_out, k_out, v_out, stats_out, carry_ref, *, tm):
    @pl.when(pl.program_id(1) == 0)
    def _():
        carry_ref[...] = jnp.zeros_like(carry_ref)

    h = _rms(x_ref[0], g_ref[...]).astype(jnp.bfloat16)

    z = _dot(h, wf_ref[...]) + b3_ref[...]
    logf = jnp.minimum(z, 0.0) - jnp.log1p(jnp.exp(-jnp.abs(z)))

    tri = tri_ref[...]
    hi, mid, lo = _split3(logf)
    c = _dot(tri, hi) + _dot(tri, mid) + _dot(tri, lo) + carry_ref[...]
    carry_ref[...] = c[tm - 1:tm, :]

    chi, cmid, clo = (p.astype(jnp.float32) for p in _split3(c * LOG2E))
    lane = lax.broadcasted_iota(jnp.int32, (tm, LANES), 1)
    parts = jnp.where(lane < FOX_HEADS, chi,
                      jnp.where(lane < 2 * FOX_HEADS, cmid,
                                jnp.where(lane < ONES_LANE, clo,
                                          jnp.where(lane == ONES_LANE, 1.0, 0.0))))
    parts = parts.astype(jnp.bfloat16)

    width = FOX_HEADS * HEAD_AUG
    q = _dot(h, wq_ref[...]) * (QK_SCALE * LOG2E)
    k = _dot(h, wk_ref[...])
    v = _dot(h, wv_ref[...])
    qn2 = _dot((q * q).astype(jnp.bfloat16), sel_ref[...])
    kn2 = _dot((k * k).astype(jnp.bfloat16), sel_ref[...])

    low = lane < HEAD_DIM
    v_tail = jnp.where(lane == HEAD_DIM, 1.0, 0.0)
    for pr in range(FOX_HEADS // 2):
        pcols = slice(pr * LANES, (pr + 1) * LANES)
        acols = slice(2 * pr * HEAD_AUG, (2 * pr + 2) * HEAD_AUG)
        q_tail = _dot(parts, place_ref[:, acols])
        k_tail = _dot(parts, place_ref[:, width + 2 * pr * HEAD_AUG:width + (2 * pr + 2) * HEAD_AUG])
        for src, tail, out in ((q[:, pcols], q_tail, q_out), (k[:, pcols], k_tail, k_out),
                               (v[:, pcols], None, v_out)):
            for par, head in enumerate((src, pltpu.roll(src, HEAD_DIM, 1))):
                fill = v_tail if tail is None else tail[:, par * HEAD_AUG:(par + 1) * HEAD_AUG]
                hcols = slice((2 * pr + par) * HEAD_AUG, (2 * pr + par + 1) * HEAD_AUG)
                out[0, :, hcols] = jnp.where(low, head, fill).astype(jnp.bfloat16)

    c2 = c * LOG2E
    stats_out[0, 0] = jnp.zeros((8, LANES), jnp.float32)
    stats_out[0, 0, 0:1, :] = jnp.sqrt(jnp.max(qn2, axis=0, keepdims=True))
    stats_out[0, 0, 1:2, :] = jnp.sqrt(jnp.max(kn2, axis=0, keepdims=True))
    stats_out[0, 0, 2:3, :] = c2[0:1, :]
    stats_out[0, 0, 3:4, :] = c2[tm - 1:tm, :]


def _fox_placement():
    width = FOX_HEADS * HEAD_AUG
    p = np.zeros((LANES, 2 * width), np.float32)
    for h in range(FOX_HEADS):
        base = h * HEAD_AUG + DEC0
        for s in range(N_SPLIT):
            p[s * FOX_HEADS + h, base + s] = 1.0
            p[ONES_LANE, base + N_SPLIT + s] = 1.0
            p[ONES_LANE, width + base + s] = 1.0
            p[s * FOX_HEADS + h, width + base + N_SPLIT + s] = -1.0
    return p


def _fox_proj(x, g, w_in, b_f):
    B, S, D = x.shape
    tm = PROJ_TM
    hd_all = FOX_HEADS * HEAD_DIM
    width = FOX_HEADS * HEAD_AUG
    wq = w_in[:, :hd_all].astype(jnp.bfloat16)
    wk = w_in[:, hd_all:2 * hd_all].astype(jnp.bfloat16)
    wv = w_in[:, 2 * hd_all:3 * hd_all].astype(jnp.bfloat16)
    wf = w_in[:, 3 * hd_all:]
    wf3 = jnp.concatenate([wf] * N_SPLIT + [jnp.zeros((D, LANES - ONES_LANE), wf.dtype)], axis=1)
    b3 = jnp.concatenate([b_f] * N_SPLIT + [jnp.zeros((LANES - ONES_LANE,), b_f.dtype)])[None, :]
    place = jnp.asarray(_fox_placement(), jnp.bfloat16)
    tri = jnp.asarray(np.tril(np.ones((tm, tm), np.float32)), jnp.bfloat16)
    sel = np.zeros((hd_all, LANES), np.float32)
    for hd in range(FOX_HEADS):
        sel[hd * HEAD_DIM:(hd + 1) * HEAD_DIM, hd] = 1.0

    out_sds = jax.ShapeDtypeStruct((B, S, width), jnp.bfloat16)
    out_spec = pl.BlockSpec((1, tm, width), lambda b, t: (b, t, 0))
    stats_sds = jax.ShapeDtypeStruct((B, S // tm, 8, LANES), jnp.float32)
    stats_spec = pl.BlockSpec((1, 1, 8, LANES), lambda b, t: (b, t, 0, 0))
    return pl.pallas_call(
        functools.partial(_fox_proj_kernel, tm=tm),
        out_shape=(out_sds, out_sds, out_sds, stats_sds),
        grid=(B, S // tm),
        in_specs=[
            pl.BlockSpec((1, tm, D), lambda b, t: (b, t, 0)),
            _const_spec((1, D)),
            _const_spec((D, hd_all)), _const_spec((D, hd_all)), _const_spec((D, hd_all)),
            _const_spec((D, LANES)), _const_spec((1, LANES)),
            _const_spec((LANES, 2 * width)),
            _const_spec((tm, tm)), _const_spec((hd_all, LANES)),
        ],
        out_specs=(out_spec, out_spec, out_spec, stats_spec),
        scratch_shapes=[pltpu.VMEM((1, LANES), jnp.float32)],
        compiler_params=pltpu.CompilerParams(
            dimension_semantics=("arbitrary", "arbitrary"), vmem_limit_bytes=VMEM_LIMIT),
        name="fox_proj",
    )(x, g[None, :], wq, wk, wv, wf3.astype(jnp.bfloat16), b3, place, tri,
      jnp.asarray(sel, jnp.bfloat16))


def _fox_flash_kernel(first_ref, q_ref, k_ref, v_ref, o_ref, vt_ref, qt_ref, m_ref, acc_ref, s_ref, p_ref,
                      mb_ref, *, bq, seq):
    qi = pl.program_id(2)
    heads = 2
    nq = seq // bq
    j0 = first_ref[(pl.program_id(0) * pl.num_programs(1) + pl.program_id(1)) * nq + qi]

    @pl.when(qi == 0)
    def _():
        for c in range(seq // bq):
            rows = slice(c * bq, (c + 1) * bq)
            vt_ref[:, rows] = v_ref[0, rows, :].T

    qt_ref[...] = q_ref[0].T
    m_ref[...] = jnp.full_like(m_ref, NEG_BIG)
    acc_ref[...] = jnp.zeros_like(acc_ref)

    def scores(j, slot, masked):
        start = pl.multiple_of(j * bq, bq)
        for hh in range(heads):
            cols = slice(hh * HEAD_AUG, (hh + 1) * HEAD_AUG)
            st = _dot(k_ref[0, pl.ds(start, bq), cols], qt_ref[cols, :])
            if masked:
                key = lax.broadcasted_iota(jnp.int32, (bq, bq), 0)
                qry = lax.broadcasted_iota(jnp.int32, (bq, bq), 1)
                st = jnp.where(key - qry <= (qi - j) * bq, st, NEG_BIG)
            s_ref[2 * slot + hh] = st
            mb_ref[2 * slot + hh] = jnp.max(st, axis=0, keepdims=True)

    def consume(j, slot):
        start = pl.multiple_of(j * bq, bq)
        alpha = []
        for hh in range(heads):
            m_prev = m_ref[hh]
            m_new = jnp.maximum(m_prev, mb_ref[2 * slot + hh])
            m_ref[hh] = m_new
            p_ref[hh] = jnp.exp2(s_ref[2 * slot + hh] - m_new).astype(jnp.bfloat16)
            alpha.append(jnp.exp2(m_prev - m_new))
        for hh in range(heads):
            cols = slice(hh * HEAD_AUG, (hh + 1) * HEAD_AUG)
            vt = vt_ref[cols, pl.ds(start, bq)]
            acc_ref[hh] = alpha[hh] * acc_ref[hh] + _dot(vt, p_ref[hh])

    def half_step(j, slot, masked):
        scores(j + 1, 1 - slot, masked)
        consume(j, slot)

    span = qi - j0
    pairs = jnp.maximum(span - 1, 0) // 2
    rest = span - 2 * pairs
    scores(j0, 0, True)

    def body(t, carry):
        half_step(j0 + 2 * t, 0, False)
        half_step(j0 + 2 * t + 1, 1, False)
        return carry

    lax.fori_loop(0, pairs, body, 0)

    @pl.when(rest >= 1)
    def _():
        half_step(j0 + 2 * pairs, 0, True)

    @pl.when(rest == 2)
    def _():
        half_step(j0 + 2 * pairs + 1, 1, True)

    @pl.when(rest == 1)
    def _():
        consume(qi, 1)

    @pl.when(rest != 1)
    def _():
        consume(qi, 0)

    outs = []
    for hh in range(heads):
        acc = acc_ref[hh]
        outs.append((acc * (1.0 / acc[HEAD_DIM:HEAD_DIM + 1, :])).T)
    lane = lax.broadcasted_iota(jnp.int32, (bq, LANES), 1)
    o = jnp.where(lane < HEAD_DIM, outs[0], pltpu.roll(outs[1], HEAD_DIM, 1))
    o_ref[0] = o.astype(jnp.bfloat16)


def _first_key_block(stats):
    qmax, kmax = stats[:, :, 0, :FOX_HEADS], stats[:, :, 1, :FOX_HEADS]
    c_first, c_last = stats[:, :, 2, :FOX_HEADS], stats[:, :, 3, :FOX_HEADS]
    kall = jnp.max(kmax, axis=1, keepdims=True)
    thr = c_first + SKIP_LOG2 + NORM_SLACK * 2.0 * qmax * kall
    skippable = c_last[:, None, :, :] > thr[:, :, None, :]
    first = jnp.sum(skippable.astype(jnp.int32), axis=2)
    nq = stats.shape[1]
    first = jnp.minimum(first, jnp.arange(nq, dtype=jnp.int32)[None, :, None])
    first = jnp.min(first.reshape(first.shape[0], nq, FOX_HEADS // 2, 2), axis=-1)
    return first.transpose(0, 2, 1).reshape(-1)


def _fox_flash(q, k, v, stats):
    B, S, width = q.shape
    bq = FLASH_BQ
    pair = 2 * HEAD_AUG
    grid_spec = pltpu.PrefetchScalarGridSpec(
        num_scalar_prefetch=1,
        grid=(B, FOX_HEADS // 2, S // bq),
        in_specs=[
            pl.BlockSpec((1, bq, pair), lambda b, hp, i, first: (b, i, hp)),
            pl.BlockSpec((1, S, pair), lambda b, hp, i, first: (b, 0, hp)),
            pl.BlockSpec((1, S, pair), lambda b, hp, i, first: (b, 0, hp), pipeline_mode=pl.Buffered(1)),
        ],
        out_specs=pl.BlockSpec((1, bq, 2 * HEAD_DIM), lambda b, hp, i, first: (b, i, hp)),
        scratch_shapes=[pltpu.VMEM((pair, S), jnp.bfloat16),
                        pltpu.VMEM((pair, bq), jnp.bfloat16),
                        pltpu.VMEM((2, 1, bq), jnp.float32),
                        pltpu.VMEM((2, HEAD_AUG, bq), jnp.float32),
                        pltpu.VMEM((4, bq, bq), jnp.float32),
                        pltpu.VMEM((2, bq, bq), jnp.bfloat16),
                        pltpu.VMEM((4, 1, bq), jnp.float32)])
    return pl.pallas_call(
        functools.partial(_fox_flash_kernel, bq=bq, seq=S),
        out_shape=jax.ShapeDtypeStruct((B, S, FOX_HEADS * HEAD_DIM), jnp.bfloat16),
        grid_spec=grid_spec,
        compiler_params=pltpu.CompilerParams(
            dimension_semantics=("arbitrary", "arbitrary", "arbitrary"),
            vmem_limit_bytes=VMEM_LIMIT),
        name="fox_flash",
    )(_first_key_block(stats), q, k, v)


def _out_mlp_kernel(o_ref, x_ref, wo_ref, g_post_ref, g_pre_ref, wu_ref, wd_ref, g_ffn_ref, y_ref):
    a = _dot(o_ref[...], wo_ref[...])
    x1 = x_ref[...] + _rms(a, g_post_ref[...])
    h = _rms(x1, g_pre_ref[...]).astype(jnp.bfloat16)
    acc = jnp.zeros_like(x1)
    for c in range(D_FF // MLP_CHUNK):
        cols = slice(c * MLP_CHUNK, (c + 1) * MLP_CHUNK)
        u = jnp.maximum(_dot(h, wu_ref[:, cols]), 0.0)
        acc = acc + _dot((u * u).astype(jnp.bfloat16), wd_ref[cols, :])
    y_ref[...] = x1 + _rms(acc, g_ffn_ref[...])


def _out_mlp(o, x, w_out, g_post, g_pre, w_up, w_down, g_ffn):
    T, D = x.shape
    tm = PROJ_TM
    row = lambda width: pl.BlockSpec((tm, width), lambda t: (t, 0))
    return pl.pallas_call(
        _out_mlp_kernel,
        out_shape=jax.ShapeDtypeStruct((T, D), jnp.float32),
        grid=(T // tm,),
        in_specs=[row(o.shape[1]), row(D),
                  _const_spec(w_out.shape), _const_spec((1, D)), _const_spec((1, D)),
                  _const_spec(w_up.shape), _const_spec(w_down.shape), _const_spec((1, D))],
        out_specs=row(D),
        compiler_params=pltpu.CompilerParams(
            dimension_semantics=("arbitrary",), vmem_limit_bytes=VMEM_LIMIT),
        name="out_mlp",
    )(o, x, w_out.astype(jnp.bfloat16), g_post[None, :], g_pre[None, :],
      w_up.astype(jnp.bfloat16), w_down.astype(jnp.bfloat16), g_ffn[None, :])


def _swa_proj_kernel(x_ref, pos_ref, g_ref, wq_ref, wk_ref, wv_ref, freq_ref, lo_ref, hi_ref,
                     q_out, k_out, v_out):
    h = _rms(x_ref[...], g_ref[...]).astype(jnp.bfloat16)
    ang = pos_ref[...].astype(jnp.float32) * freq_ref[...]
    cos = jnp.cos(ang)
    sin = jnp.sin(ang)
    sin_lo = sin * lo_ref[...]
    sin_hi = sin * hi_ref[...]
    half = HEAD_DIM // 2

    def rope(t):
        return t * cos + pltpu.roll(t, LANES - half, 1) * sin_lo + pltpu.roll(t, half, 1) * sin_hi

    q = _dot(h, wq_ref[...])
    for c in range(q_out.shape[1] // LANES):
        cols = slice(c * LANES, (c + 1) * LANES)
        q_out[:, cols] = (rope(q[:, cols]) * (QK_SCALE * LOG2E)).astype(jnp.bfloat16)
    k = _dot(h, wk_ref[...])
    for c in range(k_out.shape[1] // LANES):
        cols = slice(c * LANES, (c + 1) * LANES)
        k_out[:, cols] = rope(k[:, cols]).astype(jnp.bfloat16)
    v_out[...] = _dot(h, wv_ref[...]).astype(jnp.bfloat16)


def _dup_heads(w, heads):
    d = w.shape[0]
    w = w.reshape(d, heads, 1, HEAD_DIM)
    return jnp.broadcast_to(w, (d, heads, 2, HEAD_DIM)).reshape(d, heads * 2 * HEAD_DIM)


def _swa_proj(x, pos, g, w_in):
    T, D = x.shape
    tm = PROJ_TM
    qd = SWA_Q_HEADS * HEAD_DIM
    kvd = SWA_KV_HEADS * HEAD_DIM
    kw = SWA_KV_HEADS * 2 * HEAD_DIM
    bf = lambda a: a.astype(jnp.bfloat16)
    wq = w_in[:, :qd]
    wk = w_in[:, qd:qd + kvd]
    wv = w_in[:, qd + kvd:]
    half = HEAD_DIM // 2
    inv_freq = 1.0 / (ROPE_THETA ** (jnp.arange(0, HEAD_DIM, 2, dtype=jnp.float32) / HEAD_DIM))
    freq = jnp.tile(inv_freq, LANES // half)[None, :]
    first_half = (np.arange(LANES) % HEAD_DIM) < half
    lo = np.where(first_half, -1.0, 0.0).astype(np.float32)[None, :]
    hi = np.where(first_half, 0.0, 1.0).astype(np.float32)[None, :]
    row = lambda width: pl.BlockSpec((tm, width), lambda t: (t, 0))
    return pl.pallas_call(
        _swa_proj_kernel,
        out_shape=(jax.ShapeDtypeStruct((T, qd), jnp.bfloat16),
                   jax.ShapeDtypeStruct((T, kw), jnp.bfloat16),
                   jax.ShapeDtypeStruct((T, kw), jnp.bfloat16)),
        grid=(T // tm,),
        in_specs=[row(D), row(1), _const_spec((1, D)),
                  _const_spec((D, qd)), _const_spec((D, kw)), _const_spec((D, kw)),
                  _const_spec((1, LANES)), _const_spec((1, LANES)), _const_spec((1, LANES))],
        out_specs=(row(qd), row(kw), row(kw)),
        compiler_params=pltpu.CompilerParams(
            dimension_semantics=("arbitrary",), vmem_limit_bytes=VMEM_LIMIT),
        name="swa_proj",
    )(x, pos, g[None, :], bf(wq), bf(_dup_heads(wk, SWA_KV_HEADS)), bf(_dup_heads(wv, SWA_KV_HEADS)),
      freq, jnp.asarray(lo), jnp.asarray(hi))


def _swa_attn_kernel(sink_ref, q_ref, kp_ref, kc_ref, vp_ref, vc_ref, o_ref):
    n = pl.program_id(1)
    blk = WINDOW
    kb = jnp.concatenate([kp_ref[0], kc_ref[0]], axis=0)
    vb = jnp.concatenate([vp_ref[0], vc_ref[0]], axis=0)
    qi = lax.broadcasted_iota(jnp.int32, (blk, 2 * blk), 0)
    kj = lax.broadcasted_iota(jnp.int32, (blk, 2 * blk), 1)
    diff = qi + blk - kj
    first_key = jnp.where(n > 0, 0, blk)
    mask = (diff >= 0) & (diff < WINDOW) & (kj >= first_key)
    lane = lax.broadcasted_iota(jnp.int32, (blk, LANES), 1)
    low = lane < HEAD_DIM
    for c in range(SWA_Q_HEADS // 2):
        qcol = q_ref[0, :, c * LANES:(c + 1) * LANES].astype(jnp.float32)
        outs = []
        for par in range(2):
            head = 2 * c + par
            g = head // SWA_GROUP
            gcols = slice(g * LANES, (g + 1) * LANES)
            keep = low if par == 0 else jnp.logical_not(low)
            qm = jnp.where(keep, qcol, 0.0).astype(jnp.bfloat16)
            s = jnp.where(mask, _dot_nt(qm, kb[:, gcols]), NEG_BIG)
            sink = sink_ref[head] * LOG2E
            m = jnp.maximum(jnp.max(s, axis=1, keepdims=True), sink)
            e = jnp.exp2(s - m)
            denom = jnp.sum(e, axis=1, keepdims=True) + jnp.exp2(sink - m)
            p = (e * (1.0 / denom)).astype(jnp.bfloat16)
            outs.append(_dot(p, vb[:, gcols]))
        o_ref[0, :, c * LANES:(c + 1) * LANES] = jnp.where(low, outs[0], outs[1]).astype(jnp.bfloat16)


def _swa_attn(q, k, v, sinks):
    B, S, qd = q.shape
    kw = k.shape[2]
    blk = WINDOW
    cur = lambda b, n: (b, n, 0)
    prev = lambda b, n: (b, jnp.maximum(n - 1, 0), 0)
    return pl.pallas_call(
        _swa_attn_kernel,
        out_shape=jax.ShapeDtypeStruct((B, S, qd), jnp.bfloat16),
        grid=(B, S // blk),
        in_specs=[pl.BlockSpec(memory_space=pltpu.SMEM),
                  pl.BlockSpec((1, blk, qd), cur),
                  pl.BlockSpec((1, blk, kw), prev), pl.BlockSpec((1, blk, kw), cur),
                  pl.BlockSpec((1, blk, kw), prev), pl.BlockSpec((1, blk, kw), cur)],
        out_specs=pl.BlockSpec((1, blk, qd), cur),
        compiler_params=pltpu.CompilerParams(
            dimension_semantics=("arbitrary", "arbitrary"), vmem_limit_bytes=VMEM_LIMIT),
        name="swa_attn",
    )(sinks, q, k, k, v, v)


def kernel(x, positions, fox_w_in, fox_b_f, fox_w_out, swa_w_in, swa_sinks, swa_w_out,
           norm_pre_mix, norm_post_mix, norm_pre_ffn, norm_post_ffn, mlp_w_up, mlp_w_down):
    B, S, D = x.shape
    T = B * S
    q, k, v, stats = _fox_proj(x, norm_pre_mix[0], fox_w_in[0], fox_b_f[0])
    o = _fox_flash(q, k, v, stats)
    x1 = _out_mlp(o.reshape(T, -1), x.reshape(T, D), fox_w_out[0], norm_post_mix[0], norm_pre_ffn[0],
                  mlp_w_up[0], mlp_w_down[0], norm_post_ffn[0])
    q, k, v = _swa_proj(x1, positions.reshape(T, 1), norm_pre_mix[1], swa_w_in[0])
    o = _swa_attn(q.reshape(B, S, -1), k.reshape(B, S, -1), v.reshape(B, S, -1), swa_sinks[0])
    x2 = _out_mlp(o.reshape(T, -1), x1, swa_w_out[0], norm_post_mix[1], norm_pre_ffn[1],
                  mlp_w_up[1], mlp_w_down[1], norm_post_ffn[1])
    return x2.reshape(B, S, D)
```

```python
import functools
import math

import jax
import jax.numpy as jnp
import numpy as np
from jax import lax
from jax.experimental import pallas as pl
from jax.experimental.pallas import tpu as pltpu

D_MODEL = 1024
HEAD_DIM = 64
FOX_HEADS = 16
SWA_Q_HEADS = 16
SWA_KV_HEADS = 2
SWA_GROUP = SWA_Q_HEADS // SWA_KV_HEADS
WINDOW = 128
D_FF = 4 * D_MODEL
ROPE_THETA = 10000.0
NORM_EPS = 1e-6

LANES = 128
HEAD_AUG = LANES
LOG2E = 1.4426950408889634
QK_SCALE = HEAD_DIM ** -0.5
NEG_BIG = -1e30
SKIP_LOG2 = 170.0
NORM_SLACK = 1.02
SHIFT_GAP_LOG2 = 100.0
VMEM_LIMIT = 56 * 1024 * 1024

DEC0 = HEAD_DIM
N_SPLIT = 3
ONES_LANE = N_SPLIT * FOX_HEADS

PROJ_TM = 512
FLASH_BQ = 512
MLP_CHUNK = 1024


def _rms(x, g):
    return x * lax.rsqrt(jnp.mean(x * x, axis=-1, keepdims=True) + NORM_EPS) * g


def _dot(a, b):
    return jnp.dot(a, b, preferred_element_type=jnp.float32)


def _dot_nt(a, b):
    return lax.dot_general(a, b, (((1,), (1,)), ((), ())), preferred_element_type=jnp.float32)


def _split3(x):
    hi = x.astype(jnp.bfloat16)
    r = x - hi.astype(jnp.float32)
    mid = r.astype(jnp.bfloat16)
    lo = (r - mid.astype(jnp.float32)).astype(jnp.bfloat16)
    return hi, mid, lo


def _const_spec(shape):
    nd = len(shape)
    return pl.BlockSpec(shape, lambda *_: (0,) * nd, pipeline_mode=pl.Buffered(1))


def _fox_proj_kernel(x_ref, g_ref, wq_ref, wk_ref, wv_ref, wf_ref, b3_ref, place_ref,
                     tri_ref, sel_ref, q_out, k_out, v_out, stats_out, carry_ref, *, tm):
    @pl.when(pl.program_id(1) == 0)
    def _():
        carry_ref[...] = jnp.zeros_like(carry_ref)

    h = _rms(x_ref[0], g_ref[...]).astype(jnp.bfloat16)

    z = _dot(h, wf_ref[...]) + b3_ref[...]
    logf = jnp.minimum(z, 0.0) - jnp.log1p(jnp.exp(-jnp.abs(z)))

    tri = tri_ref[...]
    hi, mid, lo = _split3(logf)
    c = _dot(tri, hi) + _dot(tri, mid) + _dot(tri, lo) + carry_ref[...]
    carry_ref[...] = c[tm - 1:tm, :]

    chi, cmid, clo = (p.astype(jnp.float32) for p in _split3(c * LOG2E))
    lane = lax.broadcasted_iota(jnp.int32, (tm, LANES), 1)
    parts = jnp.where(lane < FOX_HEADS, chi,
                      jnp.where(lane < 2 * FOX_HEADS, cmid,
                                jnp.where(lane < ONES_LANE, clo,
                                          jnp.where(lane == ONES_LANE, 1.0, 0.0))))
    parts = parts.astype(jnp.bfloat16)

    width = FOX_HEADS * HEAD_AUG
    q = _dot(h, wq_ref[...]) * (QK_SCALE * LOG2E)
    k = _dot(h, wk_ref[...])
    v = _dot(h, wv_ref[...])
    qn2 = _dot((q * q).astype(jnp.bfloat16), sel_ref[...])
    kn2 = _dot((k * k).astype(jnp.bfloat16), sel_ref[...])

    low = lane < HEAD_DIM
    v_tail = jnp.where(lane == HEAD_DIM, 1.0, 0.0)
    for pr in range(FOX_HEADS // 2):
        pcols = slice(pr * LANES, (pr + 1) * LANES)
        acols = slice(2 * pr * HEAD_AUG, (2 * pr + 2) * HEAD_AUG)
        q_tail = _dot(parts, place_ref[:, acols])
        k_tail = _dot(parts, place_ref[:, width + 2 * pr * HEAD_AUG:width + (2 * pr + 2) * HEAD_AUG])
        for src, tail, out in ((q[:, pcols], q_tail, q_out), (k[:, pcols], k_tail, k_out),
                               (v[:, pcols], None, v_out)):
            for par, head in enumerate((src, pltpu.roll(src, HEAD_DIM, 1))):
                fill = v_tail if tail is None else tail[:, par * HEAD_AUG:(par + 1) * HEAD_AUG]
                hcols = slice((2 * pr + par) * HEAD_AUG, (2 * pr + par + 1) * HEAD_AUG)
                out[0, :, hcols] = jnp.where(low, head, fill).astype(jnp.bfloat16)

    c2 = c * LOG2E
    stats_out[0, 0] = jnp.zeros((8, LANES), jnp.float32)
    stats_out[0, 0, 0:1, :] = jnp.sqrt(jnp.max(qn2, axis=0, keepdims=True))
    stats_out[0, 0, 1:2, :] = jnp.sqrt(jnp.max(kn2, axis=0, keepdims=True))
    stats_out[0, 0, 2:3, :] = c2[0:1, :]
    stats_out[0, 0, 3:4, :] = c2[tm - 1:tm, :]
    diag = _dot((q * k).astype(jnp.bfloat16), sel_ref[...])
    stats_out[0, 0, 4:5, :] = jnp.min(diag, axis=0, keepdims=True)


def _fox_placement():
    width = FOX_HEADS * HEAD_AUG
    p = np.zeros((LANES, 2 * width), np.float32)
    for h in range(FOX_HEADS):
        base = h * HEAD_AUG + DEC0
        for s in range(N_SPLIT):
            p[s * FOX_HEADS + h, base + s] = 1.0
            p[ONES_LANE, base + N_SPLIT + s] = 1.0
            p[ONES_LANE, width + base + s] = 1.0
            p[s * FOX_HEADS + h, width + base + N_SPLIT + s] = -1.0
    return p


def _fox_proj(x, g, w_in, b_f):
    B, S, D = x.shape
    tm = PROJ_TM
    hd_all = FOX_HEADS * HEAD_DIM
    width = FOX_HEADS * HEAD_AUG
    wq = w_in[:, :hd_all].astype(jnp.bfloat16)
    wk = w_in[:, hd_all:2 * hd_all].astype(jnp.bfloat16)
    wv = w_in[:, 2 * hd_all:3 * hd_all].astype(jnp.bfloat16)
    wf = w_in[:, 3 * hd_all:]
    wf3 = jnp.concatenate([wf] * N_SPLIT + [jnp.zeros((D, LANES - ONES_LANE), wf.dtype)], axis=1)
    b3 = jnp.concatenate([b_f] * N_SPLIT + [jnp.zeros((LANES - ONES_LANE,), b_f.dtype)])[None, :]
    place = jnp.asarray(_fox_placement(), jnp.bfloat16)
    tri = jnp.asarray(np.tril(np.ones((tm, tm), np.float32)), jnp.bfloat16)
    sel = np.zeros((hd_all, LANES), np.float32)
    for hd in range(FOX_HEADS):
        sel[hd * HEAD_DIM:(hd + 1) * HEAD_DIM, hd] = 1.0

    out_sds = jax.ShapeDtypeStruct((B, S, width), jnp.bfloat16)
    out_spec = pl.BlockSpec((1, tm, width), lambda b, t: (b, t, 0))
    stats_sds = jax.ShapeDtypeStruct((B, S // tm, 8, LANES), jnp.float32)
    stats_spec = pl.BlockSpec((1, 1, 8, LANES), lambda b, t: (b, t, 0, 0))
    return pl.pallas_call(
        functools.partial(_fox_proj_kernel, tm=tm),
        out_shape=(out_sds, out_sds, out_sds, stats_sds),
        grid=(B, S // tm),
        in_specs=[
            pl.BlockSpec((1, tm, D), lambda b, t: (b, t, 0)),
            _const_spec((1, D)),
            _const_spec((D, hd_all)), _const_spec((D, hd_all)), _const_spec((D, hd_all)),
            _const_spec((D, LANES)), _const_spec((1, LANES)),
            _const_spec((LANES, 2 * width)),
            _const_spec((tm, tm)), _const_spec((hd_all, LANES)),
        ],
        out_specs=(out_spec, out_spec, out_spec, stats_spec),
        scratch_shapes=[pltpu.VMEM((1, LANES), jnp.float32)],
        compiler_params=pltpu.CompilerParams(
            dimension_semantics=("arbitrary", "arbitrary"), vmem_limit_bytes=VMEM_LIMIT),
        name="fox_proj",
    )(x, g[None, :], wq, wk, wv, wf3.astype(jnp.bfloat16), b3, place, tri,
      jnp.asarray(sel, jnp.bfloat16))


def _flash_transpose_v(v_ref, vt_ref, qi, bq, seq):
    @pl.when(qi == 0)
    def _():
        for c in range(seq // bq):
            rows = slice(c * bq, (c + 1) * bq)
            vt_ref[:, rows] = v_ref[0, rows, :].T


def _flash_finalize(acc_ref, o_ref, bq):
    outs = []
    for hh in range(2):
        acc = acc_ref[hh]
        outs.append((acc * (1.0 / acc[HEAD_DIM:HEAD_DIM + 1, :])).T)
    lane = lax.broadcasted_iota(jnp.int32, (bq, LANES), 1)
    o = jnp.where(lane < HEAD_DIM, outs[0], pltpu.roll(outs[1], HEAD_DIM, 1))
    o_ref[0] = o.astype(jnp.bfloat16)


def _fox_flash_kernel(first_ref, q_ref, k_ref, v_ref, o_ref, vt_ref, qt_ref, m_ref, acc_ref, s_ref, p_ref,
                      mb_ref, *, bq, seq):
    qi = pl.program_id(2)
    heads = 2
    nq = seq // bq
    j0 = first_ref[(pl.program_id(0) * pl.num_programs(1) + pl.program_id(1)) * nq + qi]
    _flash_transpose_v(v_ref, vt_ref, qi, bq, seq)
    qt_ref[...] = q_ref[0].T
    m_ref[...] = jnp.full_like(m_ref, NEG_BIG)
    acc_ref[...] = jnp.zeros_like(acc_ref)

    def scores(j, slot, masked):
        start = pl.multiple_of(j * bq, bq)
        for hh in range(heads):
            cols = slice(hh * HEAD_AUG, (hh + 1) * HEAD_AUG)
            st = _dot(k_ref[0, pl.ds(start, bq), cols], qt_ref[cols, :])
            if masked:
                key = lax.broadcasted_iota(jnp.int32, (bq, bq), 0)
                qry = lax.broadcasted_iota(jnp.int32, (bq, bq), 1)
                st = jnp.where(key - qry <= (qi - j) * bq, st, NEG_BIG)
            s_ref[2 * slot + hh] = st
            mb_ref[2 * slot + hh] = jnp.max(st, axis=0, keepdims=True)

    def consume(j, slot):
        start = pl.multiple_of(j * bq, bq)
        alpha = []
        for hh in range(heads):
            m_prev = m_ref[hh]
            m_new = jnp.maximum(m_prev, mb_ref[2 * slot + hh])
            m_ref[hh] = m_new
            p_ref[hh] = jnp.exp2(s_ref[2 * slot + hh] - m_new).astype(jnp.bfloat16)
            alpha.append(jnp.exp2(m_prev - m_new))
        for hh in range(heads):
            cols = slice(hh * HEAD_AUG, (hh + 1) * HEAD_AUG)
            vt = vt_ref[cols, pl.ds(start, bq)]
            acc_ref[hh] = alpha[hh] * acc_ref[hh] + _dot(vt, p_ref[hh])

    def half_step(j, slot, masked):
        scores(j + 1, 1 - slot, masked)
        consume(j, slot)

    span = qi - j0
    pairs = jnp.maximum(span - 1, 0) // 2
    rest = span - 2 * pairs
    scores(j0, 0, True)

    def body(t, carry):
        half_step(j0 + 2 * t, 0, False)
        half_step(j0 + 2 * t + 1, 1, False)
        return carry

    lax.fori_loop(0, pairs, body, 0)

    @pl.when(rest >= 1)
    def _():
        half_step(j0 + 2 * pairs, 0, True)

    @pl.when(rest == 2)
    def _():
        half_step(j0 + 2 * pairs + 1, 1, True)

    @pl.when(rest == 1)
    def _():
        consume(qi, 1)

    @pl.when(rest != 1)
    def _():
        consume(qi, 0)

    _flash_finalize(acc_ref, o_ref, bq)


def _fox_flash_fixed_kernel(first_ref, kall_ref, q_ref, k_ref, v_ref, o_ref, vt_ref, qt_ref, r_ref, acc_ref,
                            p_ref, *, bq, seq):
    b, hp, qi = pl.program_id(0), pl.program_id(1), pl.program_id(2)
    heads = 2
    j0 = first_ref[(b * pl.num_programs(1) + hp) * (seq // bq) + qi]
    _flash_transpose_v(v_ref, vt_ref, qi, bq, seq)
    qt_ref[...] = q_ref[0].T
    acc_ref[...] = jnp.zeros_like(acc_ref)
    for hh in range(heads):
        qf = qt_ref[hh * HEAD_AUG:hh * HEAD_AUG + HEAD_DIM, :].astype(jnp.float32)
        qnorm = jnp.sqrt(jnp.sum(qf * qf, axis=0, keepdims=True))
        r_ref[hh] = qnorm * (NORM_SLACK * kall_ref[b * FOX_HEADS + heads * hp + hh])

    def blocks(j, nblk, masked):
        start = pl.multiple_of(j * bq, bq)
        for hh in range(heads):
            cols = slice(hh * HEAD_AUG, (hh + 1) * HEAD_AUG)
            for u in range(nblk):
                st = _dot(k_ref[0, pl.ds(start + u * bq, bq), cols], qt_ref[cols, :])
                if masked:
                    key = lax.broadcasted_iota(jnp.int32, (bq, bq), 0)
                    qry = lax.broadcasted_iota(jnp.int32, (bq, bq), 1)
                    st = jnp.where(key <= qry, st, NEG_BIG)
                p_ref[hh, u * bq:(u + 1) * bq, :] = jnp.exp2(st - r_ref[hh]).astype(jnp.bfloat16)
        for hh in range(heads):
            cols = slice(hh * HEAD_AUG, (hh + 1) * HEAD_AUG)
            vt = vt_ref[cols, pl.ds(start, nblk * bq)]
            acc_ref[hh] += _dot(vt, p_ref[hh, 0:nblk * bq, :])

    span = qi - j0
    pairs = span // 2

    def body(t, carry):
        blocks(j0 + 2 * t, 2, False)
        return carry

    lax.fori_loop(0, pairs, body, 0)

    @pl.when(span - 2 * pairs == 1)
    def _():
        blocks(qi - 1, 1, False)

    blocks(qi, 1, True)
    _flash_finalize(acc_ref, o_ref, bq)


def _first_key_block(stats):
    qmax, kmax = stats[:, :, 0, :FOX_HEADS], stats[:, :, 1, :FOX_HEADS]
    c_first, c_last = stats[:, :, 2, :FOX_HEADS], stats[:, :, 3, :FOX_HEADS]
    kall = jnp.max(kmax, axis=1, keepdims=True)
    thr = c_first + SKIP_LOG2 + NORM_SLACK * 2.0 * qmax * kall
    skippable = c_last[:, None, :, :] > thr[:, :, None, :]
    first = jnp.sum(skippable.astype(jnp.int32), axis=2)
    nq = stats.shape[1]
    first = jnp.minimum(first, jnp.arange(nq, dtype=jnp.int32)[None, :, None])
    first = jnp.min(first.reshape(first.shape[0], nq, FOX_HEADS // 2, 2), axis=-1)
    return first.transpose(0, 2, 1).reshape(-1)


def _fox_flash(q, k, v, stats):
    B, S, width = q.shape
    bq = FLASH_BQ
    pair = 2 * HEAD_AUG
    first = _first_key_block(stats)
    qmax, kmax, diag_min = (stats[:, :, r, :FOX_HEADS] for r in (0, 1, 4))
    kall = jnp.max(kmax, axis=1)
    gap = NORM_SLACK * qmax * kall[:, None, :] - diag_min
    fixed_shift_ok = jnp.max(gap) < SHIFT_GAP_LOG2

    blk = lambda shape, index, **kw: pl.BlockSpec(shape, lambda b, hp, i, first: index(b, hp, i), **kw)
    tensor_specs = [
        blk((1, bq, pair), lambda b, hp, i: (b, i, hp)),
        blk((1, S, pair), lambda b, hp, i: (b, 0, hp)),
        blk((1, S, pair), lambda b, hp, i: (b, 0, hp), pipeline_mode=pl.Buffered(1)),
    ]
    common = dict(
        out_shape=jax.ShapeDtypeStruct((B, S, FOX_HEADS * HEAD_DIM), jnp.bfloat16),
        compiler_params=pltpu.CompilerParams(
            dimension_semantics=("arbitrary", "arbitrary", "arbitrary"),
            vmem_limit_bytes=VMEM_LIMIT))
    grid = (B, FOX_HEADS // 2, S // bq)
    out_spec = blk((1, bq, 2 * HEAD_DIM), lambda b, hp, i: (b, i, hp))
    vt_qt = [pltpu.VMEM((pair, S), jnp.bfloat16), pltpu.VMEM((pair, bq), jnp.bfloat16)]

    def fixed_shift():
        return pl.pallas_call(
            functools.partial(_fox_flash_fixed_kernel, bq=bq, seq=S),
            grid_spec=pltpu.PrefetchScalarGridSpec(
                num_scalar_prefetch=1, grid=grid,
                in_specs=[pl.BlockSpec(memory_space=pltpu.SMEM)] + tensor_specs,
                out_specs=out_spec,
                scratch_shapes=vt_qt + [pltpu.VMEM((2, 1, bq), jnp.float32),
                                        pltpu.VMEM((2, HEAD_AUG, bq), jnp.float32),
                                        pltpu.VMEM((2, 2 * bq, bq), jnp.bfloat16)]),
            name="fox_flash_fixed", **common)(first, kall.reshape(-1), q, k, v)

    def running_max():
        return pl.pallas_call(
            functools.partial(_fox_flash_kernel, bq=bq, seq=S),
            grid_spec=pltpu.PrefetchScalarGridSpec(
                num_scalar_prefetch=1, grid=grid, in_specs=tensor_specs, out_specs=out_spec,
                scratch_shapes=vt_qt + [pltpu.VMEM((2, 1, bq), jnp.float32),
                                        pltpu.VMEM((2, HEAD_AUG, bq), jnp.float32),
                                        pltpu.VMEM((4, bq, bq), jnp.float32),
                                        pltpu.VMEM((2, bq, bq), jnp.bfloat16),
                                        pltpu.VMEM((4, 1, bq), jnp.float32)]),
            name="fox_flash", **common)(first, q, k, v)

    return lax.cond(fixed_shift_ok, fixed_shift, running_max)


def _out_mlp_kernel(o_ref, x_ref, wo_ref, g_post_ref, g_pre_ref, wu_ref, wd_ref, g_ffn_ref, y_ref):
    a = _dot(o_ref[...], wo_ref[...])
    x1 = x_ref[...] + _rms(a, g_post_ref[...])
    h = _rms(x1, g_pre_ref[...]).astype(jnp.bfloat16)
    acc = jnp.zeros_like(x1)
    for c in range(D_FF // MLP_CHUNK):
        cols = slice(c * MLP_CHUNK, (c + 1) * MLP_CHUNK)
        u = jnp.maximum(_dot(h, wu_ref[:, cols]), 0.0)
        acc = acc + _dot((u * u).astype(jnp.bfloat16), wd_ref[cols, :])
    y_ref[...] = x1 + _rms(acc, g_ffn_ref[...])


def _out_mlp(o, x, w_out, g_post, g_pre, w_up, w_down, g_ffn):
    T, D = x.shape
    tm = PROJ_TM
    row = lambda width: pl.BlockSpec((tm, width), lambda t: (t, 0))
    return pl.pallas_call(
        _out_mlp_kernel,
        out_shape=jax.ShapeDtypeStruct((T, D), jnp.float32),
        grid=(T // tm,),
        in_specs=[row(o.shape[1]), row(D),
                  _const_spec(w_out.shape), _const_spec((1, D)), _const_spec((1, D)),
                  _const_spec(w_up.shape), _const_spec(w_down.shape), _const_spec((1, D))],
        out_specs=row(D),
        compiler_params=pltpu.CompilerParams(
            dimension_semantics=("arbitrary",), vmem_limit_bytes=VMEM_LIMIT),
        name="out_mlp",
    )(o, x, w_out.astype(jnp.bfloat16), g_post[None, :], g_pre[None, :],
      w_up.astype(jnp.bfloat16), w_down.astype(jnp.bfloat16), g_ffn[None, :])


def _swa_proj_kernel(x_ref, pos_ref, g_ref, wq_ref, wk_ref, wv_ref, freq_ref, lo_ref, hi_ref,
                     q_out, k_out, v_out):
    h = _rms(x_ref[...], g_ref[...]).astype(jnp.bfloat16)
    ang = pos_ref[...].astype(jnp.float32) * freq_ref[...]
    cos = jnp.cos(ang)
    sin = jnp.sin(ang)
    sin_lo = sin * lo_ref[...]
    sin_hi = sin * hi_ref[...]
    half = HEAD_DIM // 2

    def rope(t):
        return t * cos + pltpu.roll(t, LANES - half, 1) * sin_lo + pltpu.roll(t, half, 1) * sin_hi

    q = _dot(h, wq_ref[...])
    for c in range(q_out.shape[1] // LANES):
        cols = slice(c * LANES, (c + 1) * LANES)
        q_out[:, cols] = (rope(q[:, cols]) * (QK_SCALE * LOG2E)).astype(jnp.bfloat16)
    k = _dot(h, wk_ref[...])
    for c in range(k_out.shape[1] // LANES):
        cols = slice(c * LANES, (c + 1) * LANES)
        k_out[:, cols] = rope(k[:, cols]).astype(jnp.bfloat16)
    v_out[...] = _dot(h, wv_ref[...]).astype(jnp.bfloat16)


def _dup_heads(w, heads):
    d = w.shape[0]
    w = w.reshape(d, heads, 1, HEAD_DIM)
    return jnp.broadcast_to(w, (d, heads, 2, HEAD_DIM)).reshape(d, heads * 2 * HEAD_DIM)


def _swa_proj(x, pos, g, w_in):
    T, D = x.shape
    tm = PROJ_TM
    qd = SWA_Q_HEADS * HEAD_DIM
    kvd = SWA_KV_HEADS * HEAD_DIM
    kw = SWA_KV_HEADS * 2 * HEAD_DIM
    bf = lambda a: a.astype(jnp.bfloat16)
    wq = w_in[:, :qd]
    wk = w_in[:, qd:qd + kvd]
    wv = w_in[:, qd + kvd:]
    half = HEAD_DIM // 2
    inv_freq = 1.0 / (ROPE_THETA ** (jnp.arange(0, HEAD_DIM, 2, dtype=jnp.float32) / HEAD_DIM))
    freq = jnp.tile(inv_freq, LANES // half)[None, :]
    first_half = (np.arange(LANES) % HEAD_DIM) < half
    lo = np.where(first_half, -1.0, 0.0).astype(np.float32)[None, :]
    hi = np.where(first_half, 0.0, 1.0).astype(np.float32)[None, :]
    row = lambda width: pl.BlockSpec((tm, width), lambda t: (t, 0))
    return pl.pallas_call(
        _swa_proj_kernel,
        out_shape=(jax.ShapeDtypeStruct((T, qd), jnp.bfloat16),
                   jax.ShapeDtypeStruct((T, kw), jnp.bfloat16),
                   jax.ShapeDtypeStruct((T, kw), jnp.bfloat16)),
        grid=(T // tm,),
        in_specs=[row(D), row(1), _const_spec((1, D)),
                  _const_spec((D, qd)), _const_spec((D, kw)), _const_spec((D, kw)),
                  _const_spec((1, LANES)), _const_spec((1, LANES)), _const_spec((1, LANES))],
        out_specs=(row(qd), row(kw), row(kw)),
        compiler_params=pltpu.CompilerParams(
            dimension_semantics=("arbitrary",), vmem_limit_bytes=VMEM_LIMIT),
        name="swa_proj",
    )(x, pos, g[None, :], bf(wq), bf(_dup_heads(wk, SWA_KV_HEADS)), bf(_dup_heads(wv, SWA_KV_HEADS)),
      freq, jnp.asarray(lo), jnp.asarray(hi))


def _swa_attn_kernel(sink_ref, q_ref, kp_ref, kc_ref, vp_ref, vc_ref, o_ref):
    n = pl.program_id(1)
    blk = WINDOW
    kb = jnp.concatenate([kp_ref[0], kc_ref[0]], axis=0)
    vb = jnp.concatenate([vp_ref[0], vc_ref[0]], axis=0)
    qi = lax.broadcasted_iota(jnp.int32, (blk, 2 * blk), 0)
    kj = lax.broadcasted_iota(jnp.int32, (blk, 2 * blk), 1)
    diff = qi + blk - kj
    first_key = jnp.where(n > 0, 0, blk)
    mask = (diff >= 0) & (diff < WINDOW) & (kj >= first_key)
    lane = lax.broadcasted_iota(jnp.int32, (blk, LANES), 1)
    low = lane < HEAD_DIM
    for c in range(SWA_Q_HEADS // 2):
        qcol = q_ref[0, :, c * LANES:(c + 1) * LANES].astype(jnp.float32)
        outs = []
        for par in range(2):
            head = 2 * c + par
            g = head // SWA_GROUP
            gcols = slice(g * LANES, (g + 1) * LANES)
            keep = low if par == 0 else jnp.logical_not(low)
            qm = jnp.where(keep, qcol, 0.0).astype(jnp.bfloat16)
            s = jnp.where(mask, _dot_nt(qm, kb[:, gcols]), NEG_BIG)
            sink = sink_ref[head] * LOG2E
            m = jnp.maximum(jnp.max(s, axis=1, keepdims=True), sink)
            e = jnp.exp2(s - m)
            denom = jnp.sum(e, axis=1, keepdims=True) + jnp.exp2(sink - m)
            p = (e * (1.0 / denom)).astype(jnp.bfloat16)
            outs.append(_dot(p, vb[:, gcols]))
        o_ref[0, :, c * LANES:(c + 1) * LANES] = jnp.where(low, outs[0], outs[1]).astype(jnp.bfloat16)


def _swa_attn(q, k, v, sinks):
    B, S, qd = q.shape
    kw = k.shape[2]
    blk = WINDOW
    cur = lambda b, n: (b, n, 0)
    prev = lambda b, n: (b, jnp.maximum(n - 1, 0), 0)
    return pl.pallas_call(
        _swa_attn_kernel,
        out_shape=jax.ShapeDtypeStruct((B, S, qd), jnp.bfloat16),
        grid=(B, S // blk),
        in_specs=[pl.BlockSpec(memory_space=pltpu.SMEM),
                  pl.BlockSpec((1, blk, qd), cur),
                  pl.BlockSpec((1, blk, kw), prev), pl.BlockSpec((1, blk, kw), cur),
                  pl.BlockSpec((1, blk, kw), prev), pl.BlockSpec((1, blk, kw), cur)],
        out_specs=pl.BlockSpec((1, blk, qd), cur),
        compiler_params=pltpu.CompilerParams(
            dimension_semantics=("arbitrary", "arbitrary"), vmem_limit_bytes=VMEM_LIMIT),
        name="swa_attn",
    )(sinks, q, k, k, v, v)


def kernel(x, positions, fox_w_in, fox_b_f, fox_w_out, swa_w_in, swa_sinks, swa_w_out,
           norm_pre_mix, norm_post_mix, norm_pre_ffn, norm_post_ffn, mlp_w_up, mlp_w_down):
    B, S, D = x.shape
    T = B * S
    q, k, v, stats = _fox_proj(x, norm_pre_mix[0], fox_w_in[0], fox_b_f[0])
    o = _fox_flash(q, k, v, stats)
    x1 = _out_mlp(o.reshape(T, -1), x.reshape(T, D), fox_w_out[0], norm_post_mix[0], norm_pre_ffn[0],
                  mlp_w_up[0], mlp_w_down[0], norm_post_ffn[0])
    q, k, v = _swa_proj(x1, positions.reshape(T, 1), norm_pre_mix[1], swa_w_in[0])
    o = _swa_attn(q.reshape(B, S, -1), k.reshape(B, S, -1), v.reshape(B, S, -1), swa_sinks[0])
    x2 = _out_mlp(o.reshape(T, -1), x1, swa_w_out[0], norm_post_mix[1], norm_pre_ffn[1],
                  mlp_w_up[1], mlp_w_down[1], norm_post_ffn[1])
    return x2.reshape(B, S, D)
```

```python
import functools
import math

import jax
import jax.numpy as jnp
import numpy as np
from jax import lax
from jax.experimental import pallas as pl
from jax.experimental.pallas import tpu as pltpu

D_MODEL = 1024
HEAD_DIM = 64
FOX_HEADS = 16
SWA_Q_HEADS = 16
SWA_KV_HEADS = 2
SWA_GROUP = SWA_Q_HEADS // SWA_KV_HEADS
WINDOW = 128
D_FF = 4 * D_MODEL
ROPE_THETA = 10000.0
NORM_EPS = 1e-6

LANES = 128
HEAD_AUG = LANES
LOG2E = 1.4426950408889634
QK_SCALE = HEAD_DIM ** -0.5
NEG_BIG = -1e30
SKIP_LOG2 = 152.0
NORM_SLACK = 1.02
SHIFT_GAP_LOG2 = 100.0
VMEM_LIMIT = 56 * 1024 * 1024

DEC0 = HEAD_DIM
N_SPLIT = 3
ONES_LANE = N_SPLIT * FOX_HEADS

PROJ_TM = 512
FLASH_BQ = 512
MLP_CHUNK = 1024


def _rms(x, g):
    return x * lax.rsqrt(jnp.mean(x * x, axis=-1, keepdims=True) + NORM_EPS) * g


def _dot(a, b):
    return jnp.dot(a, b, preferred_element_type=jnp.float32)


def _dot_nt(a, b):
    return lax.dot_general(a, b, (((1,), (1,)), ((), ())), preferred_element_type=jnp.float32)


def _split3(x):
    hi = x.astype(jnp.bfloat16)
    r = x - hi.astype(jnp.float32)
    mid = r.astype(jnp.bfloat16)
    lo = (r - mid.astype(jnp.float32)).astype(jnp.bfloat16)
    return hi, mid, lo


def _const_spec(shape):
    nd = len(shape)
    return pl.BlockSpec(shape, lambda *_: (0,) * nd, pipeline_mode=pl.Buffered(1))


def _fox_proj_kernel(x_ref, g_ref, wq_ref, wk_ref, wv_ref, wf_ref, b3_ref, place_ref,
                     tri_ref, sel_ref, q_out, k_out, v_out, stats_out, carry_ref, *, tm):
    @pl.when(pl.program_id(1) == 0)
    def _():
        carry_ref[...] = jnp.zeros_like(carry_ref)

    h = _rms(x_ref[0], g_ref[...]).astype(jnp.bfloat16)

    z = _dot(h, wf_ref[...]) + b3_ref[...]
    logf = jnp.minimum(z, 0.0) - jnp.log1p(jnp.exp(-jnp.abs(z)))

    tri = tri_ref[...]
    hi, mid, lo = _split3(logf)
    c = _dot(tri, hi) + _dot(tri, mid) + _dot(tri, lo) + carry_ref[...]
    carry_ref[...] = c[tm - 1:tm, :]

    chi, cmid, clo = (p.astype(jnp.float32) for p in _split3(c * LOG2E))
    lane = lax.broadcasted_iota(jnp.int32, (tm, LANES), 1)
    parts = jnp.where(lane < FOX_HEADS, chi,
                      jnp.where(lane < 2 * FOX_HEADS, cmid,
                                jnp.where(lane < ONES_LANE, clo,
                                          jnp.where(lane == ONES_LANE, 1.0, 0.0))))
    parts = parts.astype(jnp.bfloat16)

    width = FOX_HEADS * HEAD_AUG
    q = _dot(h, wq_ref[...]) * (QK_SCALE * LOG2E)
    k = _dot(h, wk_ref[...])
    v = _dot(h, wv_ref[...])
    qn2 = _dot((q * q).astype(jnp.bfloat16), sel_ref[...])
    kn2 = _dot((k * k).astype(jnp.bfloat16), sel_ref[...])

    low = lane < HEAD_DIM
    v_tail = jnp.where(lane == HEAD_DIM, 1.0, 0.0)
    for pr in range(FOX_HEADS // 2):
        pcols = slice(pr * LANES, (pr + 1) * LANES)
        acols = slice(2 * pr * HEAD_AUG, (2 * pr + 2) * HEAD_AUG)
        q_tail = _dot(parts, place_ref[:, acols])
        k_tail = _dot(parts, place_ref[:, width + 2 * pr * HEAD_AUG:width + (2 * pr + 2) * HEAD_AUG])
        for src, tail, out in ((q[:, pcols], q_tail, q_out), (k[:, pcols], k_tail, k_out),
                               (v[:, pcols], None, v_out)):
            for par, head in enumerate((src, pltpu.roll(src, HEAD_DIM, 1))):
                fill = v_tail if tail is None else tail[:, par * HEAD_AUG:(par + 1) * HEAD_AUG]
                hcols = slice((2 * pr + par) * HEAD_AUG, (2 * pr + par + 1) * HEAD_AUG)
                out[0, :, hcols] = jnp.where(low, head, fill).astype(jnp.bfloat16)

    c2 = c * LOG2E
    stats_out[0, 0] = jnp.zeros((8, LANES), jnp.float32)
    stats_out[0, 0, 0:1, :] = jnp.sqrt(jnp.max(qn2, axis=0, keepdims=True))
    stats_out[0, 0, 1:2, :] = jnp.sqrt(jnp.max(kn2, axis=0, keepdims=True))
    stats_out[0, 0, 2:3, :] = c2[0:1, :]
    stats_out[0, 0, 3:4, :] = c2[tm - 1:tm, :]
    diag = _dot((q * k).astype(jnp.bfloat16), sel_ref[...])
    stats_out[0, 0, 4:5, :] = jnp.min(diag, axis=0, keepdims=True)


def _fox_placement():
    width = FOX_HEADS * HEAD_AUG
    p = np.zeros((LANES, 2 * width), np.float32)
    for h in range(FOX_HEADS):
        base = h * HEAD_AUG + DEC0
        for s in range(N_SPLIT):
            p[s * FOX_HEADS + h, base + s] = 1.0
            p[ONES_LANE, base + N_SPLIT + s] = 1.0
            p[ONES_LANE, width + base + s] = 1.0
            p[s * FOX_HEADS + h, width + base + N_SPLIT + s] = -1.0
    return p


def _fox_proj(x, g, w_in, b_f):
    B, S, D = x.shape
    tm = PROJ_TM
    hd_all = FOX_HEADS * HEAD_DIM
    width = FOX_HEADS * HEAD_AUG
    wq = w_in[:, :hd_all].astype(jnp.bfloat16)
    wk = w_in[:, hd_all:2 * hd_all].astype(jnp.bfloat16)
    wv = w_in[:, 2 * hd_all:3 * hd_all].astype(jnp.bfloat16)
    wf = w_in[:, 3 * hd_all:]
    wf3 = jnp.concatenate([wf] * N_SPLIT + [jnp.zeros((D, LANES - ONES_LANE), wf.dtype)], axis=1)
    b3 = jnp.concatenate([b_f] * N_SPLIT + [jnp.zeros((LANES - ONES_LANE,), b_f.dtype)])[None, :]
    place = jnp.asarray(_fox_placement(), jnp.bfloat16)
    tri = jnp.asarray(np.tril(np.ones((tm, tm), np.float32)), jnp.bfloat16)
    sel = np.zeros((hd_all, LANES), np.float32)
    for hd in range(FOX_HEADS):
        sel[hd * HEAD_DIM:(hd + 1) * HEAD_DIM, hd] = 1.0

    out_sds = jax.ShapeDtypeStruct((B, S, width), jnp.bfloat16)
    out_spec = pl.BlockSpec((1, tm, width), lambda b, t: (b, t, 0))
    stats_sds = jax.ShapeDtypeStruct((B, S // tm, 8, LANES), jnp.float32)
    stats_spec = pl.BlockSpec((1, 1, 8, LANES), lambda b, t: (b, t, 0, 0))
    return pl.pallas_call(
        functools.partial(_fox_proj_kernel, tm=tm),
        out_shape=(out_sds, out_sds, out_sds, stats_sds),
        grid=(B, S // tm),
        in_specs=[
            pl.BlockSpec((1, tm, D), lambda b, t: (b, t, 0)),
            _const_spec((1, D)),
            _const_spec((D, hd_all)), _const_spec((D, hd_all)), _const_spec((D, hd_all)),
            _const_spec((D, LANES)), _const_spec((1, LANES)),
            _const_spec((LANES, 2 * width)),
            _const_spec((tm, tm)), _const_spec((hd_all, LANES)),
        ],
        out_specs=(out_spec, out_spec, out_spec, stats_spec),
        scratch_shapes=[pltpu.VMEM((1, LANES), jnp.float32)],
        compiler_params=pltpu.CompilerParams(
            dimension_semantics=("arbitrary", "arbitrary"), vmem_limit_bytes=VMEM_LIMIT),
        name="fox_proj",
    )(x, g[None, :], wq, wk, wv, wf3.astype(jnp.bfloat16), b3, place, tri,
      jnp.asarray(sel, jnp.bfloat16))


def _flash_transpose_v(v_ref, vt_ref, qi, bq, seq):
    @pl.when(qi == 0)
    def _():
        for c in range(seq // bq):
            rows = slice(c * bq, (c + 1) * bq)
            vt_ref[:, rows] = v_ref[0, rows, :].T


def _flash_finalize(acc_ref, o_ref, bq):
    outs = []
    for hh in range(2):
        acc = acc_ref[hh]
        outs.append((acc * (1.0 / acc[HEAD_DIM:HEAD_DIM + 1, :])).T)
    lane = lax.broadcasted_iota(jnp.int32, (bq, LANES), 1)
    o = jnp.where(lane < HEAD_DIM, outs[0], pltpu.roll(outs[1], HEAD_DIM, 1))
    o_ref[0] = o.astype(jnp.bfloat16)


def _fox_flash_kernel(first_ref, q_ref, k_ref, v_ref, o_ref, vt_ref, qt_ref, m_ref, acc_ref, s_ref, p_ref,
                      mb_ref, *, bq, seq):
    qi = pl.program_id(2)
    heads = 2
    nq = seq // bq
    j0 = first_ref[(pl.program_id(0) * pl.num_programs(1) + pl.program_id(1)) * nq + qi]
    _flash_transpose_v(v_ref, vt_ref, qi, bq, seq)
    qt_ref[...] = q_ref[0].T
    m_ref[...] = jnp.full_like(m_ref, NEG_BIG)
    acc_ref[...] = jnp.zeros_like(acc_ref)

    def scores(j, slot, masked):
        start = pl.multiple_of(j * bq, bq)
        for hh in range(heads):
            cols = slice(hh * HEAD_AUG, (hh + 1) * HEAD_AUG)
            st = _dot(k_ref[0, pl.ds(start, bq), cols], qt_ref[cols, :])
            if masked:
                key = lax.broadcasted_iota(jnp.int32, (bq, bq), 0)
                qry = lax.broadcasted_iota(jnp.int32, (bq, bq), 1)
                st = jnp.where(key - qry <= (qi - j) * bq, st, NEG_BIG)
            s_ref[2 * slot + hh] = st
            mb_ref[2 * slot + hh] = jnp.max(st, axis=0, keepdims=True)

    def consume(j, slot):
        start = pl.multiple_of(j * bq, bq)
        alpha = []
        for hh in range(heads):
            m_prev = m_ref[hh]
            m_new = jnp.maximum(m_prev, mb_ref[2 * slot + hh])
            m_ref[hh] = m_new
            p_ref[hh] = jnp.exp2(s_ref[2 * slot + hh] - m_new).astype(jnp.bfloat16)
            alpha.append(jnp.exp2(m_prev - m_new))
        for hh in range(heads):
            cols = slice(hh * HEAD_AUG, (hh + 1) * HEAD_AUG)
            vt = vt_ref[cols, pl.ds(start, bq)]
            acc_ref[hh] = alpha[hh] * acc_ref[hh] + _dot(vt, p_ref[hh])

    def half_step(j, slot, masked):
        scores(j + 1, 1 - slot, masked)
        consume(j, slot)

    span = qi - j0
    pairs = jnp.maximum(span - 1, 0) // 2
    rest = span - 2 * pairs
    scores(j0, 0, True)

    def body(t, carry):
        half_step(j0 + 2 * t, 0, False)
        half_step(j0 + 2 * t + 1, 1, False)
        return carry

    lax.fori_loop(0, pairs, body, 0)

    @pl.when(rest >= 1)
    def _():
        half_step(j0 + 2 * pairs, 0, True)

    @pl.when(rest == 2)
    def _():
        half_step(j0 + 2 * pairs + 1, 1, True)

    @pl.when(rest == 1)
    def _():
        consume(qi, 1)

    @pl.when(rest != 1)
    def _():
        consume(qi, 0)

    _flash_finalize(acc_ref, o_ref, bq)


def _fox_flash_fixed_kernel(first_ref, kall_ref, q_ref, k_ref, v_ref, o_ref, vt_ref, qt_ref, r_ref, acc_ref,
                            p_ref, *, bq, seq):
    b, hp, qi = pl.program_id(0), pl.program_id(1), pl.program_id(2)
    heads = 2
    j0 = first_ref[(b * pl.num_programs(1) + hp) * (seq // bq) + qi]
    _flash_transpose_v(v_ref, vt_ref, qi, bq, seq)
    qt_ref[...] = q_ref[0].T
    acc_ref[...] = jnp.zeros_like(acc_ref)
    for hh in range(heads):
        qf = qt_ref[hh * HEAD_AUG:hh * HEAD_AUG + HEAD_DIM, :].astype(jnp.float32)
        qnorm = jnp.sqrt(jnp.sum(qf * qf, axis=0, keepdims=True))
        r_ref[hh] = qnorm * (NORM_SLACK * kall_ref[b * FOX_HEADS + heads * hp + hh])

    def blocks(j, nblk, masked):
        start = pl.multiple_of(j * bq, bq)
        for hh in range(heads):
            cols = slice(hh * HEAD_AUG, (hh + 1) * HEAD_AUG)
            for u in range(nblk):
                st = _dot(k_ref[0, pl.ds(start + u * bq, bq), cols], qt_ref[cols, :])
                if masked:
                    key = lax.broadcasted_iota(jnp.int32, (bq, bq), 0)
                    qry = lax.broadcasted_iota(jnp.int32, (bq, bq), 1)
                    st = jnp.where(key <= qry, st, NEG_BIG)
                p_ref[hh, u * bq:(u + 1) * bq, :] = jnp.exp2(st - r_ref[hh]).astype(jnp.bfloat16)
        for hh in range(heads):
            cols = slice(hh * HEAD_AUG, (hh + 1) * HEAD_AUG)
            vt = vt_ref[cols, pl.ds(start, nblk * bq)]
            acc_ref[hh] += _dot(vt, p_ref[hh, 0:nblk * bq, :])

    span = qi - j0
    pairs = span // 2

    def body(t, carry):
        blocks(j0 + 2 * t, 2, False)
        return carry

    lax.fori_loop(0, pairs, body, 0)

    @pl.when(span - 2 * pairs == 1)
    def _():
        blocks(qi - 1, 1, False)

    blocks(qi, 1, True)
    _flash_finalize(acc_ref, o_ref, bq)


def _first_key_block(stats):
    qmax, kmax = stats[:, :, 0, :FOX_HEADS], stats[:, :, 1, :FOX_HEADS]
    c_first, c_last = stats[:, :, 2, :FOX_HEADS], stats[:, :, 3, :FOX_HEADS]
    kall = jnp.max(kmax, axis=1, keepdims=True)
    thr = c_first + SKIP_LOG2 + NORM_SLACK * 2.0 * qmax * kall
    skippable = c_last[:, None, :, :] > thr[:, :, None, :]
    first = jnp.sum(skippable.astype(jnp.int32), axis=2)
    nq = stats.shape[1]
    first = jnp.minimum(first, jnp.arange(nq, dtype=jnp.int32)[None, :, None])
    first = jnp.min(first.reshape(first.shape[0], nq, FOX_HEADS // 2, 2), axis=-1)
    return first.transpose(0, 2, 1).reshape(-1)


def _fox_flash(q, k, v, stats):
    B, S, width = q.shape
    bq = FLASH_BQ
    pair = 2 * HEAD_AUG
    first = _first_key_block(stats)
    qmax, kmax, diag_min = (stats[:, :, r, :FOX_HEADS] for r in (0, 1, 4))
    kall = jnp.max(kmax, axis=1)
    gap = NORM_SLACK * qmax * kall[:, None, :] - diag_min
    fixed_shift_ok = jnp.max(gap) < SHIFT_GAP_LOG2

    blk = lambda shape, index, **kw: pl.BlockSpec(shape, lambda b, hp, i, first: index(b, hp, i), **kw)
    tensor_specs = [
        blk((1, bq, pair), lambda b, hp, i: (b, i, hp)),
        blk((1, S, pair), lambda b, hp, i: (b, 0, hp)),
        blk((1, S, pair), lambda b, hp, i: (b, 0, hp), pipeline_mode=pl.Buffered(1)),
    ]
    common = dict(
        out_shape=jax.ShapeDtypeStruct((B, S, FOX_HEADS * HEAD_DIM), jnp.bfloat16),
        compiler_params=pltpu.CompilerParams(
            dimension_semantics=("arbitrary", "arbitrary", "arbitrary"),
            vmem_limit_bytes=VMEM_LIMIT))
    grid = (B, FOX_HEADS // 2, S // bq)
    out_spec = blk((1, bq, 2 * HEAD_DIM), lambda b, hp, i: (b, i, hp))
    vt_qt = [pltpu.VMEM((pair, S), jnp.bfloat16), pltpu.VMEM((pair, bq), jnp.bfloat16)]

    def fixed_shift():
        return pl.pallas_call(
            functools.partial(_fox_flash_fixed_kernel, bq=bq, seq=S),
            grid_spec=pltpu.PrefetchScalarGridSpec(
                num_scalar_prefetch=1, grid=grid,
                in_specs=[pl.BlockSpec(memory_space=pltpu.SMEM)] + tensor_specs,
                out_specs=out_spec,
                scratch_shapes=vt_qt + [pltpu.VMEM((2, 1, bq), jnp.float32),
                                        pltpu.VMEM((2, HEAD_AUG, bq), jnp.float32),
                                        pltpu.VMEM((2, 2 * bq, bq), jnp.bfloat16)]),
            name="fox_flash_fixed", **common)(first, kall.reshape(-1), q, k, v)

    def running_max():
        return pl.pallas_call(
            functools.partial(_fox_flash_kernel, bq=bq, seq=S),
            grid_spec=pltpu.PrefetchScalarGridSpec(
                num_scalar_prefetch=1, grid=grid, in_specs=tensor_specs, out_specs=out_spec,
                scratch_shapes=vt_qt + [pltpu.VMEM((2, 1, bq), jnp.float32),
                                        pltpu.VMEM((2, HEAD_AUG, bq), jnp.float32),
                                        pltpu.VMEM((4, bq, bq), jnp.float32),
                                        pltpu.VMEM((2, bq, bq), jnp.bfloat16),
                                        pltpu.VMEM((4, 1, bq), jnp.float32)]),
            name="fox_flash", **common)(first, q, k, v)

    return lax.cond(fixed_shift_ok, fixed_shift, running_max)


def _out_mlp_kernel(o_ref, x_ref, wo_ref, g_post_ref, g_pre_ref, wu_ref, wd_ref, g_ffn_ref, y_ref):
    a = _dot(o_ref[...], wo_ref[...])
    x1 = x_ref[...] + _rms(a, g_post_ref[...])
    h = _rms(x1, g_pre_ref[...]).astype(jnp.bfloat16)
    acc = jnp.zeros_like(x1)
    for c in range(D_FF // MLP_CHUNK):
        cols = slice(c * MLP_CHUNK, (c + 1) * MLP_CHUNK)
        u = jnp.maximum(_dot(h, wu_ref[:, cols]), 0.0)
        acc = acc + _dot((u * u).astype(jnp.bfloat16), wd_ref[cols, :])
    y_ref[...] = x1 + _rms(acc, g_ffn_ref[...])


def _out_mlp(o, x, w_out, g_post, g_pre, w_up, w_down, g_ffn):
    T, D = x.shape
    tm = PROJ_TM
    row = lambda width: pl.BlockSpec((tm, width), lambda t: (t, 0))
    return pl.pallas_call(
        _out_mlp_kernel,
        out_shape=jax.ShapeDtypeStruct((T, D), jnp.float32),
        grid=(T // tm,),
        in_specs=[row(o.shape[1]), row(D),
                  _const_spec(w_out.shape), _const_spec((1, D)), _const_spec((1, D)),
                  _const_spec(w_up.shape), _const_spec(w_down.shape), _const_spec((1, D))],
        out_specs=row(D),
        compiler_params=pltpu.CompilerParams(
            dimension_semantics=("arbitrary",), vmem_limit_bytes=VMEM_LIMIT),
        name="out_mlp",
    )(o, x, w_out.astype(jnp.bfloat16), g_post[None, :], g_pre[None, :],
      w_up.astype(jnp.bfloat16), w_down.astype(jnp.bfloat16), g_ffn[None, :])


def _swa_proj_kernel(x_ref, pos_ref, g_ref, wq_ref, wk_ref, wv_ref, freq_ref, lo_ref, hi_ref,
                     q_out, k_out, v_out):
    h = _rms(x_ref[...], g_ref[...]).astype(jnp.bfloat16)
    ang = pos_ref[...].astype(jnp.float32) * freq_ref[...]
    cos = jnp.cos(ang)
    sin = jnp.sin(ang)
    sin_lo = sin * lo_ref[...]
    sin_hi = sin * hi_ref[...]
    half = HEAD_DIM // 2

    def rope(t):
        return t * cos + pltpu.roll(t, LANES - half, 1) * sin_lo + pltpu.roll(t, half, 1) * sin_hi

    q = _dot(h, wq_ref[...])
    for c in range(q_out.shape[1] // LANES):
        cols = slice(c * LANES, (c + 1) * LANES)
        q_out[:, cols] = (rope(q[:, cols]) * (QK_SCALE * LOG2E)).astype(jnp.bfloat16)
    k = _dot(h, wk_ref[...])
    for c in range(k_out.shape[1] // LANES):
        cols = slice(c * LANES, (c + 1) * LANES)
        k_out[:, cols] = rope(k[:, cols]).astype(jnp.bfloat16)
    v_out[...] = _dot(h, wv_ref[...]).astype(jnp.bfloat16)


def _dup_heads(w, heads):
    d = w.shape[0]
    w = w.reshape(d, heads, 1, HEAD_DIM)
    return jnp.broadcast_to(w, (d, heads, 2, HEAD_DIM)).reshape(d, heads * 2 * HEAD_DIM)


def _swa_proj(x, pos, g, w_in):
    T, D = x.shape
    tm = PROJ_TM
    qd = SWA_Q_HEADS * HEAD_DIM
    kvd = SWA_KV_HEADS * HEAD_DIM
    kw = SWA_KV_HEADS * 2 * HEAD_DIM
    bf = lambda a: a.astype(jnp.bfloat16)
    wq = w_in[:, :qd]
    wk = w_in[:, qd:qd + kvd]
    wv = w_in[:, qd + kvd:]
    half = HEAD_DIM // 2
    inv_freq = 1.0 / (ROPE_THETA ** (jnp.arange(0, HEAD_DIM, 2, dtype=jnp.float32) / HEAD_DIM))
    freq = jnp.tile(inv_freq, LANES // half)[None, :]
    first_half = (np.arange(LANES) % HEAD_DIM) < half
    lo = np.where(first_half, -1.0, 0.0).astype(np.float32)[None, :]
    hi = np.where(first_half, 0.0, 1.0).astype(np.float32)[None, :]
    row = lambda width: pl.BlockSpec((tm, width), lambda t: (t, 0))
    return pl.pallas_call(
        _swa_proj_kernel,
        out_shape=(jax.ShapeDtypeStruct((T, qd), jnp.bfloat16),
                   jax.ShapeDtypeStruct((T, kw), jnp.bfloat16),
                   jax.ShapeDtypeStruct((T, kw), jnp.bfloat16)),
        grid=(T // tm,),
        in_specs=[row(D), row(1), _const_spec((1, D)),
                  _const_spec((D, qd)), _const_spec((D, kw)), _const_spec((D, kw)),
                  _const_spec((1, LANES)), _const_spec((1, LANES)), _const_spec((1, LANES))],
        out_specs=(row(qd), row(kw), row(kw)),
        compiler_params=pltpu.CompilerParams(
            dimension_semantics=("arbitrary",), vmem_limit_bytes=VMEM_LIMIT),
        name="swa_proj",
    )(x, pos, g[None, :], bf(wq), bf(_dup_heads(wk, SWA_KV_HEADS)), bf(_dup_heads(wv, SWA_KV_HEADS)),
      freq, jnp.asarray(lo), jnp.asarray(hi))


def _swa_attn_kernel(sink_ref, q_ref, kp_ref, kc_ref, vp_ref, vc_ref, o_ref, s_ref):
    n = pl.program_id(1)
    blk = WINDOW
    cols_per_group = SWA_GROUP // 2
    kb = jnp.concatenate([kp_ref[0], kc_ref[0]], axis=0).astype(jnp.float32)
    vbt = jnp.concatenate([vp_ref[0], vc_ref[0]], axis=0).T
    key = lax.broadcasted_iota(jnp.int32, (2 * blk, blk), 0)
    qry = lax.broadcasted_iota(jnp.int32, (2 * blk, blk), 1)
    diff = qry + blk - key
    first_key = jnp.where(n > 0, 0, blk)
    mask = (diff >= 0) & (diff < WINDOW) & (key >= first_key)
    klane = lax.broadcasted_iota(jnp.int32, (2 * blk, LANES), 1)
    top = lax.broadcasted_iota(jnp.int32, (LANES, blk), 0) < HEAD_DIM
    maxes = {}
    for g in range(SWA_KV_HEADS):
        gl = slice(g * LANES, (g + 1) * LANES)
        qt = jnp.concatenate([q_ref[0, :, c * LANES:(c + 1) * LANES].T
                              for c in range(g * cols_per_group, (g + 1) * cols_per_group)], axis=1)
        for par in range(2):
            keep = (klane < HEAD_DIM) if par == 0 else (klane >= HEAD_DIM)
            kpar = jnp.where(keep, kb[:, gl], 0.0).astype(jnp.bfloat16)
            st = _dot(kpar, qt)
            st = jnp.concatenate([jnp.where(mask, st[:, c * blk:(c + 1) * blk], NEG_BIG)
                                  for c in range(cols_per_group)], axis=1)
            s_ref[2 * g + par] = st
            maxes[g, par] = jnp.max(st, axis=0, keepdims=True)
    for g in range(SWA_KV_HEADS):
        gl = slice(g * LANES, (g + 1) * LANES)
        outs = []
        for par in range(2):
            sink = sink_ref[2 * g + par:2 * g + par + 1, :]
            m = jnp.maximum(maxes[g, par], sink)
            e = jnp.exp2(s_ref[2 * g + par] - m)
            denom = jnp.sum(e, axis=0, keepdims=True) + jnp.exp2(sink - m)
            ot = _dot(vbt[gl, :], e.astype(jnp.bfloat16))
            outs.append(ot * (1.0 / denom))
        for c in range(cols_per_group):
            both = jnp.where(top, outs[0][:, c * blk:(c + 1) * blk], outs[1][:, c * blk:(c + 1) * blk])
            col = g * cols_per_group + c
            o_ref[0, :, col * LANES:(col + 1) * LANES] = both.T.astype(jnp.bfloat16)


def _swa_attn(q, k, v, sinks):
    B, S, qd = q.shape
    kw = k.shape[2]
    blk = WINDOW
    cur = lambda b, n: (b, n, 0)
    prev = lambda b, n: (b, jnp.maximum(n - 1, 0), 0)
    cols_per_group = SWA_GROUP // 2
    sink_rows = (sinks * LOG2E).reshape(SWA_KV_HEADS, cols_per_group, 2).transpose(0, 2, 1)
    sink_rows = jnp.repeat(sink_rows.reshape(2 * SWA_KV_HEADS, cols_per_group), blk, axis=1)
    sink_rows = jnp.pad(sink_rows, ((0, 8 - 2 * SWA_KV_HEADS), (0, 0)))
    return pl.pallas_call(
        _swa_attn_kernel,
        out_shape=jax.ShapeDtypeStruct((B, S, qd), jnp.bfloat16),
        grid=(B, S // blk),
        in_specs=[_const_spec(sink_rows.shape),
                  pl.BlockSpec((1, blk, qd), cur),
                  pl.BlockSpec((1, blk, kw), prev), pl.BlockSpec((1, blk, kw), cur),
                  pl.BlockSpec((1, blk, kw), prev), pl.BlockSpec((1, blk, kw), cur)],
        out_specs=pl.BlockSpec((1, blk, qd), cur),
        scratch_shapes=[pltpu.VMEM((2 * SWA_KV_HEADS, 2 * blk, cols_per_group * blk), jnp.float32)],
        compiler_params=pltpu.CompilerParams(
            dimension_semantics=("arbitrary", "arbitrary"), vmem_limit_bytes=VMEM_LIMIT),
        name="swa_attn",
    )(sink_rows, q, k, k, v, v)


def kernel(x, positions, fox_w_in, fox_b_f, fox_w_out, swa_w_in, swa_sinks, swa_w_out,
           norm_pre_mix, norm_post_mix, norm_pre_ffn, norm_post_ffn, mlp_w_up, mlp_w_down):
    B, S, D = x.shape
    T = B * S
    q, k, v, stats = _fox_proj(x, norm_pre_mix[0], fox_w_in[0], fox_b_f[0])
    o = _fox_flash(q, k, v, stats)
    x1 = _out_mlp(o.reshape(T, -1), x.reshape(T, D), fox_w_out[0], norm_post_mix[0], norm_pre_ffn[0],
                  mlp_w_up[0], mlp_w_down[0], norm_post_ffn[0])
    q, k, v = _swa_proj(x1, positions.reshape(T, 1), norm_pre_mix[1], swa_w_in[0])
    o = _swa_attn(q.reshape(B, S, -1), k.reshape(B, S, -1), v.reshape(B, S, -1), swa_sinks[0])
    x2 = _out_mlp(o.reshape(T, -1), x1, swa_w_out[0], norm_post_mix[1], norm_pre_ffn[1],
                  mlp_w_up[1], mlp_w_down[1], norm_post_ffn[1])
    return x2.reshape(B, S, D)
```

```python
import functools
import math

import jax
import jax.numpy as jnp
import numpy as np
from jax import lax
from jax.experimental import pallas as pl
from jax.experimental.pallas import tpu as pltpu

D_MODEL = 1024
HEAD_DIM = 64
FOX_HEADS = 16
SWA_Q_HEADS = 16
SWA_KV_HEADS = 2
SWA_GROUP = SWA_Q_HEADS // SWA_KV_HEADS
WINDOW = 128
D_FF = 4 * D_MODEL
ROPE_THETA = 10000.0
NORM_EPS = 1e-6

LANES = 128
HEAD_AUG = LANES
LOG2E = 1.4426950408889634
QK_SCALE = HEAD_DIM ** -0.5
NEG_BIG = -1e30
SKIP_LOG2 = 152.0
NORM_SLACK = 1.02
SHIFT_GAP_LOG2 = 100.0
VMEM_LIMIT = 56 * 1024 * 1024

DEC0 = HEAD_DIM
N_SPLIT = 3
ONES_LANE = N_SPLIT * FOX_HEADS

PROJ_TM = 512
FLASH_BQ = 512
MLP_CHUNK = 1024


def _rms(x, g):
    return x * lax.rsqrt(jnp.mean(x * x, axis=-1, keepdims=True) + NORM_EPS) * g


def _dot(a, b):
    return jnp.dot(a, b, preferred_element_type=jnp.float32)


def _dot_nt(a, b):
    return lax.dot_general(a, b, (((1,), (1,)), ((), ())), preferred_element_type=jnp.float32)


def _split3(x):
    hi = x.astype(jnp.bfloat16)
    r = x - hi.astype(jnp.float32)
    mid = r.astype(jnp.bfloat16)
    lo = (r - mid.astype(jnp.float32)).astype(jnp.bfloat16)
    return hi, mid, lo


def _const_spec(shape):
    nd = len(shape)
    return pl.BlockSpec(shape, lambda *_: (0,) * nd, pipeline_mode=pl.Buffered(1))


def _fox_proj_kernel(x_ref, g_ref, wq_ref, wk_ref, wv_ref, wf_ref, b3_ref, place_ref,
                     tri_ref, sel_ref, q_out, k_out, v_out, stats_out, carry_ref, *, tm):
    @pl.when(pl.program_id(1) == 0)
    def _():
        carry_ref[...] = jnp.zeros_like(carry_ref)

    h = _rms(x_ref[0], g_ref[...]).astype(jnp.bfloat16)

    z = _dot(h, wf_ref[...]) + b3_ref[...]
    logf = jnp.minimum(z, 0.0) - jnp.log1p(jnp.exp(-jnp.abs(z)))

    tri = tri_ref[...]
    hi, mid, lo = _split3(logf)
    c = _dot(tri, hi) + _dot(tri, mid) + _dot(tri, lo) + carry_ref[...]
    carry_ref[...] = c[tm - 1:tm, :]

    chi, cmid, clo = (p.astype(jnp.float32) for p in _split3(c * LOG2E))
    lane = lax.broadcasted_iota(jnp.int32, (tm, LANES), 1)
    parts = jnp.where(lane < FOX_HEADS, chi,
                      jnp.where(lane < 2 * FOX_HEADS, cmid,
                                jnp.where(lane < ONES_LANE, clo,
                                          jnp.where(lane == ONES_LANE, 1.0, 0.0))))
    parts = parts.astype(jnp.bfloat16)

    width = FOX_HEADS * HEAD_AUG
    q = _dot(h, wq_ref[...]) * (QK_SCALE * LOG2E)
    k = _dot(h, wk_ref[...])
    v = _dot(h, wv_ref[...])
    qn2 = _dot((q * q).astype(jnp.bfloat16), sel_ref[...])
    kn2 = _dot((k * k).astype(jnp.bfloat16), sel_ref[...])

    low = lane < HEAD_DIM
    v_tail = jnp.where(lane == HEAD_DIM, 1.0, 0.0)
    for pr in range(FOX_HEADS // 2):
        pcols = slice(pr * LANES, (pr + 1) * LANES)
        acols = slice(2 * pr * HEAD_AUG, (2 * pr + 2) * HEAD_AUG)
        q_tail = _dot(parts, place_ref[:, acols])
        k_tail = _dot(parts, place_ref[:, width + 2 * pr * HEAD_AUG:width + (2 * pr + 2) * HEAD_AUG])
        for src, tail, out in ((q[:, pcols], q_tail, q_out), (k[:, pcols], k_tail, k_out),
                               (v[:, pcols], None, v_out)):
            for par, head in enumerate((src, pltpu.roll(src, HEAD_DIM, 1))):
                fill = v_tail if tail is None else tail[:, par * HEAD_AUG:(par + 1) * HEAD_AUG]
                hcols = slice((2 * pr + par) * HEAD_AUG, (2 * pr + par + 1) * HEAD_AUG)
                out[0, :, hcols] = jnp.where(low, head, fill).astype(jnp.bfloat16)

    c2 = c * LOG2E
    stats_out[0, 0] = jnp.zeros((8, LANES), jnp.float32)
    stats_out[0, 0, 0:1, :] = jnp.sqrt(jnp.max(qn2, axis=0, keepdims=True))
    stats_out[0, 0, 1:2, :] = jnp.sqrt(jnp.max(kn2, axis=0, keepdims=True))
    stats_out[0, 0, 2:3, :] = c2[0:1, :]
    stats_out[0, 0, 3:4, :] = c2[tm - 1:tm, :]
    diag = _dot((q * k).astype(jnp.bfloat16), sel_ref[...])
    stats_out[0, 0, 4:5, :] = jnp.min(diag, axis=0, keepdims=True)


def _fox_placement():
    width = FOX_HEADS * HEAD_AUG
    p = np.zeros((LANES, 2 * width), np.float32)
    for h in range(FOX_HEADS):
        base = h * HEAD_AUG + DEC0
        for s in range(N_SPLIT):
            p[s * FOX_HEADS + h, base + s] = 1.0
            p[ONES_LANE, base + N_SPLIT + s] = 1.0
            p[ONES_LANE, width + base + s] = 1.0
            p[s * FOX_HEADS + h, width + base + N_SPLIT + s] = -1.0
    return p


def _fox_proj(x, g, w_in, b_f):
    B, S, D = x.shape
    tm = PROJ_TM
    hd_all = FOX_HEADS * HEAD_DIM
    width = FOX_HEADS * HEAD_AUG
    wq = w_in[:, :hd_all].astype(jnp.bfloat16)
    wk = w_in[:, hd_all:2 * hd_all].astype(jnp.bfloat16)
    wv = w_in[:, 2 * hd_all:3 * hd_all].astype(jnp.bfloat16)
    wf = w_in[:, 3 * hd_all:]
    wf3 = jnp.concatenate([wf] * N_SPLIT + [jnp.zeros((D, LANES - ONES_LANE), wf.dtype)], axis=1)
    b3 = jnp.concatenate([b_f] * N_SPLIT + [jnp.zeros((LANES - ONES_LANE,), b_f.dtype)])[None, :]
    place = jnp.asarray(_fox_placement(), jnp.bfloat16)
    tri = jnp.asarray(np.tril(np.ones((tm, tm), np.float32)), jnp.bfloat16)
    sel = np.zeros((hd_all, LANES), np.float32)
    for hd in range(FOX_HEADS):
        sel[hd * HEAD_DIM:(hd + 1) * HEAD_DIM, hd] = 1.0

    out_sds = jax.ShapeDtypeStruct((B, S, width), jnp.bfloat16)
    out_spec = pl.BlockSpec((1, tm, width), lambda b, t: (b, t, 0))
    stats_sds = jax.ShapeDtypeStruct((B, S // tm, 8, LANES), jnp.float32)
    stats_spec = pl.BlockSpec((1, 1, 8, LANES), lambda b, t: (b, t, 0, 0))
    return pl.pallas_call(
        functools.partial(_fox_proj_kernel, tm=tm),
        out_shape=(out_sds, out_sds, out_sds, stats_sds),
        grid=(B, S // tm),
        in_specs=[
            pl.BlockSpec((1, tm, D), lambda b, t: (b, t, 0)),
            _const_spec((1, D)),
            _const_spec((D, hd_all)), _const_spec((D, hd_all)), _const_spec((D, hd_all)),
            _const_spec((D, LANES)), _const_spec((1, LANES)),
            _const_spec((LANES, 2 * width)),
            _const_spec((tm, tm)), _const_spec((hd_all, LANES)),
        ],
        out_specs=(out_spec, out_spec, out_spec, stats_spec),
        scratch_shapes=[pltpu.VMEM((1, LANES), jnp.float32)],
        compiler_params=pltpu.CompilerParams(
            dimension_semantics=("arbitrary", "arbitrary"), vmem_limit_bytes=VMEM_LIMIT),
        name="fox_proj",
    )(x, g[None, :], wq, wk, wv, wf3.astype(jnp.bfloat16), b3, place, tri,
      jnp.asarray(sel, jnp.bfloat16))


def _flash_transpose_v(v_ref, vt_ref, qi, bq, seq):
    @pl.when(qi == 0)
    def _():
        for c in range(seq // bq):
            rows = slice(c * bq, (c + 1) * bq)
            vt_ref[:, rows] = v_ref[0, rows, :].T


def _flash_finalize(acc_ref, o_ref, bq):
    outs = []
    for hh in range(2):
        acc = acc_ref[hh]
        outs.append((acc * (1.0 / acc[HEAD_DIM:HEAD_DIM + 1, :])).T)
    lane = lax.broadcasted_iota(jnp.int32, (bq, LANES), 1)
    o = jnp.where(lane < HEAD_DIM, outs[0], pltpu.roll(outs[1], HEAD_DIM, 1))
    o_ref[0] = o.astype(jnp.bfloat16)


def _fox_flash_kernel(first_ref, q_ref, k_ref, v_ref, o_ref, vt_ref, qt_ref, m_ref, acc_ref, s_ref, p_ref,
                      mb_ref, *, bq, seq):
    qi = pl.program_id(2)
    heads = 2
    nq = seq // bq
    head0 = (pl.program_id(0) * FOX_HEADS + heads * pl.program_id(1)) * nq + qi
    j0 = jnp.minimum(first_ref[head0], first_ref[head0 + nq])
    _flash_transpose_v(v_ref, vt_ref, qi, bq, seq)
    qt_ref[...] = q_ref[0].T
    m_ref[...] = jnp.full_like(m_ref, NEG_BIG)
    acc_ref[...] = jnp.zeros_like(acc_ref)

    def scores(j, slot, masked):
        start = pl.multiple_of(j * bq, bq)
        for hh in range(heads):
            cols = slice(hh * HEAD_AUG, (hh + 1) * HEAD_AUG)
            st = _dot(k_ref[0, pl.ds(start, bq), cols], qt_ref[cols, :])
            if masked:
                key = lax.broadcasted_iota(jnp.int32, (bq, bq), 0)
                qry = lax.broadcasted_iota(jnp.int32, (bq, bq), 1)
                st = jnp.where(key - qry <= (qi - j) * bq, st, NEG_BIG)
            s_ref[2 * slot + hh] = st
            mb_ref[2 * slot + hh] = jnp.max(st, axis=0, keepdims=True)

    def consume(j, slot):
        start = pl.multiple_of(j * bq, bq)
        alpha = []
        for hh in range(heads):
            m_prev = m_ref[hh]
            m_new = jnp.maximum(m_prev, mb_ref[2 * slot + hh])
            m_ref[hh] = m_new
            p_ref[hh] = jnp.exp2(s_ref[2 * slot + hh] - m_new).astype(jnp.bfloat16)
            alpha.append(jnp.exp2(m_prev - m_new))
        for hh in range(heads):
            cols = slice(hh * HEAD_AUG, (hh + 1) * HEAD_AUG)
            vt = vt_ref[cols, pl.ds(start, bq)]
            acc_ref[hh] = alpha[hh] * acc_ref[hh] + _dot(vt, p_ref[hh])

    def half_step(j, slot, masked):
        scores(j + 1, 1 - slot, masked)
        consume(j, slot)

    span = qi - j0
    pairs = jnp.maximum(span - 1, 0) // 2
    rest = span - 2 * pairs
    scores(j0, 0, True)

    def body(t, carry):
        half_step(j0 + 2 * t, 0, False)
        half_step(j0 + 2 * t + 1, 1, False)
        return carry

    lax.fori_loop(0, pairs, body, 0)

    @pl.when(rest >= 1)
    def _():
        half_step(j0 + 2 * pairs, 0, True)

    @pl.when(rest == 2)
    def _():
        half_step(j0 + 2 * pairs + 1, 1, True)

    @pl.when(rest == 1)
    def _():
        consume(qi, 1)

    @pl.when(rest != 1)
    def _():
        consume(qi, 0)

    _flash_finalize(acc_ref, o_ref, bq)


def _fox_flash_fixed_kernel(first_ref, kall_ref, q_ref, k_ref, v_ref, o_ref, vt_ref, qt_ref, r_ref, acc_ref,
                            p_ref, *, bq, seq):
    b, hp, qi = pl.program_id(0), pl.program_id(1), pl.program_id(2)
    heads = 2
    nq = seq // bq
    head0 = (b * FOX_HEADS + heads * hp) * nq + qi
    first = (first_ref[head0], first_ref[head0 + nq])
    j0 = jnp.maximum(first[0], first[1])
    _flash_transpose_v(v_ref, vt_ref, qi, bq, seq)
    qt_ref[...] = q_ref[0].T
    acc_ref[...] = jnp.zeros_like(acc_ref)
    for hh in range(heads):
        qf = qt_ref[hh * HEAD_AUG:hh * HEAD_AUG + HEAD_DIM, :].astype(jnp.float32)
        qnorm = jnp.sqrt(jnp.sum(qf * qf, axis=0, keepdims=True))
        r_ref[hh] = qnorm * (NORM_SLACK * kall_ref[b * FOX_HEADS + heads * hp + hh])

    def blocks(j, nblk, masked, which=(0, 1)):
        start = pl.multiple_of(j * bq, bq)
        for hh in which:
            cols = slice(hh * HEAD_AUG, (hh + 1) * HEAD_AUG)
            for u in range(nblk):
                st = _dot(k_ref[0, pl.ds(start + u * bq, bq), cols], qt_ref[cols, :])
                if masked:
                    key = lax.broadcasted_iota(jnp.int32, (bq, bq), 0)
                    qry = lax.broadcasted_iota(jnp.int32, (bq, bq), 1)
                    st = jnp.where(key <= qry, st, NEG_BIG)
                p_ref[hh, u * bq:(u + 1) * bq, :] = jnp.exp2(st - r_ref[hh]).astype(jnp.bfloat16)
        for hh in which:
            cols = slice(hh * HEAD_AUG, (hh + 1) * HEAD_AUG)
            vt = vt_ref[cols, pl.ds(start, nblk * bq)]
            acc_ref[hh] += _dot(vt, p_ref[hh, 0:nblk * bq, :])

    for hh in range(heads):
        @pl.when(first[hh] < j0)
        def _(hh=hh):
            lo = first[hh]
            count = j0 - lo

            def body1(t, carry):
                blocks(lo + 2 * t, 2, False, (hh,))
                return carry

            lax.fori_loop(0, count // 2, body1, 0)

            @pl.when(count % 2 == 1)
            def _():
                blocks(j0 - 1, 1, False, (hh,))

    span = qi - j0
    pairs = span // 2

    def body(t, carry):
        blocks(j0 + 2 * t, 2, False)
        return carry

    lax.fori_loop(0, pairs, body, 0)

    @pl.when(span - 2 * pairs == 1)
    def _():
        blocks(qi - 1, 1, False)

    blocks(qi, 1, True)
    _flash_finalize(acc_ref, o_ref, bq)


def _first_key_block(stats):
    qmax, kmax = stats[:, :, 0, :FOX_HEADS], stats[:, :, 1, :FOX_HEADS]
    c_first, c_last = stats[:, :, 2, :FOX_HEADS], stats[:, :, 3, :FOX_HEADS]
    kall = jnp.max(kmax, axis=1, keepdims=True)
    thr = c_first + SKIP_LOG2 + NORM_SLACK * 2.0 * qmax * kall
    skippable = c_last[:, None, :, :] > thr[:, :, None, :]
    first = jnp.sum(skippable.astype(jnp.int32), axis=2)
    nq = stats.shape[1]
    first = jnp.minimum(first, jnp.arange(nq, dtype=jnp.int32)[None, :, None])
    return first.transpose(0, 2, 1).reshape(-1)


def _fox_flash(q, k, v, stats):
    B, S, width = q.shape
    bq = FLASH_BQ
    pair = 2 * HEAD_AUG
    first = _first_key_block(stats)
    qmax, kmax, diag_min = (stats[:, :, r, :FOX_HEADS] for r in (0, 1, 4))
    kall = jnp.max(kmax, axis=1)
    gap = NORM_SLACK * qmax * kall[:, None, :] - diag_min
    fixed_shift_ok = jnp.max(gap) < SHIFT_GAP_LOG2

    blk = lambda shape, index, **kw: pl.BlockSpec(shape, lambda b, hp, i, first: index(b, hp, i), **kw)
    tensor_specs = [
        blk((1, bq, pair), lambda b, hp, i: (b, i, hp)),
        blk((1, S, pair), lambda b, hp, i: (b, 0, hp)),
        blk((1, S, pair), lambda b, hp, i: (b, 0, hp), pipeline_mode=pl.Buffered(1)),
    ]
    common = dict(
        out_shape=jax.ShapeDtypeStruct((B, S, FOX_HEADS * HEAD_DIM), jnp.bfloat16),
        compiler_params=pltpu.CompilerParams(
            dimension_semantics=("arbitrary", "arbitrary", "arbitrary"),
            vmem_limit_bytes=VMEM_LIMIT))
    grid = (B, FOX_HEADS // 2, S // bq)
    out_spec = blk((1, bq, 2 * HEAD_DIM), lambda b, hp, i: (b, i, hp))
    vt_qt = [pltpu.VMEM((pair, S), jnp.bfloat16), pltpu.VMEM((pair, bq), jnp.bfloat16)]

    def fixed_shift():
        return pl.pallas_call(
            functools.partial(_fox_flash_fixed_kernel, bq=bq, seq=S),
            grid_spec=pltpu.PrefetchScalarGridSpec(
                num_scalar_prefetch=1, grid=grid,
                in_specs=[pl.BlockSpec(memory_space=pltpu.SMEM)] + tensor_specs,
                out_specs=out_spec,
                scratch_shapes=vt_qt + [pltpu.VMEM((2, 1, bq), jnp.float32),
                                        pltpu.VMEM((2, HEAD_AUG, bq), jnp.float32),
                                        pltpu.VMEM((2, 2 * bq, bq), jnp.bfloat16)]),
            name="fox_flash_fixed", **common)(first, kall.reshape(-1), q, k, v)

    def running_max():
        return pl.pallas_call(
            functools.partial(_fox_flash_kernel, bq=bq, seq=S),
            grid_spec=pltpu.PrefetchScalarGridSpec(
                num_scalar_prefetch=1, grid=grid, in_specs=tensor_specs, out_specs=out_spec,
                scratch_shapes=vt_qt + [pltpu.VMEM((2, 1, bq), jnp.float32),
                                        pltpu.VMEM((2, HEAD_AUG, bq), jnp.float32),
                                        pltpu.VMEM((4, bq, bq), jnp.float32),
                                        pltpu.VMEM((2, bq, bq), jnp.bfloat16),
                                        pltpu.VMEM((4, 1, bq), jnp.float32)]),
            name="fox_flash", **common)(first, q, k, v)

    return lax.cond(fixed_shift_ok, fixed_shift, running_max)


def _out_mlp_kernel(o_ref, x_ref, wo_ref, g_post_ref, g_pre_ref, wu_ref, wd_ref, g_ffn_ref, y_ref):
    a = _dot(o_ref[...], wo_ref[...])
    x1 = x_ref[...] + _rms(a, g_post_ref[...])
    h = _rms(x1, g_pre_ref[...]).astype(jnp.bfloat16)
    acc = jnp.zeros_like(x1)
    for c in range(D_FF // MLP_CHUNK):
        cols = slice(c * MLP_CHUNK, (c + 1) * MLP_CHUNK)
        u = jnp.maximum(_dot(h, wu_ref[:, cols]), 0.0)
        acc = acc + _dot((u * u).astype(jnp.bfloat16), wd_ref[cols, :])
    y_ref[...] = x1 + _rms(acc, g_ffn_ref[...])


def _out_mlp(o, x, w_out, g_post, g_pre, w_up, w_down, g_ffn):
    T, D = x.shape
    tm = PROJ_TM
    row = lambda width: pl.BlockSpec((tm, width), lambda t: (t, 0))
    return pl.pallas_call(
        _out_mlp_kernel,
        out_shape=jax.ShapeDtypeStruct((T, D), jnp.float32),
        grid=(T // tm,),
        in_specs=[row(o.shape[1]), row(D),
                  _const_spec(w_out.shape), _const_spec((1, D)), _const_spec((1, D)),
                  _const_spec(w_up.shape), _const_spec(w_down.shape), _const_spec((1, D))],
        out_specs=row(D),
        compiler_params=pltpu.CompilerParams(
            dimension_semantics=("arbitrary",), vmem_limit_bytes=VMEM_LIMIT),
        name="out_mlp",
    )(o, x, w_out.astype(jnp.bfloat16), g_post[None, :], g_pre[None, :],
      w_up.astype(jnp.bfloat16), w_down.astype(jnp.bfloat16), g_ffn[None, :])


def _swa_proj_kernel(x_ref, pos_ref, g_ref, wq_ref, wk_ref, wv_ref, freq_ref, lo_ref, hi_ref,
                     q_out, k_out, v_out):
    h = _rms(x_ref[...], g_ref[...]).astype(jnp.bfloat16)
    ang = pos_ref[...].astype(jnp.float32) * freq_ref[...]
    cos = jnp.cos(ang)
    sin = jnp.sin(ang)
    sin_lo = sin * lo_ref[...]
    sin_hi = sin * hi_ref[...]
    half = HEAD_DIM // 2

    def rope(t):
        return t * cos + pltpu.roll(t, LANES - half, 1) * sin_lo + pltpu.roll(t, half, 1) * sin_hi

    q = _dot(h, wq_ref[...])
    for c in range(q_out.shape[1] // LANES):
        cols = slice(c * LANES, (c + 1) * LANES)
        q_out[:, cols] = (rope(q[:, cols]) * (QK_SCALE * LOG2E)).astype(jnp.bfloat16)
    k = _dot(h, wk_ref[...])
    for c in range(k_out.shape[1] // LANES):
        cols = slice(c * LANES, (c + 1) * LANES)
        k_out[:, cols] = rope(k[:, cols]).astype(jnp.bfloat16)
    v_out[...] = _dot(h, wv_ref[...]).astype(jnp.bfloat16)


def _dup_heads(w, heads):
    d = w.shape[0]
    w = w.reshape(d, heads, 1, HEAD_DIM)
    return jnp.broadcast_to(w, (d, heads, 2, HEAD_DIM)).reshape(d, heads * 2 * HEAD_DIM)


def _swa_proj(x, pos, g, w_in):
    T, D = x.shape
    tm = PROJ_TM
    qd = SWA_Q_HEADS * HEAD_DIM
    kvd = SWA_KV_HEADS * HEAD_DIM
    kw = SWA_KV_HEADS * 2 * HEAD_DIM
    bf = lambda a: a.astype(jnp.bfloat16)
    wq = w_in[:, :qd]
    wk = w_in[:, qd:qd + kvd]
    wv = w_in[:, qd + kvd:]
    half = HEAD_DIM // 2
    inv_freq = 1.0 / (ROPE_THETA ** (jnp.arange(0, HEAD_DIM, 2, dtype=jnp.float32) / HEAD_DIM))
    freq = jnp.tile(inv_freq, LANES // half)[None, :]
    first_half = (np.arange(LANES) % HEAD_DIM) < half
    lo = np.where(first_half, -1.0, 0.0).astype(np.float32)[None, :]
    hi = np.where(first_half, 0.0, 1.0).astype(np.float32)[None, :]
    row = lambda width: pl.BlockSpec((tm, width), lambda t: (t, 0))
    return pl.pallas_call(
        _swa_proj_kernel,
        out_shape=(jax.ShapeDtypeStruct((T, qd), jnp.bfloat16),
                   jax.ShapeDtypeStruct((T, kw), jnp.bfloat16),
                   jax.ShapeDtypeStruct((T, kw), jnp.bfloat16)),
        grid=(T // tm,),
        in_specs=[row(D), row(1), _const_spec((1, D)),
                  _const_spec((D, qd)), _const_spec((D, kw)), _const_spec((D, kw)),
                  _const_spec((1, LANES)), _const_spec((1, LANES)), _const_spec((1, LANES))],
        out_specs=(row(qd), row(kw), row(kw)),
        compiler_params=pltpu.CompilerParams(
            dimension_semantics=("arbitrary",), vmem_limit_bytes=VMEM_LIMIT),
        name="swa_proj",
    )(x, pos, g[None, :], bf(wq), bf(_dup_heads(wk, SWA_KV_HEADS)), bf(_dup_heads(wv, SWA_KV_HEADS)),
      freq, jnp.asarray(lo), jnp.asarray(hi))


def _swa_attn_kernel(sink_ref, q_ref, kp_ref, kc_ref, vp_ref, vc_ref, o_ref, s_ref):
    n = pl.program_id(1)
    blk = WINDOW
    cols_per_group = SWA_GROUP // 2
    kb = jnp.concatenate([kp_ref[0], kc_ref[0]], axis=0).astype(jnp.float32)
    vbt = jnp.concatenate([vp_ref[0], vc_ref[0]], axis=0).T
    key = lax.broadcasted_iota(jnp.int32, (2 * blk, blk), 0)
    qry = lax.broadcasted_iota(jnp.int32, (2 * blk, blk), 1)
    diff = qry + blk - key
    first_key = jnp.where(n > 0, 0, blk)
    mask = (diff >= 0) & (diff < WINDOW) & (key >= first_key)
    klane = lax.broadcasted_iota(jnp.int32, (2 * blk, LANES), 1)
    top = lax.broadcasted_iota(jnp.int32, (LANES, blk), 0) < HEAD_DIM
    maxes = {}
    for g in range(SWA_KV_HEADS):
        gl = slice(g * LANES, (g + 1) * LANES)
        qt = jnp.concatenate([q_ref[0, :, c * LANES:(c + 1) * LANES].T
                              for c in range(g * cols_per_group, (g + 1) * cols_per_group)], axis=1)
        for par in range(2):
            keep = (klane < HEAD_DIM) if par == 0 else (klane >= HEAD_DIM)
            kpar = jnp.where(keep, kb[:, gl], 0.0).astype(jnp.bfloat16)
            st = _dot(kpar, qt)
            st = jnp.concatenate([jnp.where(mask, st[:, c * blk:(c + 1) * blk], NEG_BIG)
                                  for c in range(cols_per_group)], axis=1)
            s_ref[2 * g + par] = st
            maxes[g, par] = jnp.max(st, axis=0, keepdims=True)
    for g in range(SWA_KV_HEADS):
        gl = slice(g * LANES, (g + 1) * LANES)
        outs = []
        for par in range(2):
            sink = sink_ref[2 * g + par:2 * g + par + 1, :]
            m = jnp.maximum(maxes[g, par], sink)
            e = jnp.exp2(s_ref[2 * g + par] - m)
            denom = jnp.sum(e, axis=0, keepdims=True) + jnp.exp2(sink - m)
            ot = _dot(vbt[gl, :], e.astype(jnp.bfloat16))
            outs.append(ot * (1.0 / denom))
        for c in range(cols_per_group):
            both = jnp.where(top, outs[0][:, c * blk:(c + 1) * blk], outs[1][:, c * blk:(c + 1) * blk])
            col = g * cols_per_group + c
            o_ref[0, :, col * LANES:(col + 1) * LANES] = both.T.astype(jnp.bfloat16)


def _swa_attn(q, k, v, sinks):
    B, S, qd = q.shape
    kw = k.shape[2]
    blk = WINDOW
    cur = lambda b, n: (b, n, 0)
    prev = lambda b, n: (b, jnp.maximum(n - 1, 0), 0)
    cols_per_group = SWA_GROUP // 2
    sink_rows = (sinks * LOG2E).reshape(SWA_KV_HEADS, cols_per_group, 2).transpose(0, 2, 1)
    sink_rows = jnp.repeat(sink_rows.reshape(2 * SWA_KV_HEADS, cols_per_group), blk, axis=1)
    sink_rows = jnp.pad(sink_rows, ((0, 8 - 2 * SWA_KV_HEADS), (0, 0)))
    return pl.pallas_call(
        _swa_attn_kernel,
        out_shape=jax.ShapeDtypeStruct((B, S, qd), jnp.bfloat16),
        grid=(B, S // blk),
        in_specs=[_const_spec(sink_rows.shape),
                  pl.BlockSpec((1, blk, qd), cur),
                  pl.BlockSpec((1, blk, kw), prev), pl.BlockSpec((1, blk, kw), cur),
                  pl.BlockSpec((1, blk, kw), prev), pl.BlockSpec((1, blk, kw), cur)],
        out_specs=pl.BlockSpec((1, blk, qd), cur),
        scratch_shapes=[pltpu.VMEM((2 * SWA_KV_HEADS, 2 * blk, cols_per_group * blk), jnp.float32)],
        compiler_params=pltpu.CompilerParams(
            dimension_semantics=("arbitrary", "arbitrary"), vmem_limit_bytes=VMEM_LIMIT),
        name="swa_attn",
    )(sink_rows, q, k, k, v, v)


def kernel(x, positions, fox_w_in, fox_b_f, fox_w_out, swa_w_in, swa_sinks, swa_w_out,
           norm_pre_mix, norm_post_mix, norm_pre_ffn, norm_post_ffn, mlp_w_up, mlp_w_down):
    B, S, D = x.shape
    T = B * S
    q, k, v, stats = _fox_proj(x, norm_pre_mix[0], fox_w_in[0], fox_b_f[0])
    o = _fox_flash(q, k, v, stats)
    x1 = _out_mlp(o.reshape(T, -1), x.reshape(T, D), fox_w_out[0], norm_post_mix[0], norm_pre_ffn[0],
                  mlp_w_up[0], mlp_w_down[0], norm_post_ffn[0])
    q, k, v = _swa_proj(x1, positions.reshape(T, 1), norm_pre_mix[1], swa_w_in[0])
    o = _swa_attn(q.reshape(B, S, -1), k.reshape(B, S, -1), v.reshape(B, S, -1), swa_sinks[0])
    x2 = _out_mlp(o.reshape(T, -1), x1, swa_w_out[0], norm_post_mix[1], norm_pre_ffn[1],
                  mlp_w_up[1], mlp_w_down[1], norm_post_ffn[1])
    return x2.reshape(B, S, D)
```

```python
import functools
import math

import jax
import jax.numpy as jnp
import numpy as np
from jax import lax
from jax.experimental import pallas as pl
from jax.experimental.pallas import tpu as pltpu

D_MODEL = 1024
HEAD_DIM = 64
FOX_HEADS = 16
SWA_Q_HEADS = 16
SWA_KV_HEADS = 2
SWA_GROUP = SWA_Q_HEADS // SWA_KV_HEADS
WINDOW = 128
D_FF = 4 * D_MODEL
ROPE_THETA = 10000.0
NORM_EPS = 1e-6

LANES = 128
HEAD_AUG = LANES
LOG2E = 1.4426950408889634
QK_SCALE = HEAD_DIM ** -0.5
NEG_BIG = -1e30
SKIP_LOG2 = 152.0
NORM_SLACK = 1.02
DIAG_SLACK = 0.02
SHIFT_GAP_LOG2 = 100.0
VMEM_LIMIT = 56 * 1024 * 1024

DEC0 = HEAD_DIM
N_SPLIT = 3
ONES_LANE = N_SPLIT * FOX_HEADS

PROJ_TM = 512
FLASH_BQ = 512
MLP_CHUNK = 1024


def _rms(x, g):
    return x * lax.rsqrt(jnp.mean(x * x, axis=-1, keepdims=True) + NORM_EPS) * g


def _dot(a, b):
    return jnp.dot(a, b, preferred_element_type=jnp.float32)


def _dot_nt(a, b):
    return lax.dot_general(a, b, (((1,), (1,)), ((), ())), preferred_element_type=jnp.float32)


def _split3(x):
    hi = x.astype(jnp.bfloat16)
    r = x - hi.astype(jnp.float32)
    mid = r.astype(jnp.bfloat16)
    lo = (r - mid.astype(jnp.float32)).astype(jnp.bfloat16)
    return hi, mid, lo


def _const_spec(shape):
    nd = len(shape)
    return pl.BlockSpec(shape, lambda *_: (0,) * nd, pipeline_mode=pl.Buffered(1))


def _fox_proj_kernel(x_ref, g_ref, wq_ref, wk_ref, wv_ref, wf_ref, b3_ref, place_ref,
                     tri_ref, sel_ref, q_out, k_out, v_out, stats_out, carry_ref, *, tm):
    @pl.when(pl.program_id(1) == 0)
    def _():
        carry_ref[...] = jnp.zeros_like(carry_ref)

    h = _rms(x_ref[0], g_ref[...]).astype(jnp.bfloat16)

    z = _dot(h, wf_ref[...]) + b3_ref[...]
    logf = jnp.minimum(z, 0.0) - jnp.log1p(jnp.exp(-jnp.abs(z)))

    tri = tri_ref[...]
    hi, mid, lo = _split3(logf)
    c = _dot(tri, hi) + _dot(tri, mid) + _dot(tri, lo) + carry_ref[...]
    carry_ref[...] = c[tm - 1:tm, :]

    chi, cmid, clo = (p.astype(jnp.float32) for p in _split3(c * LOG2E))
    lane = lax.broadcasted_iota(jnp.int32, (tm, LANES), 1)
    parts = jnp.where(lane < FOX_HEADS, chi,
                      jnp.where(lane < 2 * FOX_HEADS, cmid,
                                jnp.where(lane < ONES_LANE, clo,
                                          jnp.where(lane == ONES_LANE, 1.0, 0.0))))
    parts = parts.astype(jnp.bfloat16)

    width = FOX_HEADS * HEAD_AUG
    q = _dot(h, wq_ref[...]) * (QK_SCALE * LOG2E)
    k = _dot(h, wk_ref[...])
    v = _dot(h, wv_ref[...])
    qn2 = _dot((q * q).astype(jnp.bfloat16), sel_ref[...])
    kn2 = _dot((k * k).astype(jnp.bfloat16), sel_ref[...])

    low = lane < HEAD_DIM
    v_tail = jnp.where(lane == HEAD_DIM, 1.0, 0.0)
    for pr in range(FOX_HEADS // 2):
        pcols = slice(pr * LANES, (pr + 1) * LANES)
        acols = slice(2 * pr * HEAD_AUG, (2 * pr + 2) * HEAD_AUG)
        q_tail = _dot(parts, place_ref[:, acols])
        k_tail = _dot(parts, place_ref[:, width + 2 * pr * HEAD_AUG:width + (2 * pr + 2) * HEAD_AUG])
        for src, tail, out in ((q[:, pcols], q_tail, q_out), (k[:, pcols], k_tail, k_out),
                               (v[:, pcols], None, v_out)):
            for par, head in enumerate((src, pltpu.roll(src, HEAD_DIM, 1))):
                fill = v_tail if tail is None else tail[:, par * HEAD_AUG:(par + 1) * HEAD_AUG]
                hcols = slice((2 * pr + par) * HEAD_AUG, (2 * pr + par + 1) * HEAD_AUG)
                out[0, :, hcols] = jnp.where(low, head, fill).astype(jnp.bfloat16)

    c2 = c * LOG2E
    stats_out[0, 0] = jnp.zeros((8, LANES), jnp.float32)
    stats_out[0, 0, 0:1, :] = jnp.sqrt(jnp.max(qn2, axis=0, keepdims=True))
    stats_out[0, 0, 1:2, :] = jnp.sqrt(jnp.max(kn2, axis=0, keepdims=True))
    stats_out[0, 0, 2:3, :] = c2[0:1, :]
    stats_out[0, 0, 3:4, :] = c2[tm - 1:tm, :]
    diag = _dot((q * k).astype(jnp.bfloat16), sel_ref[...])
    stats_out[0, 0, 4:5, :] = jnp.min(diag, axis=0, keepdims=True)


def _fox_placement():
    width = FOX_HEADS * HEAD_AUG
    p = np.zeros((LANES, 2 * width), np.float32)
    for h in range(FOX_HEADS):
        base = h * HEAD_AUG + DEC0
        for s in range(N_SPLIT):
            p[s * FOX_HEADS + h, base + s] = 1.0
            p[ONES_LANE, base + N_SPLIT + s] = 1.0
            p[ONES_LANE, width + base + s] = 1.0
            p[s * FOX_HEADS + h, width + base + N_SPLIT + s] = -1.0
    return p


def _fox_proj(x, g, w_in, b_f):
    B, S, D = x.shape
    tm = PROJ_TM
    hd_all = FOX_HEADS * HEAD_DIM
    width = FOX_HEADS * HEAD_AUG
    wq = w_in[:, :hd_all].astype(jnp.bfloat16)
    wk = w_in[:, hd_all:2 * hd_all].astype(jnp.bfloat16)
    wv = w_in[:, 2 * hd_all:3 * hd_all].astype(jnp.bfloat16)
    wf = w_in[:, 3 * hd_all:]
    wf3 = jnp.concatenate([wf] * N_SPLIT + [jnp.zeros((D, LANES - ONES_LANE), wf.dtype)], axis=1)
    b3 = jnp.concatenate([b_f] * N_SPLIT + [jnp.zeros((LANES - ONES_LANE,), b_f.dtype)])[None, :]
    place = jnp.asarray(_fox_placement(), jnp.bfloat16)
    tri = jnp.asarray(np.tril(np.ones((tm, tm), np.float32)), jnp.bfloat16)
    sel = np.zeros((hd_all, LANES), np.float32)
    for hd in range(FOX_HEADS):
        sel[hd * HEAD_DIM:(hd + 1) * HEAD_DIM, hd] = 1.0

    out_sds = jax.ShapeDtypeStruct((B, S, width), jnp.bfloat16)
    out_spec = pl.BlockSpec((1, tm, width), lambda b, t: (b, t, 0))
    stats_sds = jax.ShapeDtypeStruct((B, S // tm, 8, LANES), jnp.float32)
    stats_spec = pl.BlockSpec((1, 1, 8, LANES), lambda b, t: (b, t, 0, 0))
    return pl.pallas_call(
        functools.partial(_fox_proj_kernel, tm=tm),
        out_shape=(out_sds, out_sds, out_sds, stats_sds),
        grid=(B, S // tm),
        in_specs=[
            pl.BlockSpec((1, tm, D), lambda b, t: (b, t, 0)),
            _const_spec((1, D)),
            _const_spec((D, hd_all)), _const_spec((D, hd_all)), _const_spec((D, hd_all)),
            _const_spec((D, LANES)), _const_spec((1, LANES)),
            _const_spec((LANES, 2 * width)),
            _const_spec((tm, tm)), _const_spec((hd_all, LANES)),
        ],
        out_specs=(out_spec, out_spec, out_spec, stats_spec),
        scratch_shapes=[pltpu.VMEM((1, LANES), jnp.float32)],
        compiler_params=pltpu.CompilerParams(
            dimension_semantics=("arbitrary", "arbitrary"), vmem_limit_bytes=VMEM_LIMIT),
        name="fox_proj",
    )(x, g[None, :], wq, wk, wv, wf3.astype(jnp.bfloat16), b3, place, tri,
      jnp.asarray(sel, jnp.bfloat16))


def _flash_transpose_v(v_ref, vt_ref, qi, bq, seq):
    @pl.when(qi == 0)
    def _():
        for c in range(seq // bq):
            rows = slice(c * bq, (c + 1) * bq)
            vt_ref[:, rows] = v_ref[0, rows, :].T


def _flash_finalize(acc_ref, o_ref, bq):
    halves = []
    for hh in range(2):
        acc = acc_ref[hh]
        halves.append(acc[0:HEAD_DIM, :] * (1.0 / acc[HEAD_DIM:HEAD_DIM + 1, :]))
    o_ref[0] = jnp.concatenate(halves, axis=0).T.astype(jnp.bfloat16)


def _fox_flash_kernel(first_ref, q_ref, k_ref, v_ref, o_ref, vt_ref, qt_ref, m_ref, acc_ref, s_ref, p_ref,
                      mb_ref, *, bq, seq):
    qi = pl.program_id(2)
    heads = 2
    nq = seq // bq
    head0 = (pl.program_id(0) * FOX_HEADS + heads * pl.program_id(1)) * nq + qi
    j0 = jnp.minimum(first_ref[head0], first_ref[head0 + nq])
    _flash_transpose_v(v_ref, vt_ref, qi, bq, seq)
    qt_ref[...] = q_ref[0].T
    m_ref[...] = jnp.full_like(m_ref, NEG_BIG)
    acc_ref[...] = jnp.zeros_like(acc_ref)

    def scores(j, slot, masked):
        start = pl.multiple_of(j * bq, bq)
        for hh in range(heads):
            cols = slice(hh * HEAD_AUG, (hh + 1) * HEAD_AUG)
            st = _dot(k_ref[0, pl.ds(start, bq), cols], qt_ref[cols, :])
            if masked:
                key = lax.broadcasted_iota(jnp.int32, (bq, bq), 0)
                qry = lax.broadcasted_iota(jnp.int32, (bq, bq), 1)
                st = jnp.where(key - qry <= (qi - j) * bq, st, NEG_BIG)
            s_ref[2 * slot + hh] = st
            mb_ref[2 * slot + hh] = jnp.max(st, axis=0, keepdims=True)

    def consume(j, slot):
        start = pl.multiple_of(j * bq, bq)
        alpha = []
        for hh in range(heads):
            m_prev = m_ref[hh]
            m_new = jnp.maximum(m_prev, mb_ref[2 * slot + hh])
            m_ref[hh] = m_new
            p_ref[hh] = jnp.exp2(s_ref[2 * slot + hh] - m_new).astype(jnp.bfloat16)
            alpha.append(jnp.exp2(m_prev - m_new))
        for hh in range(heads):
            cols = slice(hh * HEAD_AUG, (hh + 1) * HEAD_AUG)
            vt = vt_ref[cols, pl.ds(start, bq)]
            acc_ref[hh] = alpha[hh] * acc_ref[hh] + _dot(vt, p_ref[hh])

    def half_step(j, slot, masked):
        scores(j + 1, 1 - slot, masked)
        consume(j, slot)

    span = qi - j0
    pairs = jnp.maximum(span - 1, 0) // 2
    rest = span - 2 * pairs
    scores(j0, 0, True)

    def body(t, carry):
        half_step(j0 + 2 * t, 0, False)
        half_step(j0 + 2 * t + 1, 1, False)
        return carry

    lax.fori_loop(0, pairs, body, 0)

    @pl.when(rest >= 1)
    def _():
        half_step(j0 + 2 * pairs, 0, True)

    @pl.when(rest == 2)
    def _():
        half_step(j0 + 2 * pairs + 1, 1, True)

    @pl.when(rest == 1)
    def _():
        consume(qi, 1)

    @pl.when(rest != 1)
    def _():
        consume(qi, 0)

    _flash_finalize(acc_ref, o_ref, bq)


def _fox_flash_fixed_kernel(first_ref, kall_ref, q_ref, k_ref, v_ref, o_ref, vt_ref, qt_ref, r_ref, acc_ref,
                            p_ref, *, bq, seq):
    b, hp, qi = pl.program_id(0), pl.program_id(1), pl.program_id(2)
    heads = 2
    nq = seq // bq
    head0 = (b * FOX_HEADS + heads * hp) * nq + qi
    first = (first_ref[head0], first_ref[head0 + nq])
    j0 = jnp.maximum(first[0], first[1])
    _flash_transpose_v(v_ref, vt_ref, qi, bq, seq)
    qt_ref[...] = q_ref[0].T
    acc_ref[...] = jnp.zeros_like(acc_ref)
    for hh in range(heads):
        qf = qt_ref[hh * HEAD_AUG:hh * HEAD_AUG + HEAD_DIM, :].astype(jnp.float32)
        qnorm = jnp.sqrt(jnp.sum(qf * qf, axis=0, keepdims=True))
        r_ref[hh] = qnorm * (NORM_SLACK * kall_ref[b * FOX_HEADS + heads * hp + hh])

    def blocks(j, nblk, masked, which=(0, 1)):
        start = pl.multiple_of(j * bq, bq)
        for hh in which:
            cols = slice(hh * HEAD_AUG, (hh + 1) * HEAD_AUG)
            for u in range(nblk):
                st = _dot(k_ref[0, pl.ds(start + u * bq, bq), cols], qt_ref[cols, :])
                if masked and u == nblk - 1:
                    key = lax.broadcasted_iota(jnp.int32, (bq, bq), 0)
                    qry = lax.broadcasted_iota(jnp.int32, (bq, bq), 1)
                    st = jnp.where(key <= qry, st, NEG_BIG)
                p_ref[hh, u * bq:(u + 1) * bq, :] = jnp.exp2(st - r_ref[hh]).astype(jnp.bfloat16)
        for hh in which:
            cols = slice(hh * HEAD_AUG, (hh + 1) * HEAD_AUG)
            vt = vt_ref[cols, pl.ds(start, nblk * bq)]
            acc_ref[hh] += _dot(vt, p_ref[hh, 0:nblk * bq, :])

    for hh in range(heads):
        @pl.when(first[hh] < j0)
        def _(hh=hh):
            lo = first[hh]
            count = j0 - lo

            def body1(t, carry):
                blocks(lo + 2 * t, 2, False, (hh,))
                return carry

            lax.fori_loop(0, count // 2, body1, 0)

            @pl.when(count % 2 == 1)
            def _():
                blocks(j0 - 1, 1, False, (hh,))

    span = qi - j0
    pairs = span // 2

    def body(t, carry):
        blocks(j0 + 2 * t, 2, False)
        return carry

    lax.fori_loop(0, pairs, body, 0)

    @pl.when(span - 2 * pairs == 1)
    def _():
        blocks(qi - 1, 2, True)

    @pl.when(span - 2 * pairs == 0)
    def _():
        blocks(qi, 1, True)

    _flash_finalize(acc_ref, o_ref, bq)


def _first_key_block(stats):
    qmax, kmax = stats[:, :, 0, :FOX_HEADS], stats[:, :, 1, :FOX_HEADS]
    c_first, c_last = stats[:, :, 2, :FOX_HEADS], stats[:, :, 3, :FOX_HEADS]
    diag_min = stats[:, :, 4, :FOX_HEADS]
    kall = jnp.max(kmax, axis=1, keepdims=True)
    thr = c_first + SKIP_LOG2 + (NORM_SLACK + DIAG_SLACK) * qmax * kall - diag_min
    skippable = c_last[:, None, :, :] > thr[:, :, None, :]
    first = jnp.sum(skippable.astype(jnp.int32), axis=2)
    nq = stats.shape[1]
    first = jnp.minimum(first, jnp.arange(nq, dtype=jnp.int32)[None, :, None])
    return first.transpose(0, 2, 1).reshape(-1)


def _fox_flash(q, k, v, stats):
    B, S, width = q.shape
    bq = FLASH_BQ
    pair = 2 * HEAD_AUG
    first = _first_key_block(stats)
    qmax, kmax, diag_min = (stats[:, :, r, :FOX_HEADS] for r in (0, 1, 4))
    kall = jnp.max(kmax, axis=1)
    gap = NORM_SLACK * qmax * kall[:, None, :] - diag_min
    fixed_shift_ok = jnp.max(gap) < SHIFT_GAP_LOG2

    blk = lambda shape, index, **kw: pl.BlockSpec(shape, lambda b, hp, i, first: index(b, hp, i), **kw)
    tensor_specs = [
        blk((1, bq, pair), lambda b, hp, i: (b, i, hp)),
        blk((1, S, pair), lambda b, hp, i: (b, 0, hp)),
        blk((1, S, pair), lambda b, hp, i: (b, 0, hp), pipeline_mode=pl.Buffered(1)),
    ]
    common = dict(
        out_shape=jax.ShapeDtypeStruct((B, S, FOX_HEADS * HEAD_DIM), jnp.bfloat16),
        compiler_params=pltpu.CompilerParams(
            dimension_semantics=("arbitrary", "arbitrary", "arbitrary"),
            vmem_limit_bytes=VMEM_LIMIT))
    grid = (B, FOX_HEADS // 2, S // bq)
    out_spec = blk((1, bq, 2 * HEAD_DIM), lambda b, hp, i: (b, i, hp))
    vt_qt = [pltpu.VMEM((pair, S), jnp.bfloat16), pltpu.VMEM((pair, bq), jnp.bfloat16)]

    def fixed_shift():
        return pl.pallas_call(
            functools.partial(_fox_flash_fixed_kernel, bq=bq, seq=S),
            grid_spec=pltpu.PrefetchScalarGridSpec(
                num_scalar_prefetch=1, grid=grid,
                in_specs=[pl.BlockSpec(memory_space=pltpu.SMEM)] + tensor_specs,
                out_specs=out_spec,
                scratch_shapes=vt_qt + [pltpu.VMEM((2, 1, bq), jnp.float32),
                                        pltpu.VMEM((2, HEAD_AUG, bq), jnp.float32),
                                        pltpu.VMEM((2, 2 * bq, bq), jnp.bfloat16)]),
            name="fox_flash_fixed", **common)(first, kall.reshape(-1), q, k, v)

    def running_max():
        return pl.pallas_call(
            functools.partial(_fox_flash_kernel, bq=bq, seq=S),
            grid_spec=pltpu.PrefetchScalarGridSpec(
                num_scalar_prefetch=1, grid=grid, in_specs=tensor_specs, out_specs=out_spec,
                scratch_shapes=vt_qt + [pltpu.VMEM((2, 1, bq), jnp.float32),
                                        pltpu.VMEM((2, HEAD_AUG, bq), jnp.float32),
                                        pltpu.VMEM((4, bq, bq), jnp.float32),
                                        pltpu.VMEM((2, bq, bq), jnp.bfloat16),
                                        pltpu.VMEM((4, 1, bq), jnp.float32)]),
            name="fox_flash", **common)(first, q, k, v)

    return lax.cond(fixed_shift_ok, fixed_shift, running_max)


def _out_mlp_kernel(o_ref, x_ref, wo_ref, g_post_ref, g_pre_ref, wu_ref, wd_ref, g_ffn_ref, y_ref):
    a = _dot(o_ref[...], wo_ref[...])
    x1 = x_ref[...] + _rms(a, g_post_ref[...])
    h = _rms(x1, g_pre_ref[...]).astype(jnp.bfloat16)
    acc = jnp.zeros_like(x1)
    for c in range(D_FF // MLP_CHUNK):
        cols = slice(c * MLP_CHUNK, (c + 1) * MLP_CHUNK)
        u = jnp.maximum(_dot(h, wu_ref[:, cols]), 0.0)
        acc = acc + _dot((u * u).astype(jnp.bfloat16), wd_ref[cols, :])
    y_ref[...] = x1 + _rms(acc, g_ffn_ref[...])


def _out_mlp(o, x, w_out, g_post, g_pre, w_up, w_down, g_ffn):
    T, D = x.shape
    tm = PROJ_TM
    row = lambda width: pl.BlockSpec((tm, width), lambda t: (t, 0))
    return pl.pallas_call(
        _out_mlp_kernel,
        out_shape=jax.ShapeDtypeStruct((T, D), jnp.float32),
        grid=(T // tm,),
        in_specs=[row(o.shape[1]), row(D),
                  _const_spec(w_out.shape), _const_spec((1, D)), _const_spec((1, D)),
                  _const_spec(w_up.shape), _const_spec(w_down.shape), _const_spec((1, D))],
        out_specs=row(D),
        compiler_params=pltpu.CompilerParams(
            dimension_semantics=("arbitrary",), vmem_limit_bytes=VMEM_LIMIT),
        name="out_mlp",
    )(o, x, w_out.astype(jnp.bfloat16), g_post[None, :], g_pre[None, :],
      w_up.astype(jnp.bfloat16), w_down.astype(jnp.bfloat16), g_ffn[None, :])


def _swa_proj_kernel(x_ref, pos_ref, g_ref, wq_ref, wk_ref, wv_ref, freq_ref, lo_ref, hi_ref,
                     q_out, k_out, v_out):
    h = _rms(x_ref[...], g_ref[...]).astype(jnp.bfloat16)
    ang = pos_ref[...].astype(jnp.float32) * freq_ref[...]
    cos = jnp.cos(ang)
    sin = jnp.sin(ang)
    sin_lo = sin * lo_ref[...]
    sin_hi = sin * hi_ref[...]
    half = HEAD_DIM // 2

    def rope(t):
        return t * cos + pltpu.roll(t, LANES - half, 1) * sin_lo + pltpu.roll(t, half, 1) * sin_hi

    q = _dot(h, wq_ref[...])
    for c in range(q_out.shape[1] // LANES):
        cols = slice(c * LANES, (c + 1) * LANES)
        q_out[:, cols] = (rope(q[:, cols]) * (QK_SCALE * LOG2E)).astype(jnp.bfloat16)
    k = _dot(h, wk_ref[...])
    for c in range(k_out.shape[1] // LANES):
        cols = slice(c * LANES, (c + 1) * LANES)
        k_out[:, cols] = rope(k[:, cols]).astype(jnp.bfloat16)
    v_out[...] = _dot(h, wv_ref[...]).astype(jnp.bfloat16)


def _dup_heads(w, heads):
    d = w.shape[0]
    w = w.reshape(d, heads, 1, HEAD_DIM)
    return jnp.broadcast_to(w, (d, heads, 2, HEAD_DIM)).reshape(d, heads * 2 * HEAD_DIM)


def _swa_proj(x, pos, g, w_in):
    T, D = x.shape
    tm = PROJ_TM
    qd = SWA_Q_HEADS * HEAD_DIM
    kvd = SWA_KV_HEADS * HEAD_DIM
    kw = SWA_KV_HEADS * 2 * HEAD_DIM
    bf = lambda a: a.astype(jnp.bfloat16)
    wq = w_in[:, :qd]
    wk = w_in[:, qd:qd + kvd]
    wv = w_in[:, qd + kvd:]
    half = HEAD_DIM // 2
    inv_freq = 1.0 / (ROPE_THETA ** (jnp.arange(0, HEAD_DIM, 2, dtype=jnp.float32) / HEAD_DIM))
    freq = jnp.tile(inv_freq, LANES // half)[None, :]
    first_half = (np.arange(LANES) % HEAD_DIM) < half
    lo = np.where(first_half, -1.0, 0.0).astype(np.float32)[None, :]
    hi = np.where(first_half, 0.0, 1.0).astype(np.float32)[None, :]
    row = lambda width: pl.BlockSpec((tm, width), lambda t: (t, 0))
    return pl.pallas_call(
        _swa_proj_kernel,
        out_shape=(jax.ShapeDtypeStruct((T, qd), jnp.bfloat16),
                   jax.ShapeDtypeStruct((T, kw), jnp.bfloat16),
                   jax.ShapeDtypeStruct((T, kw), jnp.bfloat16)),
        grid=(T // tm,),
        in_specs=[row(D), row(1), _const_spec((1, D)),
                  _const_spec((D, qd)), _const_spec((D, kw)), _const_spec((D, kw)),
                  _const_spec((1, LANES)), _const_spec((1, LANES)), _const_spec((1, LANES))],
        out_specs=(row(qd), row(kw), row(kw)),
        compiler_params=pltpu.CompilerParams(
            dimension_semantics=("arbitrary",), vmem_limit_bytes=VMEM_LIMIT),
        name="swa_proj",
    )(x, pos, g[None, :], bf(wq), bf(_dup_heads(wk, SWA_KV_HEADS)), bf(_dup_heads(wv, SWA_KV_HEADS)),
      freq, jnp.asarray(lo), jnp.asarray(hi))


def _swa_attn_kernel(sink_ref, q_ref, kp_ref, kc_ref, vp_ref, vc_ref, o_ref, s_ref):
    n = pl.program_id(1)
    blk = WINDOW
    cols_per_group = SWA_GROUP // 2
    kb = jnp.concatenate([kp_ref[0], kc_ref[0]], axis=0).astype(jnp.float32)
    vbt = jnp.concatenate([vp_ref[0], vc_ref[0]], axis=0).T
    key = lax.broadcasted_iota(jnp.int32, (2 * blk, blk), 0)
    qry = lax.broadcasted_iota(jnp.int32, (2 * blk, blk), 1)
    diff = qry + blk - key
    first_key = jnp.where(n > 0, 0, blk)
    mask = (diff >= 0) & (diff < WINDOW) & (key >= first_key)
    klane = lax.broadcasted_iota(jnp.int32, (2 * blk, LANES), 1)
    top = lax.broadcasted_iota(jnp.int32, (LANES, blk), 0) < HEAD_DIM
    maxes = {}
    for g in range(SWA_KV_HEADS):
        gl = slice(g * LANES, (g + 1) * LANES)
        qt = jnp.concatenate([q_ref[0, :, c * LANES:(c + 1) * LANES].T
                              for c in range(g * cols_per_group, (g + 1) * cols_per_group)], axis=1)
        for par in range(2):
            keep = (klane < HEAD_DIM) if par == 0 else (klane >= HEAD_DIM)
            kpar = jnp.where(keep, kb[:, gl], 0.0).astype(jnp.bfloat16)
            st = _dot(kpar, qt)
            st = jnp.concatenate([jnp.where(mask, st[:, c * blk:(c + 1) * blk], NEG_BIG)
                                  for c in range(cols_per_group)], axis=1)
            s_ref[2 * g + par] = st
            maxes[g, par] = jnp.max(st, axis=0, keepdims=True)
    for g in range(SWA_KV_HEADS):
        gl = slice(g * LANES, (g + 1) * LANES)
        outs = []
        for par in range(2):
            sink = sink_ref[2 * g + par:2 * g + par + 1, :]
            m = jnp.maximum(maxes[g, par], sink)
            e = jnp.exp2(s_ref[2 * g + par] - m)
            denom = jnp.sum(e, axis=0, keepdims=True) + jnp.exp2(sink - m)
            ot = _dot(vbt[gl, :], e.astype(jnp.bfloat16))
            outs.append(ot * (1.0 / denom))
        for c in range(cols_per_group):
            both = jnp.where(top, outs[0][:, c * blk:(c + 1) * blk], outs[1][:, c * blk:(c + 1) * blk])
            col = g * cols_per_group + c
            o_ref[0, :, col * LANES:(col + 1) * LANES] = both.T.astype(jnp.bfloat16)


def _swa_attn(q, k, v, sinks):
    B, S, qd = q.shape
    kw = k.shape[2]
    blk = WINDOW
    cur = lambda b, n: (b, n, 0)
    prev = lambda b, n: (b, jnp.maximum(n - 1, 0), 0)
    cols_per_group = SWA_GROUP // 2
    sink_rows = (sinks * LOG2E).reshape(SWA_KV_HEADS, cols_per_group, 2).transpose(0, 2, 1)
    sink_rows = jnp.repeat(sink_rows.reshape(2 * SWA_KV_HEADS, cols_per_group), blk, axis=1)
    sink_rows = jnp.pad(sink_rows, ((0, 8 - 2 * SWA_KV_HEADS), (0, 0)))
    return pl.pallas_call(
        _swa_attn_kernel,
        out_shape=jax.ShapeDtypeStruct((B, S, qd), jnp.bfloat16),
        grid=(B, S // blk),
        in_specs=[_const_spec(sink_rows.shape),
                  pl.BlockSpec((1, blk, qd), cur),
                  pl.BlockSpec((1, blk, kw), prev), pl.BlockSpec((1, blk, kw), cur),
                  pl.BlockSpec((1, blk, kw), prev), pl.BlockSpec((1, blk, kw), cur)],
        out_specs=pl.BlockSpec((1, blk, qd), cur),
        scratch_shapes=[pltpu.VMEM((2 * SWA_KV_HEADS, 2 * blk, cols_per_group * blk), jnp.float32)],
        compiler_params=pltpu.CompilerParams(
            dimension_semantics=("arbitrary", "arbitrary"), vmem_limit_bytes=VMEM_LIMIT),
        name="swa_attn",
    )(sink_rows, q, k, k, v, v)


def kernel(x, positions, fox_w_in, fox_b_f, fox_w_out, swa_w_in, swa_sinks, swa_w_out,
           norm_pre_mix, norm_post_mix, norm_pre_ffn, norm_post_ffn, mlp_w_up, mlp_w_down):
    B, S, D = x.shape
    T = B * S
    q, k, v, stats = _fox_proj(x, norm_pre_mix[0], fox_w_in[0], fox_b_f[0])
    o = _fox_flash(q, k, v, stats)
    x1 = _out_mlp(o.reshape(T, -1), x.reshape(T, D), fox_w_out[0], norm_post_mix[0], norm_pre_ffn[0],
                  mlp_w_up[0], mlp_w_down[0], norm_post_ffn[0])
    q, k, v = _swa_proj(x1, positions.reshape(T, 1), norm_pre_mix[1], swa_w_in[0])
    o = _swa_attn(q.reshape(B, S, -1), k.reshape(B, S, -1), v.reshape(B, S, -1), swa_sinks[0])
    x2 = _out_mlp(o.reshape(T, -1), x1, swa_w_out[0], norm_post_mix[1], norm_pre_ffn[1],
                  mlp_w_up[1], mlp_w_down[1], norm_post_ffn[1])
    return x2.reshape(B, S, D)
```

```python
import functools
import math

import jax
import jax.numpy as jnp
import numpy as np
from jax import lax
from jax.experimental import pallas as pl
from jax.experimental.pallas import tpu as pltpu

D_MODEL = 1024
HEAD_DIM = 64
FOX_HEADS = 16
SWA_Q_HEADS = 16
SWA_KV_HEADS = 2
SWA_GROUP = SWA_Q_HEADS // SWA_KV_HEADS
WINDOW = 128
D_FF = 4 * D_MODEL
ROPE_THETA = 10000.0
NORM_EPS = 1e-6

LANES = 128
HEAD_AUG = LANES
LOG2E = 1.4426950408889634
QK_SCALE = HEAD_DIM ** -0.5
NEG_BIG = -1e30
SKIP_LOG2 = 152.0
NORM_SLACK = 1.02
DIAG_SLACK = 0.02
SHIFT_GAP_LOG2 = 100.0
VMEM_LIMIT = 56 * 1024 * 1024

DEC0 = HEAD_DIM
N_SPLIT = 3
ONES_LANE = N_SPLIT * FOX_HEADS

PROJ_TM = 512
FLASH_BQ = 512
MLP_CHUNK = 1024


def _rms(x, g):
    return x * lax.rsqrt(jnp.mean(x * x, axis=-1, keepdims=True) + NORM_EPS) * g


def _dot(a, b):
    return jnp.dot(a, b, preferred_element_type=jnp.float32)


def _dot_nt(a, b):
    return lax.dot_general(a, b, (((1,), (1,)), ((), ())), preferred_element_type=jnp.float32)


def _split3(x):
    hi = x.astype(jnp.bfloat16)
    r = x - hi.astype(jnp.float32)
    mid = r.astype(jnp.bfloat16)
    lo = (r - mid.astype(jnp.float32)).astype(jnp.bfloat16)
    return hi, mid, lo


def _const_spec(shape):
    nd = len(shape)
    return pl.BlockSpec(shape, lambda *_: (0,) * nd, pipeline_mode=pl.Buffered(1))


def _fox_proj_kernel(x_ref, g_ref, wq_ref, wk_ref, wv_ref, wf_ref, b3_ref, place_ref,
                     tri_ref, sel_ref, q_out, k_out, v_out, stats_out, carry_ref, *, tm):
    @pl.when(pl.program_id(1) == 0)
    def _():
        carry_ref[...] = jnp.zeros_like(carry_ref)

    h = _rms(x_ref[0], g_ref[...]).astype(jnp.bfloat16)

    z = _dot(h, wf_ref[...]) + b3_ref[...]
    logf = jnp.minimum(z, 0.0) - jnp.log1p(jnp.exp(-jnp.abs(z)))

    tri = tri_ref[...]
    hi, mid, lo = _split3(logf)
    c = _dot(tri, hi) + _dot(tri, mid) + _dot(tri, lo) + carry_ref[...]
    carry_ref[...] = c[tm - 1:tm, :]

    chi, cmid, clo = (p.astype(jnp.float32) for p in _split3(c * LOG2E))
    lane = lax.broadcasted_iota(jnp.int32, (tm, LANES), 1)
    parts = jnp.where(lane < FOX_HEADS, chi,
                      jnp.where(lane < 2 * FOX_HEADS, cmid,
                                jnp.where(lane < ONES_LANE, clo,
                                          jnp.where(lane == ONES_LANE, 1.0, 0.0))))
    parts = parts.astype(jnp.bfloat16)

    width = FOX_HEADS * HEAD_AUG
    q = _dot(h, wq_ref[...]) * (QK_SCALE * LOG2E)
    k = _dot(h, wk_ref[...])
    v = _dot(h, wv_ref[...])
    qn2 = _dot((q * q).astype(jnp.bfloat16), sel_ref[...])
    kn2 = _dot((k * k).astype(jnp.bfloat16), sel_ref[...])

    low = lane < HEAD_DIM
    v_tail = jnp.where(lane == HEAD_DIM, 1.0, 0.0)
    for pr in range(FOX_HEADS // 2):
        pcols = slice(pr * LANES, (pr + 1) * LANES)
        acols = slice(2 * pr * HEAD_AUG, (2 * pr + 2) * HEAD_AUG)
        q_tail = _dot(parts, place_ref[:, acols])
        k_tail = _dot(parts, place_ref[:, width + 2 * pr * HEAD_AUG:width + (2 * pr + 2) * HEAD_AUG])
        for src, tail, out in ((q[:, pcols], q_tail, q_out), (k[:, pcols], k_tail, k_out),
                               (v[:, pcols], None, v_out)):
            for par, head in enumerate((src, pltpu.roll(src, HEAD_DIM, 1))):
                fill = v_tail if tail is None else tail[:, par * HEAD_AUG:(par + 1) * HEAD_AUG]
                hcols = slice((2 * pr + par) * HEAD_AUG, (2 * pr + par + 1) * HEAD_AUG)
                out[0, :, hcols] = jnp.where(low, head, fill).astype(jnp.bfloat16)

    c2 = c * LOG2E
    stats_out[0, 0] = jnp.zeros((8, LANES), jnp.float32)
    stats_out[0, 0, 0:1, :] = jnp.sqrt(jnp.max(qn2, axis=0, keepdims=True))
    stats_out[0, 0, 1:2, :] = jnp.sqrt(jnp.max(kn2, axis=0, keepdims=True))
    stats_out[0, 0, 2:3, :] = c2[0:1, :]
    stats_out[0, 0, 3:4, :] = c2[tm - 1:tm, :]
    diag = _dot((q * k).astype(jnp.bfloat16), sel_ref[...])
    stats_out[0, 0, 4:5, :] = jnp.min(diag, axis=0, keepdims=True)


def _fox_placement():
    width = FOX_HEADS * HEAD_AUG
    p = np.zeros((LANES, 2 * width), np.float32)
    for h in range(FOX_HEADS):
        base = h * HEAD_AUG + DEC0
        for s in range(N_SPLIT):
            p[s * FOX_HEADS + h, base + s] = 1.0
            p[ONES_LANE, base + N_SPLIT + s] = 1.0
            p[ONES_LANE, width + base + s] = 1.0
            p[s * FOX_HEADS + h, width + base + N_SPLIT + s] = -1.0
    return p


def _fox_proj(x, g, w_in, b_f):
    B, S, D = x.shape
    tm = PROJ_TM
    hd_all = FOX_HEADS * HEAD_DIM
    width = FOX_HEADS * HEAD_AUG
    wq = w_in[:, :hd_all].astype(jnp.bfloat16)
    wk = w_in[:, hd_all:2 * hd_all].astype(jnp.bfloat16)
    wv = w_in[:, 2 * hd_all:3 * hd_all].astype(jnp.bfloat16)
    wf = w_in[:, 3 * hd_all:]
    wf3 = jnp.concatenate([wf] * N_SPLIT + [jnp.zeros((D, LANES - ONES_LANE), wf.dtype)], axis=1)
    b3 = jnp.concatenate([b_f] * N_SPLIT + [jnp.zeros((LANES - ONES_LANE,), b_f.dtype)])[None, :]
    place = jnp.asarray(_fox_placement(), jnp.bfloat16)
    tri = jnp.asarray(np.tril(np.ones((tm, tm), np.float32)), jnp.bfloat16)
    sel = np.zeros((hd_all, LANES), np.float32)
    for hd in range(FOX_HEADS):
        sel[hd * HEAD_DIM:(hd + 1) * HEAD_DIM, hd] = 1.0

    out_sds = jax.ShapeDtypeStruct((B, S, width), jnp.bfloat16)
    out_spec = pl.BlockSpec((1, tm, width), lambda b, t: (b, t, 0))
    stats_sds = jax.ShapeDtypeStruct((B, S // tm, 8, LANES), jnp.float32)
    stats_spec = pl.BlockSpec((1, 1, 8, LANES), lambda b, t: (b, t, 0, 0))
    return pl.pallas_call(
        functools.partial(_fox_proj_kernel, tm=tm),
        out_shape=(out_sds, out_sds, out_sds, stats_sds),
        grid=(B, S // tm),
        in_specs=[
            pl.BlockSpec((1, tm, D), lambda b, t: (b, t, 0)),
            _const_spec((1, D)),
            _const_spec((D, hd_all)), _const_spec((D, hd_all)), _const_spec((D, hd_all)),
            _const_spec((D, LANES)), _const_spec((1, LANES)),
            _const_spec((LANES, 2 * width)),
            _const_spec((tm, tm)), _const_spec((hd_all, LANES)),
        ],
        out_specs=(out_spec, out_spec, out_spec, stats_spec),
        scratch_shapes=[pltpu.VMEM((1, LANES), jnp.float32)],
        compiler_params=pltpu.CompilerParams(
            dimension_semantics=("arbitrary", "arbitrary"), vmem_limit_bytes=VMEM_LIMIT),
        name="fox_proj",
    )(x, g[None, :], wq, wk, wv, wf3.astype(jnp.bfloat16), b3, place, tri,
      jnp.asarray(sel, jnp.bfloat16))


def _flash_transpose_v(v_ref, vt_ref, qi, bq, seq):
    @pl.when(qi == 0)
    def _():
        for c in range(seq // bq):
            rows = slice(c * bq, (c + 1) * bq)
            vt_ref[:, rows] = v_ref[0, rows, :].T


def _flash_finalize(acc_ref, o_ref, bq):
    halves = []
    for hh in range(2):
        acc = acc_ref[hh]
        halves.append(acc[0:HEAD_DIM, :] * (1.0 / acc[HEAD_DIM:HEAD_DIM + 1, :]))
    o_ref[0] = jnp.concatenate(halves, axis=0).T.astype(jnp.bfloat16)


def _fox_flash_kernel(first_ref, q_ref, k_ref, v_ref, o_ref, vt_ref, qt_ref, m_ref, acc_ref, s_ref, p_ref,
                      mb_ref, *, bq, seq):
    qi = pl.program_id(2)
    heads = 2
    nq = seq // bq
    head0 = (pl.program_id(0) * FOX_HEADS + heads * pl.program_id(1)) * nq + qi
    j0 = jnp.minimum(first_ref[head0], first_ref[head0 + nq])
    _flash_transpose_v(v_ref, vt_ref, qi, bq, seq)
    qt_ref[...] = q_ref[0].T
    m_ref[...] = jnp.full_like(m_ref, NEG_BIG)
    acc_ref[...] = jnp.zeros_like(acc_ref)

    def scores(j, slot, masked):
        start = pl.multiple_of(j * bq, bq)
        for hh in range(heads):
            cols = slice(hh * HEAD_AUG, (hh + 1) * HEAD_AUG)
            st = _dot(k_ref[0, pl.ds(start, bq), cols], qt_ref[cols, :])
            if masked:
                key = lax.broadcasted_iota(jnp.int32, (bq, bq), 0)
                qry = lax.broadcasted_iota(jnp.int32, (bq, bq), 1)
                st = jnp.where(key - qry <= (qi - j) * bq, st, NEG_BIG)
            s_ref[2 * slot + hh] = st
            mb_ref[2 * slot + hh] = jnp.max(st, axis=0, keepdims=True)

    def consume(j, slot):
        start = pl.multiple_of(j * bq, bq)
        alpha = []
        for hh in range(heads):
            m_prev = m_ref[hh]
            m_new = jnp.maximum(m_prev, mb_ref[2 * slot + hh])
            m_ref[hh] = m_new
            p_ref[hh] = jnp.exp2(s_ref[2 * slot + hh] - m_new).astype(jnp.bfloat16)
            alpha.append(jnp.exp2(m_prev - m_new))
        for hh in range(heads):
            cols = slice(hh * HEAD_AUG, (hh + 1) * HEAD_AUG)
            vt = vt_ref[cols, pl.ds(start, bq)]
            acc_ref[hh] = alpha[hh] * acc_ref[hh] + _dot(vt, p_ref[hh])

    def half_step(j, slot, masked):
        scores(j + 1, 1 - slot, masked)
        consume(j, slot)

    span = qi - j0
    pairs = jnp.maximum(span - 1, 0) // 2
    rest = span - 2 * pairs
    scores(j0, 0, True)

    def body(t, carry):
        half_step(j0 + 2 * t, 0, False)
        half_step(j0 + 2 * t + 1, 1, False)
        return carry

    lax.fori_loop(0, pairs, body, 0)

    @pl.when(rest >= 1)
    def _():
        half_step(j0 + 2 * pairs, 0, True)

    @pl.when(rest == 2)
    def _():
        half_step(j0 + 2 * pairs + 1, 1, True)

    @pl.when(rest == 1)
    def _():
        consume(qi, 1)

    @pl.when(rest != 1)
    def _():
        consume(qi, 0)

    _flash_finalize(acc_ref, o_ref, bq)


def _fox_flash_fixed_kernel(first_ref, kall_ref, q_ref, k_ref, v_ref, o_ref, vt_ref, qt_ref, r_ref, acc_ref,
                            p_ref, *, bq, seq):
    b, hp, qi = pl.program_id(0), pl.program_id(1), pl.program_id(2)
    heads = 2
    nq = seq // bq
    head0 = (b * FOX_HEADS + heads * hp) * nq + qi
    first = (first_ref[head0], first_ref[head0 + nq])
    j0 = jnp.maximum(first[0], first[1])
    _flash_transpose_v(v_ref, vt_ref, qi, bq, seq)
    qt_ref[...] = q_ref[0].T
    acc_ref[...] = jnp.zeros_like(acc_ref)
    for hh in range(heads):
        qf = qt_ref[hh * HEAD_AUG:hh * HEAD_AUG + HEAD_DIM, :].astype(jnp.float32)
        qnorm = jnp.sqrt(jnp.sum(qf * qf, axis=0, keepdims=True))
        r_ref[hh] = qnorm * (NORM_SLACK * kall_ref[b * FOX_HEADS + heads * hp + hh])

    def blocks(j, nblk, masked, which=(0, 1)):
        start = pl.multiple_of(j * bq, bq)
        for hh in which:
            cols = slice(hh * HEAD_AUG, (hh + 1) * HEAD_AUG)
            for u in range(nblk):
                st = _dot(k_ref[0, pl.ds(start + u * bq, bq), cols], qt_ref[cols, :])
                if masked and u == nblk - 1:
                    key = lax.broadcasted_iota(jnp.int32, (bq, bq), 0)
                    qry = lax.broadcasted_iota(jnp.int32, (bq, bq), 1)
                    st = jnp.where(key <= qry, st, NEG_BIG)
                p_ref[hh, u * bq:(u + 1) * bq, :] = jnp.exp2(st - r_ref[hh]).astype(jnp.bfloat16)
        for hh in which:
            cols = slice(hh * HEAD_AUG, (hh + 1) * HEAD_AUG)
            vt = vt_ref[cols, pl.ds(start, nblk * bq)]
            acc_ref[hh] += _dot(vt, p_ref[hh, 0:nblk * bq, :])

    for hh in range(heads):
        @pl.when(first[hh] < j0)
        def _(hh=hh):
            lo = first[hh]
            count = j0 - lo

            def body1(t, carry):
                blocks(lo + 4 * t, 4, False, (hh,))
                return carry

            lax.fori_loop(0, count // 4, body1, 0)

            @pl.when(count % 4 >= 2)
            def _():
                blocks(lo + 4 * (count // 4), 2, False, (hh,))

            @pl.when(count % 2 == 1)
            def _():
                blocks(j0 - 1, 1, False, (hh,))

    span = qi - j0
    quads = span // 4
    rest = span - 4 * quads

    def body(t, carry):
        blocks(j0 + 4 * t, 4, False)
        return carry

    lax.fori_loop(0, quads, body, 0)

    @pl.when(rest >= 2)
    def _():
        blocks(j0 + 4 * quads, 2, False)

    @pl.when(rest % 2 == 1)
    def _():
        blocks(qi - 1, 2, True)

    @pl.when(rest % 2 == 0)
    def _():
        blocks(qi, 1, True)

    _flash_finalize(acc_ref, o_ref, bq)


def _first_key_block(stats):
    qmax, kmax = stats[:, :, 0, :FOX_HEADS], stats[:, :, 1, :FOX_HEADS]
    c_first, c_last = stats[:, :, 2, :FOX_HEADS], stats[:, :, 3, :FOX_HEADS]
    diag_min = stats[:, :, 4, :FOX_HEADS]
    kall = jnp.max(kmax, axis=1, keepdims=True)
    thr = c_first + SKIP_LOG2 + (NORM_SLACK + DIAG_SLACK) * qmax * kall - diag_min
    skippable = c_last[:, None, :, :] > thr[:, :, None, :]
    first = jnp.sum(skippable.astype(jnp.int32), axis=2)
    nq = stats.shape[1]
    first = jnp.minimum(first, jnp.arange(nq, dtype=jnp.int32)[None, :, None])
    return first.transpose(0, 2, 1).reshape(-1)


def _fox_flash(q, k, v, stats):
    B, S, width = q.shape
    bq = FLASH_BQ
    pair = 2 * HEAD_AUG
    first = _first_key_block(stats)
    qmax, kmax, diag_min = (stats[:, :, r, :FOX_HEADS] for r in (0, 1, 4))
    kall = jnp.max(kmax, axis=1)
    gap = NORM_SLACK * qmax * kall[:, None, :] - diag_min
    fixed_shift_ok = jnp.max(gap) < SHIFT_GAP_LOG2

    blk = lambda shape, index, **kw: pl.BlockSpec(shape, lambda b, hp, i, first: index(b, hp, i), **kw)
    tensor_specs = [
        blk((1, bq, pair), lambda b, hp, i: (b, i, hp)),
        blk((1, S, pair), lambda b, hp, i: (b, 0, hp)),
        blk((1, S, pair), lambda b, hp, i: (b, 0, hp), pipeline_mode=pl.Buffered(1)),
    ]
    common = dict(
        out_shape=jax.ShapeDtypeStruct((B, S, FOX_HEADS * HEAD_DIM), jnp.bfloat16),
        compiler_params=pltpu.CompilerParams(
            dimension_semantics=("arbitrary", "arbitrary", "arbitrary"),
            vmem_limit_bytes=VMEM_LIMIT))
    grid = (B, FOX_HEADS // 2, S // bq)
    out_spec = blk((1, bq, 2 * HEAD_DIM), lambda b, hp, i: (b, i, hp))
    vt_qt = [pltpu.VMEM((pair, S), jnp.bfloat16), pltpu.VMEM((pair, bq), jnp.bfloat16)]

    def fixed_shift():
        return pl.pallas_call(
            functools.partial(_fox_flash_fixed_kernel, bq=bq, seq=S),
            grid_spec=pltpu.PrefetchScalarGridSpec(
                num_scalar_prefetch=1, grid=grid,
                in_specs=[pl.BlockSpec(memory_space=pltpu.SMEM)] + tensor_specs,
                out_specs=out_spec,
                scratch_shapes=vt_qt + [pltpu.VMEM((2, 1, bq), jnp.float32),
                                        pltpu.VMEM((2, HEAD_AUG, bq), jnp.float32),
                                        pltpu.VMEM((2, 4 * bq, bq), jnp.bfloat16)]),
            name="fox_flash_fixed", **common)(first, kall.reshape(-1), q, k, v)

    def running_max():
        return pl.pallas_call(
            functools.partial(_fox_flash_kernel, bq=bq, seq=S),
            grid_spec=pltpu.PrefetchScalarGridSpec(
                num_scalar_prefetch=1, grid=grid, in_specs=tensor_specs, out_specs=out_spec,
                scratch_shapes=vt_qt + [pltpu.VMEM((2, 1, bq), jnp.float32),
                                        pltpu.VMEM((2, HEAD_AUG, bq), jnp.float32),
                                        pltpu.VMEM((4, bq, bq), jnp.float32),
                                        pltpu.VMEM((2, bq, bq), jnp.bfloat16),
                                        pltpu.VMEM((4, 1, bq), jnp.float32)]),
            name="fox_flash", **common)(first, q, k, v)

    return lax.cond(fixed_shift_ok, fixed_shift, running_max)


def _out_mlp_kernel(o_ref, x_ref, wo_ref, g_post_ref, g_pre_ref, wu_ref, wd_ref, g_ffn_ref, y_ref):
    a = _dot(o_ref[...], wo_ref[...])
    x1 = x_ref[...] + _rms(a, g_post_ref[...])
    h = _rms(x1, g_pre_ref[...]).astype(jnp.bfloat16)
    acc = jnp.zeros_like(x1)
    for c in range(D_FF // MLP_CHUNK):
        cols = slice(c * MLP_CHUNK, (c + 1) * MLP_CHUNK)
        u = jnp.maximum(_dot(h, wu_ref[:, cols]), 0.0)
        acc = acc + _dot((u * u).astype(jnp.bfloat16), wd_ref[cols, :])
    y_ref[...] = x1 + _rms(acc, g_ffn_ref[...])


def _out_mlp(o, x, w_out, g_post, g_pre, w_up, w_down, g_ffn):
    T, D = x.shape
    tm = PROJ_TM
    row = lambda width: pl.BlockSpec((tm, width), lambda t: (t, 0))
    return pl.pallas_call(
        _out_mlp_kernel,
        out_shape=jax.ShapeDtypeStruct((T, D), jnp.float32),
        grid=(T // tm,),
        in_specs=[row(o.shape[1]), row(D),
                  _const_spec(w_out.shape), _const_spec((1, D)), _const_spec((1, D)),
                  _const_spec(w_up.shape), _const_spec(w_down.shape), _const_spec((1, D))],
        out_specs=row(D),
        compiler_params=pltpu.CompilerParams(
            dimension_semantics=("arbitrary",), vmem_limit_bytes=VMEM_LIMIT),
        name="out_mlp",
    )(o, x, w_out.astype(jnp.bfloat16), g_post[None, :], g_pre[None, :],
      w_up.astype(jnp.bfloat16), w_down.astype(jnp.bfloat16), g_ffn[None, :])


def _swa_proj_kernel(x_ref, pos_ref, g_ref, wq_ref, wk_ref, wv_ref, freq_ref, lo_ref, hi_ref,
                     q_out, k_out, v_out):
    h = _rms(x_ref[...], g_ref[...]).astype(jnp.bfloat16)
    ang = pos_ref[...].astype(jnp.float32) * freq_ref[...]
    cos = jnp.cos(ang)
    sin = jnp.sin(ang)
    sin_lo = sin * lo_ref[...]
    sin_hi = sin * hi_ref[...]
    half = HEAD_DIM // 2

    def rope(t):
        return t * cos + pltpu.roll(t, LANES - half, 1) * sin_lo + pltpu.roll(t, half, 1) * sin_hi

    q = _dot(h, wq_ref[...])
    for c in range(q_out.shape[1] // LANES):
        cols = slice(c * LANES, (c + 1) * LANES)
        q_out[:, cols] = (rope(q[:, cols]) * (QK_SCALE * LOG2E)).astype(jnp.bfloat16)
    k = _dot(h, wk_ref[...])
    for c in range(k_out.shape[1] // LANES):
        cols = slice(c * LANES, (c + 1) * LANES)
        k_out[:, cols] = rope(k[:, cols]).astype(jnp.bfloat16)
    v_out[...] = _dot(h, wv_ref[...]).astype(jnp.bfloat16)


def _dup_heads(w, heads):
    d = w.shape[0]
    w = w.reshape(d, heads, 1, HEAD_DIM)
    return jnp.broadcast_to(w, (d, heads, 2, HEAD_DIM)).reshape(d, heads * 2 * HEAD_DIM)


def _swa_proj(x, pos, g, w_in):
    T, D = x.shape
    tm = PROJ_TM
    qd = SWA_Q_HEADS * HEAD_DIM
    kvd = SWA_KV_HEADS * HEAD_DIM
    kw = SWA_KV_HEADS * 2 * HEAD_DIM
    bf = lambda a: a.astype(jnp.bfloat16)
    wq = w_in[:, :qd]
    wk = w_in[:, qd:qd + kvd]
    wv = w_in[:, qd + kvd:]
    half = HEAD_DIM // 2
    inv_freq = 1.0 / (ROPE_THETA ** (jnp.arange(0, HEAD_DIM, 2, dtype=jnp.float32) / HEAD_DIM))
    freq = jnp.tile(inv_freq, LANES // half)[None, :]
    first_half = (np.arange(LANES) % HEAD_DIM) < half
    lo = np.where(first_half, -1.0, 0.0).astype(np.float32)[None, :]
    hi = np.where(first_half, 0.0, 1.0).astype(np.float32)[None, :]
    row = lambda width: pl.BlockSpec((tm, width), lambda t: (t, 0))
    return pl.pallas_call(
        _swa_proj_kernel,
        out_shape=(jax.ShapeDtypeStruct((T, qd), jnp.bfloat16),
                   jax.ShapeDtypeStruct((T, kw), jnp.bfloat16),
                   jax.ShapeDtypeStruct((T, kw), jnp.bfloat16)),
        grid=(T // tm,),
        in_specs=[row(D), row(1), _const_spec((1, D)),
                  _const_spec((D, qd)), _const_spec((D, kw)), _const_spec((D, kw)),
                  _const_spec((1, LANES)), _const_spec((1, LANES)), _const_spec((1, LANES))],
        out_specs=(row(qd), row(kw), row(kw)),
        compiler_params=pltpu.CompilerParams(
            dimension_semantics=("arbitrary",), vmem_limit_bytes=VMEM_LIMIT),
        name="swa_proj",
    )(x, pos, g[None, :], bf(wq), bf(_dup_heads(wk, SWA_KV_HEADS)), bf(_dup_heads(wv, SWA_KV_HEADS)),
      freq, jnp.asarray(lo), jnp.asarray(hi))


def _swa_attn_kernel(sink_ref, q_ref, kp_ref, kc_ref, vp_ref, vc_ref, o_ref, s_ref):
    n = pl.program_id(1)
    blk = WINDOW
    cols_per_group = SWA_GROUP // 2
    kb = jnp.concatenate([kp_ref[0], kc_ref[0]], axis=0).astype(jnp.float32)
    vbt = jnp.concatenate([vp_ref[0], vc_ref[0]], axis=0).T
    key = lax.broadcasted_iota(jnp.int32, (2 * blk, blk), 0)
    qry = lax.broadcasted_iota(jnp.int32, (2 * blk, blk), 1)
    diff = qry + blk - key
    first_key = jnp.where(n > 0, 0, blk)
    mask = (diff >= 0) & (diff < WINDOW) & (key >= first_key)
    klane = lax.broadcasted_iota(jnp.int32, (2 * blk, LANES), 1)
    top = lax.broadcasted_iota(jnp.int32, (LANES, blk), 0) < HEAD_DIM
    maxes = {}
    for g in range(SWA_KV_HEADS):
        gl = slice(g * LANES, (g + 1) * LANES)
        qt = jnp.concatenate([q_ref[0, :, c * LANES:(c + 1) * LANES].T
                              for c in range(g * cols_per_group, (g + 1) * cols_per_group)], axis=1)
        for par in range(2):
            keep = (klane < HEAD_DIM) if par == 0 else (klane >= HEAD_DIM)
            kpar = jnp.where(keep, kb[:, gl], 0.0).astype(jnp.bfloat16)
            st = _dot(kpar, qt)
            st = jnp.concatenate([jnp.where(mask, st[:, c * blk:(c + 1) * blk], NEG_BIG)
                                  for c in range(cols_per_group)], axis=1)
            s_ref[2 * g + par] = st
            maxes[g, par] = jnp.max(st, axis=0, keepdims=True)
    for g in range(SWA_KV_HEADS):
        gl = slice(g * LANES, (g + 1) * LANES)
        outs = []
        for par in range(2):
            sink = sink_ref[2 * g + par:2 * g + par + 1, :]
            m = jnp.maximum(maxes[g, par], sink)
            e = jnp.exp2(s_ref[2 * g + par] - m)
            denom = jnp.sum(e, axis=0, keepdims=True) + jnp.exp2(sink - m)
            ot = _dot(vbt[gl, :], e.astype(jnp.bfloat16))
            outs.append(ot * (1.0 / denom))
        for c in range(cols_per_group):
            both = jnp.where(top, outs[0][:, c * blk:(c + 1) * blk], outs[1][:, c * blk:(c + 1) * blk])
            col = g * cols_per_group + c
            o_ref[0, :, col * LANES:(col + 1) * LANES] = both.T.astype(jnp.bfloat16)


def _swa_attn(q, k, v, sinks):
    B, S, qd = q.shape
    kw = k.shape[2]
    blk = WINDOW
    cur = lambda b, n: (b, n, 0)
    prev = lambda b, n: (b, jnp.maximum(n - 1, 0), 0)
    cols_per_group = SWA_GROUP // 2
    sink_rows = (sinks * LOG2E).reshape(SWA_KV_HEADS, cols_per_group, 2).transpose(0, 2, 1)
    sink_rows = jnp.repeat(sink_rows.reshape(2 * SWA_KV_HEADS, cols_per_group), blk, axis=1)
    sink_rows = jnp.pad(sink_rows, ((0, 8 - 2 * SWA_KV_HEADS), (0, 0)))
    return pl.pallas_call(
        _swa_attn_kernel,
        out_shape=jax.ShapeDtypeStruct((B, S, qd), jnp.bfloat16),
        grid=(B, S // blk),
        in_specs=[_const_spec(sink_rows.shape),
                  pl.BlockSpec((1, blk, qd), cur),
                  pl.BlockSpec((1, blk, kw), prev), pl.BlockSpec((1, blk, kw), cur),
                  pl.BlockSpec((1, blk, kw), prev), pl.BlockSpec((1, blk, kw), cur)],
        out_specs=pl.BlockSpec((1, blk, qd), cur),
        scratch_shapes=[pltpu.VMEM((2 * SWA_KV_HEADS, 2 * blk, cols_per_group * blk), jnp.float32)],
        compiler_params=pltpu.CompilerParams(
            dimension_semantics=("arbitrary", "arbitrary"), vmem_limit_bytes=VMEM_LIMIT),
        name="swa_attn",
    )(sink_rows, q, k, k, v, v)


def kernel(x, positions, fox_w_in, fox_b_f, fox_w_out, swa_w_in, swa_sinks, swa_w_out,
           norm_pre_mix, norm_post_mix, norm_pre_ffn, norm_post_ffn, mlp_w_up, mlp_w_down):
    B, S, D = x.shape
    T = B * S
    q, k, v, stats = _fox_proj(x, norm_pre_mix[0], fox_w_in[0], fox_b_f[0])
    o = _fox_flash(q, k, v, stats)
    x1 = _out_mlp(o.reshape(T, -1), x.reshape(T, D), fox_w_out[0], norm_post_mix[0], norm_pre_ffn[0],
                  mlp_w_up[0], mlp_w_down[0], norm_post_ffn[0])
    q, k, v = _swa_proj(x1, positions.reshape(T, 1), norm_pre_mix[1], swa_w_in[0])
    o = _swa_attn(q.reshape(B, S, -1), k.reshape(B, S, -1), v.reshape(B, S, -1), swa_sinks[0])
    x2 = _out_mlp(o.reshape(T, -1), x1, swa_w_out[0], norm_post_mix[1], norm_pre_ffn[1],
                  mlp_w_up[1], mlp_w_down[1], norm_post_ffn[1])
    return x2.reshape(B, S, D)
```

```python
import functools
import math

import jax
import jax.numpy as jnp
import numpy as np
from jax import lax
from jax.experimental import pallas as pl
from jax.experimental.pallas import tpu as pltpu

D_MODEL = 1024
HEAD_DIM = 64
FOX_HEADS = 16
SWA_Q_HEADS = 16
SWA_KV_HEADS = 2
SWA_GROUP = SWA_Q_HEADS // SWA_KV_HEADS
WINDOW = 128
D_FF = 4 * D_MODEL
ROPE_THETA = 10000.0
NORM_EPS = 1e-6

LANES = 128
HEAD_AUG = LANES
LOG2E = 1.4426950408889634
QK_SCALE = HEAD_DIM ** -0.5
NEG_BIG = -1e30
SKIP_LOG2 = 152.0
NORM_SLACK = 1.02
DIAG_SLACK = 0.02
SHIFT_GAP_LOG2 = 100.0
VMEM_LIMIT = 56 * 1024 * 1024

DEC0 = HEAD_DIM
N_SPLIT = 3
ONES_LANE = N_SPLIT * FOX_HEADS

PROJ_TM = 512
FLASH_BQ = 512
MLP_CHUNK = 1024
SWA_SUB = 4


def _rms(x, g):
    return x * lax.rsqrt(jnp.mean(x * x, axis=-1, keepdims=True) + NORM_EPS) * g


def _dot(a, b):
    return jnp.dot(a, b, preferred_element_type=jnp.float32)


def _dot_nt(a, b):
    return lax.dot_general(a, b, (((1,), (1,)), ((), ())), preferred_element_type=jnp.float32)


def _split3(x):
    hi = x.astype(jnp.bfloat16)
    r = x - hi.astype(jnp.float32)
    mid = r.astype(jnp.bfloat16)
    lo = (r - mid.astype(jnp.float32)).astype(jnp.bfloat16)
    return hi, mid, lo


def _const_spec(shape):
    nd = len(shape)
    return pl.BlockSpec(shape, lambda *_: (0,) * nd, pipeline_mode=pl.Buffered(1))


def _fox_proj_kernel(x_ref, g_ref, wq_ref, wk_ref, wv_ref, wf_ref, b3_ref, place_ref,
                     tri_ref, sel_ref, q_out, k_out, v_out, stats_out, carry_ref, *, tm):
    @pl.when(pl.program_id(1) == 0)
    def _():
        carry_ref[...] = jnp.zeros_like(carry_ref)

    h = _rms(x_ref[0], g_ref[...]).astype(jnp.bfloat16)

    z = _dot(h, wf_ref[...]) + b3_ref[...]
    logf = jnp.minimum(z, 0.0) - jnp.log1p(jnp.exp(-jnp.abs(z)))

    tri = tri_ref[...]
    hi, mid, lo = _split3(logf)
    c = _dot(tri, hi) + _dot(tri, mid) + _dot(tri, lo) + carry_ref[...]
    carry_ref[...] = c[tm - 1:tm, :]

    chi, cmid, clo = (p.astype(jnp.float32) for p in _split3(c * LOG2E))
    lane = lax.broadcasted_iota(jnp.int32, (tm, LANES), 1)
    parts = jnp.where(lane < FOX_HEADS, chi,
                      jnp.where(lane < 2 * FOX_HEADS, cmid,
                                jnp.where(lane < ONES_LANE, clo,
                                          jnp.where(lane == ONES_LANE, 1.0, 0.0))))
    parts = parts.astype(jnp.bfloat16)

    width = FOX_HEADS * HEAD_AUG
    q = _dot(h, wq_ref[...]) * (QK_SCALE * LOG2E)
    k = _dot(h, wk_ref[...])
    v = _dot(h, wv_ref[...])
    qn2 = _dot((q * q).astype(jnp.bfloat16), sel_ref[...])
    kn2 = _dot((k * k).astype(jnp.bfloat16), sel_ref[...])

    low = lane < HEAD_DIM
    v_tail = jnp.where(lane == HEAD_DIM, 1.0, 0.0)
    for pr in range(FOX_HEADS // 2):
        pcols = slice(pr * LANES, (pr + 1) * LANES)
        acols = slice(2 * pr * HEAD_AUG, (2 * pr + 2) * HEAD_AUG)
        q_tail = _dot(parts, place_ref[:, acols])
        k_tail = _dot(parts, place_ref[:, width + 2 * pr * HEAD_AUG:width + (2 * pr + 2) * HEAD_AUG])
        for src, tail, out in ((q[:, pcols], q_tail, q_out), (k[:, pcols], k_tail, k_out),
                               (v[:, pcols], None, v_out)):
            for par, head in enumerate((src, pltpu.roll(src, HEAD_DIM, 1))):
                fill = v_tail if tail is None else tail[:, par * HEAD_AUG:(par + 1) * HEAD_AUG]
                hcols = slice((2 * pr + par) * HEAD_AUG, (2 * pr + par + 1) * HEAD_AUG)
                out[0, :, hcols] = jnp.where(low, head, fill).astype(jnp.bfloat16)

    c2 = c * LOG2E
    stats_out[0, 0] = jnp.zeros((8, LANES), jnp.float32)
    stats_out[0, 0, 0:1, :] = jnp.sqrt(jnp.max(qn2, axis=0, keepdims=True))
    stats_out[0, 0, 1:2, :] = jnp.sqrt(jnp.max(kn2, axis=0, keepdims=True))
    stats_out[0, 0, 2:3, :] = c2[0:1, :]
    stats_out[0, 0, 3:4, :] = c2[tm - 1:tm, :]
    diag = _dot((q * k).astype(jnp.bfloat16), sel_ref[...])
    stats_out[0, 0, 4:5, :] = jnp.min(diag, axis=0, keepdims=True)


def _fox_placement():
    width = FOX_HEADS * HEAD_AUG
    p = np.zeros((LANES, 2 * width), np.float32)
    for h in range(FOX_HEADS):
        base = h * HEAD_AUG + DEC0
        for s in range(N_SPLIT):
            p[s * FOX_HEADS + h, base + s] = 1.0
            p[ONES_LANE, base + N_SPLIT + s] = 1.0
            p[ONES_LANE, width + base + s] = 1.0
            p[s * FOX_HEADS + h, width + base + N_SPLIT + s] = -1.0
    return p


def _fox_proj(x, g, w_in, b_f):
    B, S, D = x.shape
    tm = PROJ_TM
    hd_all = FOX_HEADS * HEAD_DIM
    width = FOX_HEADS * HEAD_AUG
    wq = w_in[:, :hd_all].astype(jnp.bfloat16)
    wk = w_in[:, hd_all:2 * hd_all].astype(jnp.bfloat16)
    wv = w_in[:, 2 * hd_all:3 * hd_all].astype(jnp.bfloat16)
    wf = w_in[:, 3 * hd_all:]
    wf3 = jnp.concatenate([wf] * N_SPLIT + [jnp.zeros((D, LANES - ONES_LANE), wf.dtype)], axis=1)
    b3 = jnp.concatenate([b_f] * N_SPLIT + [jnp.zeros((LANES - ONES_LANE,), b_f.dtype)])[None, :]
    place = jnp.asarray(_fox_placement(), jnp.bfloat16)
    tri = jnp.asarray(np.tril(np.ones((tm, tm), np.float32)), jnp.bfloat16)
    sel = np.zeros((hd_all, LANES), np.float32)
    for hd in range(FOX_HEADS):
        sel[hd * HEAD_DIM:(hd + 1) * HEAD_DIM, hd] = 1.0

    out_sds = jax.ShapeDtypeStruct((B, S, width), jnp.bfloat16)
    out_spec = pl.BlockSpec((1, tm, width), lambda b, t: (b, t, 0))
    stats_sds = jax.ShapeDtypeStruct((B, S // tm, 8, LANES), jnp.float32)
    stats_spec = pl.BlockSpec((1, 1, 8, LANES), lambda b, t: (b, t, 0, 0))
    return pl.pallas_call(
        functools.partial(_fox_proj_kernel, tm=tm),
        out_shape=(out_sds, out_sds, out_sds, stats_sds),
        grid=(B, S // tm),
        in_specs=[
            pl.BlockSpec((1, tm, D), lambda b, t: (b, t, 0)),
            _const_spec((1, D)),
            _const_spec((D, hd_all)), _const_spec((D, hd_all)), _const_spec((D, hd_all)),
            _const_spec((D, LANES)), _const_spec((1, LANES)),
            _const_spec((LANES, 2 * width)),
            _const_spec((tm, tm)), _const_spec((hd_all, LANES)),
        ],
        out_specs=(out_spec, out_spec, out_spec, stats_spec),
        scratch_shapes=[pltpu.VMEM((1, LANES), jnp.float32)],
        compiler_params=pltpu.CompilerParams(
            dimension_semantics=("arbitrary", "arbitrary"), vmem_limit_bytes=VMEM_LIMIT),
        name="fox_proj",
    )(x, g[None, :], wq, wk, wv, wf3.astype(jnp.bfloat16), b3, place, tri,
      jnp.asarray(sel, jnp.bfloat16))


def _flash_transpose_v(v_ref, vt_ref, qi, bq, seq):
    @pl.when(qi == 0)
    def _():
        for c in range(seq // bq):
            rows = slice(c * bq, (c + 1) * bq)
            vt_ref[:, rows] = v_ref[0, rows, :].T


def _flash_finalize(acc_ref, o_ref, bq):
    halves = []
    for hh in range(2):
        acc = acc_ref[hh]
        halves.append(acc[0:HEAD_DIM, :] * (1.0 / acc[HEAD_DIM:HEAD_DIM + 1, :]))
    o_ref[0] = jnp.concatenate(halves, axis=0).T.astype(jnp.bfloat16)


def _fox_flash_kernel(first_ref, q_ref, k_ref, v_ref, o_ref, vt_ref, qt_ref, m_ref, acc_ref, s_ref, p_ref,
                      mb_ref, *, bq, seq):
    qi = pl.program_id(2)
    heads = 2
    nq = seq // bq
    head0 = (pl.program_id(0) * FOX_HEADS + heads * pl.program_id(1)) * nq + qi
    j0 = jnp.minimum(first_ref[head0], first_ref[head0 + nq])
    _flash_transpose_v(v_ref, vt_ref, qi, bq, seq)
    qt_ref[...] = q_ref[0].T
    m_ref[...] = jnp.full_like(m_ref, NEG_BIG)
    acc_ref[...] = jnp.zeros_like(acc_ref)

    def scores(j, slot, masked):
        start = pl.multiple_of(j * bq, bq)
        for hh in range(heads):
            cols = slice(hh * HEAD_AUG, (hh + 1) * HEAD_AUG)
            st = _dot(k_ref[0, pl.ds(start, bq), cols], qt_ref[cols, :])
            if masked:
                key = lax.broadcasted_iota(jnp.int32, (bq, bq), 0)
                qry = lax.broadcasted_iota(jnp.int32, (bq, bq), 1)
                st = jnp.where(key - qry <= (qi - j) * bq, st, NEG_BIG)
            s_ref[2 * slot + hh] = st
            mb_ref[2 * slot + hh] = jnp.max(st, axis=0, keepdims=True)

    def consume(j, slot):
        start = pl.multiple_of(j * bq, bq)
        alpha = []
        for hh in range(heads):
            m_prev = m_ref[hh]
            m_new = jnp.maximum(m_prev, mb_ref[2 * slot + hh])
            m_ref[hh] = m_new
            p_ref[hh] = jnp.exp2(s_ref[2 * slot + hh] - m_new).astype(jnp.bfloat16)
            alpha.append(jnp.exp2(m_prev - m_new))
        for hh in range(heads):
            cols = slice(hh * HEAD_AUG, (hh + 1) * HEAD_AUG)
            vt = vt_ref[cols, pl.ds(start, bq)]
            acc_ref[hh] = alpha[hh] * acc_ref[hh] + _dot(vt, p_ref[hh])

    def half_step(j, slot, masked):
        scores(j + 1, 1 - slot, masked)
        consume(j, slot)

    span = qi - j0
    pairs = jnp.maximum(span - 1, 0) // 2
    rest = span - 2 * pairs
    scores(j0, 0, True)

    def body(t, carry):
        half_step(j0 + 2 * t, 0, False)
        half_step(j0 + 2 * t + 1, 1, False)
        return carry

    lax.fori_loop(0, pairs, body, 0)

    @pl.when(rest >= 1)
    def _():
        half_step(j0 + 2 * pairs, 0, True)

    @pl.when(rest == 2)
    def _():
        half_step(j0 + 2 * pairs + 1, 1, True)

    @pl.when(rest == 1)
    def _():
        consume(qi, 1)

    @pl.when(rest != 1)
    def _():
        consume(qi, 0)

    _flash_finalize(acc_ref, o_ref, bq)


def _fox_flash_fixed_kernel(first_ref, kall_ref, q_ref, k_ref, v_ref, o_ref, vt_ref, qt_ref, r_ref, acc_ref,
                            p_ref, *, bq, seq):
    b, hp, qi = pl.program_id(0), pl.program_id(1), pl.program_id(2)
    heads = 2
    nq = seq // bq
    head0 = (b * FOX_HEADS + heads * hp) * nq + qi
    first = (first_ref[head0], first_ref[head0 + nq])
    j0 = jnp.maximum(first[0], first[1])
    _flash_transpose_v(v_ref, vt_ref, qi, bq, seq)
    qt_ref[...] = q_ref[0].T
    acc_ref[...] = jnp.zeros_like(acc_ref)
    for hh in range(heads):
        qf = qt_ref[hh * HEAD_AUG:hh * HEAD_AUG + HEAD_DIM, :].astype(jnp.float32)
        qnorm = jnp.sqrt(jnp.sum(qf * qf, axis=0, keepdims=True))
        r_ref[hh] = qnorm * (NORM_SLACK * kall_ref[b * FOX_HEADS + heads * hp + hh])

    def blocks(j, nblk, masked, which=(0, 1)):
        start = pl.multiple_of(j * bq, bq)
        for hh in which:
            cols = slice(hh * HEAD_AUG, (hh + 1) * HEAD_AUG)
            for u in range(nblk):
                st = _dot(k_ref[0, pl.ds(start + u * bq, bq), cols], qt_ref[cols, :])
                if masked and u == nblk - 1:
                    key = lax.broadcasted_iota(jnp.int32, (bq, bq), 0)
                    qry = lax.broadcasted_iota(jnp.int32, (bq, bq), 1)
                    st = jnp.where(key <= qry, st, NEG_BIG)
                p_ref[hh, u * bq:(u + 1) * bq, :] = jnp.exp2(st - r_ref[hh]).astype(jnp.bfloat16)
        for hh in which:
            cols = slice(hh * HEAD_AUG, (hh + 1) * HEAD_AUG)
            vt = vt_ref[cols, pl.ds(start, nblk * bq)]
            acc_ref[hh] += _dot(vt, p_ref[hh, 0:nblk * bq, :])

    for hh in range(heads):
        @pl.when(first[hh] < j0)
        def _(hh=hh):
            lo = first[hh]
            count = j0 - lo

            def body1(t, carry):
                blocks(lo + 4 * t, 4, False, (hh,))
                return carry

            lax.fori_loop(0, count // 4, body1, 0)

            @pl.when(count % 4 >= 2)
            def _():
                blocks(lo + 4 * (count // 4), 2, False, (hh,))

            @pl.when(count % 2 == 1)
            def _():
                blocks(j0 - 1, 1, False, (hh,))

    span = qi - j0
    quads = span // 4
    rest = span - 4 * quads

    def body(t, carry):
        blocks(j0 + 4 * t, 4, False)
        return carry

    lax.fori_loop(0, quads, body, 0)

    @pl.when(rest >= 2)
    def _():
        blocks(j0 + 4 * quads, 2, False)

    @pl.when(rest % 2 == 1)
    def _():
        blocks(qi - 1, 2, True)

    @pl.when(rest % 2 == 0)
    def _():
        blocks(qi, 1, True)

    _flash_finalize(acc_ref, o_ref, bq)


def _first_key_block(stats):
    qmax, kmax = stats[:, :, 0, :FOX_HEADS], stats[:, :, 1, :FOX_HEADS]
    c_first, c_last = stats[:, :, 2, :FOX_HEADS], stats[:, :, 3, :FOX_HEADS]
    diag_min = stats[:, :, 4, :FOX_HEADS]
    kall = jnp.max(kmax, axis=1, keepdims=True)
    thr = c_first + SKIP_LOG2 + (NORM_SLACK + DIAG_SLACK) * qmax * kall - diag_min
    skippable = c_last[:, None, :, :] > thr[:, :, None, :]
    first = jnp.sum(skippable.astype(jnp.int32), axis=2)
    nq = stats.shape[1]
    first = jnp.minimum(first, jnp.arange(nq, dtype=jnp.int32)[None, :, None])
    return first.transpose(0, 2, 1).reshape(-1)


def _fox_flash(q, k, v, stats):
    B, S, width = q.shape
    bq = FLASH_BQ
    pair = 2 * HEAD_AUG
    first = _first_key_block(stats)
    qmax, kmax, diag_min = (stats[:, :, r, :FOX_HEADS] for r in (0, 1, 4))
    kall = jnp.max(kmax, axis=1)
    gap = NORM_SLACK * qmax * kall[:, None, :] - diag_min
    fixed_shift_ok = jnp.max(gap) < SHIFT_GAP_LOG2

    blk = lambda shape, index, **kw: pl.BlockSpec(shape, lambda b, hp, i, first: index(b, hp, i), **kw)
    tensor_specs = [
        blk((1, bq, pair), lambda b, hp, i: (b, i, hp)),
        blk((1, S, pair), lambda b, hp, i: (b, 0, hp)),
        blk((1, S, pair), lambda b, hp, i: (b, 0, hp), pipeline_mode=pl.Buffered(1)),
    ]
    common = dict(
        out_shape=jax.ShapeDtypeStruct((B, S, FOX_HEADS * HEAD_DIM), jnp.bfloat16),
        compiler_params=pltpu.CompilerParams(
            dimension_semantics=("arbitrary", "arbitrary", "arbitrary"),
            vmem_limit_bytes=VMEM_LIMIT))
    grid = (B, FOX_HEADS // 2, S // bq)
    out_spec = blk((1, bq, 2 * HEAD_DIM), lambda b, hp, i: (b, i, hp))
    vt_qt = [pltpu.VMEM((pair, S), jnp.bfloat16), pltpu.VMEM((pair, bq), jnp.bfloat16)]

    def fixed_shift():
        return pl.pallas_call(
            functools.partial(_fox_flash_fixed_kernel, bq=bq, seq=S),
            grid_spec=pltpu.PrefetchScalarGridSpec(
                num_scalar_prefetch=1, grid=grid,
                in_specs=[pl.BlockSpec(memory_space=pltpu.SMEM)] + tensor_specs,
                out_specs=out_spec,
                scratch_shapes=vt_qt + [pltpu.VMEM((2, 1, bq), jnp.float32),
                                        pltpu.VMEM((2, HEAD_AUG, bq), jnp.float32),
                                        pltpu.VMEM((2, 4 * bq, bq), jnp.bfloat16)]),
            name="fox_flash_fixed", **common)(first, kall.reshape(-1), q, k, v)

    def running_max():
        return pl.pallas_call(
            functools.partial(_fox_flash_kernel, bq=bq, seq=S),
            grid_spec=pltpu.PrefetchScalarGridSpec(
                num_scalar_prefetch=1, grid=grid, in_specs=tensor_specs, out_specs=out_spec,
                scratch_shapes=vt_qt + [pltpu.VMEM((2, 1, bq), jnp.float32),
                                        pltpu.VMEM((2, HEAD_AUG, bq), jnp.float32),
                                        pltpu.VMEM((4, bq, bq), jnp.float32),
                                        pltpu.VMEM((2, bq, bq), jnp.bfloat16),
                                        pltpu.VMEM((4, 1, bq), jnp.float32)]),
            name="fox_flash", **common)(first, q, k, v)

    return lax.cond(fixed_shift_ok, fixed_shift, running_max)


def _out_mlp_kernel(o_ref, x_ref, wo_ref, g_post_ref, g_pre_ref, wu_ref, wd_ref, g_ffn_ref, y_ref):
    a = _dot(o_ref[...], wo_ref[...])
    x1 = x_ref[...] + _rms(a, g_post_ref[...])
    h = _rms(x1, g_pre_ref[...]).astype(jnp.bfloat16)
    acc = jnp.zeros_like(x1)
    for c in range(D_FF // MLP_CHUNK):
        cols = slice(c * MLP_CHUNK, (c + 1) * MLP_CHUNK)
        u = jnp.maximum(_dot(h, wu_ref[:, cols]), 0.0)
        acc = acc + _dot((u * u).astype(jnp.bfloat16), wd_ref[cols, :])
    y_ref[...] = x1 + _rms(acc, g_ffn_ref[...])


def _out_mlp(o, x, w_out, g_post, g_pre, w_up, w_down, g_ffn):
    T, D = x.shape
    tm = PROJ_TM
    row = lambda width: pl.BlockSpec((tm, width), lambda t: (t, 0))
    return pl.pallas_call(
        _out_mlp_kernel,
        out_shape=jax.ShapeDtypeStruct((T, D), jnp.float32),
        grid=(T // tm,),
        in_specs=[row(o.shape[1]), row(D),
                  _const_spec(w_out.shape), _const_spec((1, D)), _const_spec((1, D)),
                  _const_spec(w_up.shape), _const_spec(w_down.shape), _const_spec((1, D))],
        out_specs=row(D),
        compiler_params=pltpu.CompilerParams(
            dimension_semantics=("arbitrary",), vmem_limit_bytes=VMEM_LIMIT),
        name="out_mlp",
    )(o, x, w_out.astype(jnp.bfloat16), g_post[None, :], g_pre[None, :],
      w_up.astype(jnp.bfloat16), w_down.astype(jnp.bfloat16), g_ffn[None, :])


def _swa_proj_kernel(x_ref, pos_ref, g_ref, wq_ref, wk_ref, wv_ref, freq_ref, lo_ref, hi_ref,
                     q_out, k_out, v_out):
    h = _rms(x_ref[...], g_ref[...]).astype(jnp.bfloat16)
    ang = pos_ref[...].astype(jnp.float32) * freq_ref[...]
    cos = jnp.cos(ang)
    sin = jnp.sin(ang)
    sin_lo = sin * lo_ref[...]
    sin_hi = sin * hi_ref[...]
    half = HEAD_DIM // 2

    def rope(t):
        return t * cos + pltpu.roll(t, LANES - half, 1) * sin_lo + pltpu.roll(t, half, 1) * sin_hi

    q = _dot(h, wq_ref[...])
    for c in range(q_out.shape[1] // LANES):
        cols = slice(c * LANES, (c + 1) * LANES)
        q_out[:, cols] = (rope(q[:, cols]) * (QK_SCALE * LOG2E)).astype(jnp.bfloat16)
    k = _dot(h, wk_ref[...])
    for c in range(k_out.shape[1] // LANES):
        cols = slice(c * LANES, (c + 1) * LANES)
        k_out[:, cols] = rope(k[:, cols]).astype(jnp.bfloat16)
    v_out[...] = _dot(h, wv_ref[...]).astype(jnp.bfloat16)


def _dup_heads(w, heads):
    d = w.shape[0]
    w = w.reshape(d, heads, 1, HEAD_DIM)
    return jnp.broadcast_to(w, (d, heads, 2, HEAD_DIM)).reshape(d, heads * 2 * HEAD_DIM)


def _swa_proj(x, pos, g, w_in):
    T, D = x.shape
    tm = PROJ_TM
    qd = SWA_Q_HEADS * HEAD_DIM
    kvd = SWA_KV_HEADS * HEAD_DIM
    kw = SWA_KV_HEADS * 2 * HEAD_DIM
    bf = lambda a: a.astype(jnp.bfloat16)
    wq = w_in[:, :qd]
    wk = w_in[:, qd:qd + kvd]
    wv = w_in[:, qd + kvd:]
    half = HEAD_DIM // 2
    inv_freq = 1.0 / (ROPE_THETA ** (jnp.arange(0, HEAD_DIM, 2, dtype=jnp.float32) / HEAD_DIM))
    freq = jnp.tile(inv_freq, LANES // half)[None, :]
    first_half = (np.arange(LANES) % HEAD_DIM) < half
    lo = np.where(first_half, -1.0, 0.0).astype(np.float32)[None, :]
    hi = np.where(first_half, 0.0, 1.0).astype(np.float32)[None, :]
    row = lambda width: pl.BlockSpec((tm, width), lambda t: (t, 0))
    return pl.pallas_call(
        _swa_proj_kernel,
        out_shape=(jax.ShapeDtypeStruct((T, qd), jnp.bfloat16),
                   jax.ShapeDtypeStruct((T, kw), jnp.bfloat16),
                   jax.ShapeDtypeStruct((T, kw), jnp.bfloat16)),
        grid=(T // tm,),
        in_specs=[row(D), row(1), _const_spec((1, D)),
                  _const_spec((D, qd)), _const_spec((D, kw)), _const_spec((D, kw)),
                  _const_spec((1, LANES)), _const_spec((1, LANES)), _const_spec((1, LANES))],
        out_specs=(row(qd), row(kw), row(kw)),
        compiler_params=pltpu.CompilerParams(
            dimension_semantics=("arbitrary",), vmem_limit_bytes=VMEM_LIMIT),
        name="swa_proj",
    )(x, pos, g[None, :], bf(wq), bf(_dup_heads(wk, SWA_KV_HEADS)), bf(_dup_heads(wv, SWA_KV_HEADS)),
      freq, jnp.asarray(lo), jnp.asarray(hi))


def _swa_attn_kernel(sink_ref, q_ref, *refs):
    k_refs, v_refs = refs[:SWA_SUB + 1], refs[SWA_SUB + 1:2 * SWA_SUB + 2]
    o_ref, s_ref = refs[2 * SWA_SUB + 2:]
    n = pl.program_id(1)
    blk = WINDOW
    cols_per_group = SWA_GROUP // 2
    key = lax.broadcasted_iota(jnp.int32, (2 * blk, blk), 0)
    qry = lax.broadcasted_iota(jnp.int32, (2 * blk, blk), 1)
    diff = qry + blk - key
    in_window = (diff >= 0) & (diff < WINDOW)
    klane = lax.broadcasted_iota(jnp.int32, (2 * blk, LANES), 1)
    top = lax.broadcasted_iota(jnp.int32, (LANES, blk), 0) < HEAD_DIM
    maxes = {}
    for sub in range(SWA_SUB):
        rows = slice(sub * blk, (sub + 1) * blk)
        kb = jnp.concatenate([k_refs[sub][0], k_refs[sub + 1][0]], axis=0).astype(jnp.float32)
        mask = in_window & (key >= jnp.where(n > 0, 0, blk)) if sub == 0 else in_window
        for g in range(SWA_KV_HEADS):
            gl = slice(g * LANES, (g + 1) * LANES)
            qt = jnp.concatenate([q_ref[0, rows, c * LANES:(c + 1) * LANES].T
                                  for c in range(g * cols_per_group, (g + 1) * cols_per_group)], axis=1)
            for par in range(2):
                keep = (klane < HEAD_DIM) if par == 0 else (klane >= HEAD_DIM)
                kpar = jnp.where(keep, kb[:, gl], 0.0).astype(jnp.bfloat16)
                st = _dot(kpar, qt)
                st = jnp.concatenate([jnp.where(mask, st[:, c * blk:(c + 1) * blk], NEG_BIG)
                                      for c in range(cols_per_group)], axis=1)
                slot = (sub * SWA_KV_HEADS + g) * 2 + par
                s_ref[slot] = st
                maxes[slot] = jnp.max(st, axis=0, keepdims=True)
    for sub in range(SWA_SUB):
        rows = slice(sub * blk, (sub + 1) * blk)
        vbt = jnp.concatenate([v_refs[sub][0], v_refs[sub + 1][0]], axis=0).T
        for g in range(SWA_KV_HEADS):
            gl = slice(g * LANES, (g + 1) * LANES)
            outs = []
            for par in range(2):
                slot = (sub * SWA_KV_HEADS + g) * 2 + par
                sink = sink_ref[2 * g + par:2 * g + par + 1, :]
                m = jnp.maximum(maxes[slot], sink)
                e = jnp.exp2(s_ref[slot] - m)
                denom = jnp.sum(e, axis=0, keepdims=True) + jnp.exp2(sink - m)
                ot = _dot(vbt[gl, :], e.astype(jnp.bfloat16))
                outs.append(ot * (1.0 / denom))
            for c in range(cols_per_group):
                both = jnp.where(top, outs[0][:, c * blk:(c + 1) * blk], outs[1][:, c * blk:(c + 1) * blk])
                col = g * cols_per_group + c
                o_ref[0, rows, col * LANES:(col + 1) * LANES] = both.T.astype(jnp.bfloat16)


def _swa_attn(q, k, v, sinks):
    B, S, qd = q.shape
    kw = k.shape[2]
    blk = WINDOW
    own = lambda b, n: (b, n, 0)
    band = [lambda b, n: (b, jnp.maximum(SWA_SUB * n - 1, 0), 0)]
    band += [functools.partial(lambda b, n, i: (b, SWA_SUB * n + i, 0), i=i) for i in range(SWA_SUB)]
    band_specs = [pl.BlockSpec((1, blk, kw), index) for index in band]
    cols_per_group = SWA_GROUP // 2
    sink_rows = (sinks * LOG2E).reshape(SWA_KV_HEADS, cols_per_group, 2).transpose(0, 2, 1)
    sink_rows = jnp.repeat(sink_rows.reshape(2 * SWA_KV_HEADS, cols_per_group), blk, axis=1)
    sink_rows = jnp.pad(sink_rows, ((0, 8 - 2 * SWA_KV_HEADS), (0, 0)))
    return pl.pallas_call(
        _swa_attn_kernel,
        out_shape=jax.ShapeDtypeStruct((B, S, qd), jnp.bfloat16),
        grid=(B, S // (SWA_SUB * blk)),
        in_specs=[_const_spec(sink_rows.shape), pl.BlockSpec((1, SWA_SUB * blk, qd), own)] + band_specs + band_specs,
        out_specs=pl.BlockSpec((1, SWA_SUB * blk, qd), own),
        scratch_shapes=[pltpu.VMEM((SWA_SUB * 2 * SWA_KV_HEADS, 2 * blk, cols_per_group * blk), jnp.float32)],
        compiler_params=pltpu.CompilerParams(
            dimension_semantics=("arbitrary", "arbitrary"), vmem_limit_bytes=VMEM_LIMIT),
        name="swa_attn",
    )(sink_rows, q, *([k] * (SWA_SUB + 1)), *([v] * (SWA_SUB + 1)))


def kernel(x, positions, fox_w_in, fox_b_f, fox_w_out, swa_w_in, swa_sinks, swa_w_out,
           norm_pre_mix, norm_post_mix, norm_pre_ffn, norm_post_ffn, mlp_w_up, mlp_w_down):
    B, S, D = x.shape
    T = B * S
    q, k, v, stats = _fox_proj(x, norm_pre_mix[0], fox_w_in[0], fox_b_f[0])
    o = _fox_flash(q, k, v, stats)
    x1 = _out_mlp(o.reshape(T, -1), x.reshape(T, D), fox_w_out[0], norm_post_mix[0], norm_pre_ffn[0],
                  mlp_w_up[0], mlp_w_down[0], norm_post_ffn[0])
    q, k, v = _swa_proj(x1, positions.reshape(T, 1), norm_pre_mix[1], swa_w_in[0])
    o = _swa_attn(q.reshape(B, S, -1), k.reshape(B, S, -1), v.reshape(B, S, -1), swa_sinks[0])
    x2 = _out_mlp(o.reshape(T, -1), x1, swa_w_out[0], norm_post_mix[1], norm_pre_ffn[1],
                  mlp_w_up[1], mlp_w_down[1], norm_post_ffn[1])
    return x2.reshape(B, S, D)
```

```python
import functools
import math

import jax
import jax.numpy as jnp
import numpy as np
from jax import lax
from jax.experimental import pallas as pl
from jax.experimental.pallas import tpu as pltpu

D_MODEL = 1024
HEAD_DIM = 64
FOX_HEADS = 16
SWA_Q_HEADS = 16
SWA_KV_HEADS = 2
SWA_GROUP = SWA_Q_HEADS // SWA_KV_HEADS
WINDOW = 128
D_FF = 4 * D_MODEL
ROPE_THETA = 10000.0
NORM_EPS = 1e-6

LANES = 128
HEAD_AUG = LANES
LOG2E = 1.4426950408889634
QK_SCALE = HEAD_DIM ** -0.5
NEG_BIG = -1e30
SKIP_LOG2 = 152.0
NORM_SLACK = 1.02
DIAG_SLACK = 0.02
SHIFT_GAP_LOG2 = 100.0
VMEM_LIMIT = 56 * 1024 * 1024

DEC0 = HEAD_DIM
N_SPLIT = 3
ONES_LANE = N_SPLIT * FOX_HEADS

PROJ_TM = 512
FLASH_BQ = 512
FLASH_NSUB = 4
MLP_TM = 1024
MLP_CHUNK = 1024
SWA_SUB = 4


def _rms(x, g):
    return x * lax.rsqrt(jnp.mean(x * x, axis=-1, keepdims=True) + NORM_EPS) * g


def _dot(a, b):
    return jnp.dot(a, b, preferred_element_type=jnp.float32)


def _dot_nt(a, b):
    return lax.dot_general(a, b, (((1,), (1,)), ((), ())), preferred_element_type=jnp.float32)


def _split3(x):
    hi = x.astype(jnp.bfloat16)
    r = x - hi.astype(jnp.float32)
    mid = r.astype(jnp.bfloat16)
    lo = (r - mid.astype(jnp.float32)).astype(jnp.bfloat16)
    return hi, mid, lo


def _const_spec(shape):
    nd = len(shape)
    return pl.BlockSpec(shape, lambda *_: (0,) * nd, pipeline_mode=pl.Buffered(1))


def _fox_proj_kernel(x_ref, g_ref, wq_ref, wk_ref, wv_ref, wf_ref, b3_ref, place_ref,
                     tri_ref, sel_ref, q_out, k_out, v_out, stats_out, carry_ref, *, tm):
    @pl.when(pl.program_id(1) == 0)
    def _():
        carry_ref[...] = jnp.zeros_like(carry_ref)

    h = _rms(x_ref[0], g_ref[...]).astype(jnp.bfloat16)

    z = _dot(h, wf_ref[...]) + b3_ref[...]
    logf = jnp.minimum(z, 0.0) - jnp.log1p(jnp.exp(-jnp.abs(z)))

    tri = tri_ref[...]
    hi, mid, lo = _split3(logf)
    c = _dot(tri, hi) + _dot(tri, mid) + _dot(tri, lo) + carry_ref[...]
    carry_ref[...] = c[tm - 1:tm, :]

    chi, cmid, clo = (p.astype(jnp.float32) for p in _split3(c * LOG2E))
    lane = lax.broadcasted_iota(jnp.int32, (tm, LANES), 1)
    parts = jnp.where(lane < FOX_HEADS, chi,
                      jnp.where(lane < 2 * FOX_HEADS, cmid,
                                jnp.where(lane < ONES_LANE, clo,
                                          jnp.where(lane == ONES_LANE, 1.0, 0.0))))
    parts = parts.astype(jnp.bfloat16)

    width = FOX_HEADS * HEAD_AUG
    q = _dot(h, wq_ref[...]) * (QK_SCALE * LOG2E)
    k = _dot(h, wk_ref[...])
    v = _dot(h, wv_ref[...])
    qn2 = _dot((q * q).astype(jnp.bfloat16), sel_ref[...])
    kn2 = _dot((k * k).astype(jnp.bfloat16), sel_ref[...])

    low = lane < HEAD_DIM
    v_tail = jnp.where(lane == HEAD_DIM, 1.0, 0.0)
    for pr in range(FOX_HEADS // 2):
        pcols = slice(pr * LANES, (pr + 1) * LANES)
        acols = slice(2 * pr * HEAD_AUG, (2 * pr + 2) * HEAD_AUG)
        q_tail = _dot(parts, place_ref[:, acols])
        k_tail = _dot(parts, place_ref[:, width + 2 * pr * HEAD_AUG:width + (2 * pr + 2) * HEAD_AUG])
        for src, tail, out in ((q[:, pcols], q_tail, q_out), (k[:, pcols], k_tail, k_out),
                               (v[:, pcols], None, v_out)):
            for par, head in enumerate((src, pltpu.roll(src, HEAD_DIM, 1))):
                fill = v_tail if tail is None else tail[:, par * HEAD_AUG:(par + 1) * HEAD_AUG]
                hcols = slice((2 * pr + par) * HEAD_AUG, (2 * pr + par + 1) * HEAD_AUG)
                out[0, :, hcols] = jnp.where(low, head, fill).astype(jnp.bfloat16)

    c2 = c * LOG2E
    stats_out[0, 0] = jnp.zeros((8, LANES), jnp.float32)
    stats_out[0, 0, 0:1, :] = jnp.sqrt(jnp.max(qn2, axis=0, keepdims=True))
    stats_out[0, 0, 1:2, :] = jnp.sqrt(jnp.max(kn2, axis=0, keepdims=True))
    stats_out[0, 0, 2:3, :] = c2[0:1, :]
    stats_out[0, 0, 3:4, :] = c2[tm - 1:tm, :]
    diag = _dot((q * k).astype(jnp.bfloat16), sel_ref[...])
    stats_out[0, 0, 4:5, :] = jnp.min(diag, axis=0, keepdims=True)


def _fox_placement():
    width = FOX_HEADS * HEAD_AUG
    p = np.zeros((LANES, 2 * width), np.float32)
    for h in range(FOX_HEADS):
        base = h * HEAD_AUG + DEC0
        for s in range(N_SPLIT):
            p[s * FOX_HEADS + h, base + s] = 1.0
            p[ONES_LANE, base + N_SPLIT + s] = 1.0
            p[ONES_LANE, width + base + s] = 1.0
            p[s * FOX_HEADS + h, width + base + N_SPLIT + s] = -1.0
    return p


def _fox_proj(x, g, w_in, b_f):
    B, S, D = x.shape
    tm = PROJ_TM
    hd_all = FOX_HEADS * HEAD_DIM
    width = FOX_HEADS * HEAD_AUG
    wq = w_in[:, :hd_all].astype(jnp.bfloat16)
    wk = w_in[:, hd_all:2 * hd_all].astype(jnp.bfloat16)
    wv = w_in[:, 2 * hd_all:3 * hd_all].astype(jnp.bfloat16)
    wf = w_in[:, 3 * hd_all:]
    wf3 = jnp.concatenate([wf] * N_SPLIT + [jnp.zeros((D, LANES - ONES_LANE), wf.dtype)], axis=1)
    b3 = jnp.concatenate([b_f] * N_SPLIT + [jnp.zeros((LANES - ONES_LANE,), b_f.dtype)])[None, :]
    place = jnp.asarray(_fox_placement(), jnp.bfloat16)
    tri = jnp.asarray(np.tril(np.ones((tm, tm), np.float32)), jnp.bfloat16)
    sel = np.zeros((hd_all, LANES), np.float32)
    for hd in range(FOX_HEADS):
        sel[hd * HEAD_DIM:(hd + 1) * HEAD_DIM, hd] = 1.0

    out_sds = jax.ShapeDtypeStruct((B, S, width), jnp.bfloat16)
    out_spec = pl.BlockSpec((1, tm, width), lambda b, t: (b, t, 0))
    stats_sds = jax.ShapeDtypeStruct((B, S // tm, 8, LANES), jnp.float32)
    stats_spec = pl.BlockSpec((1, 1, 8, LANES), lambda b, t: (b, t, 0, 0))
    return pl.pallas_call(
        functools.partial(_fox_proj_kernel, tm=tm),
        out_shape=(out_sds, out_sds, out_sds, stats_sds),
        grid=(B, S // tm),
        in_specs=[
            pl.BlockSpec((1, tm, D), lambda b, t: (b, t, 0)),
            _const_spec((1, D)),
            _const_spec((D, hd_all)), _const_spec((D, hd_all)), _const_spec((D, hd_all)),
            _const_spec((D, LANES)), _const_spec((1, LANES)),
            _const_spec((LANES, 2 * width)),
            _const_spec((tm, tm)), _const_spec((hd_all, LANES)),
        ],
        out_specs=(out_spec, out_spec, out_spec, stats_spec),
        scratch_shapes=[pltpu.VMEM((1, LANES), jnp.float32)],
        compiler_params=pltpu.CompilerParams(
            dimension_semantics=("arbitrary", "arbitrary"), vmem_limit_bytes=VMEM_LIMIT),
        name="fox_proj",
    )(x, g[None, :], wq, wk, wv, wf3.astype(jnp.bfloat16), b3, place, tri,
      jnp.asarray(sel, jnp.bfloat16))


def _flash_transpose_v(v_ref, vt_ref, qi, bq, seq):
    @pl.when(qi == 0)
    def _():
        for c in range(seq // bq):
            rows = slice(c * bq, (c + 1) * bq)
            vt_ref[:, rows] = v_ref[0, rows, :].T


def _flash_finalize(acc_ref, o_ref, bq):
    halves = []
    for hh in range(2):
        acc = acc_ref[hh]
        halves.append(acc[0:HEAD_DIM, :] * (1.0 / acc[HEAD_DIM:HEAD_DIM + 1, :]))
    o_ref[...] = jnp.concatenate(halves, axis=0).T.astype(jnp.bfloat16)


def _per_query_block(block_fn, q_ref, o_ref, bq, nsub):
    def one(u, carry):
        rows = pl.ds(pl.multiple_of(u * bq, bq), bq)
        block_fn(nsub * pl.program_id(2) + u, q_ref.at[0, rows, :], o_ref.at[0, rows, :])
        return carry

    lax.fori_loop(0, nsub, one, 0)


def _fox_flash_kernel(first_ref, q_ref, k_ref, v_ref, o_ref, *scratch, bq, seq, nsub):
    block_fn = functools.partial(_flash_running_max_block, first_ref=first_ref, k_ref=k_ref, v_ref=v_ref,
                                 scratch=scratch, bq=bq, seq=seq)
    _per_query_block(block_fn, q_ref, o_ref, bq, nsub)


def _flash_running_max_block(qi, q_ref, o_ref, *, first_ref, k_ref, v_ref, scratch, bq, seq):
    vt_ref, qt_ref, m_ref, acc_ref, s_ref, p_ref, mb_ref = scratch
    heads = 2
    nq = seq // bq
    head0 = (pl.program_id(0) * FOX_HEADS + heads * pl.program_id(1)) * nq + qi
    j0 = jnp.minimum(first_ref[head0], first_ref[head0 + nq])
    _flash_transpose_v(v_ref, vt_ref, qi, bq, seq)
    qt_ref[...] = q_ref[...].T
    m_ref[...] = jnp.full_like(m_ref, NEG_BIG)
    acc_ref[...] = jnp.zeros_like(acc_ref)

    def scores(j, slot, masked):
        start = pl.multiple_of(j * bq, bq)
        for hh in range(heads):
            cols = slice(hh * HEAD_AUG, (hh + 1) * HEAD_AUG)
            st = _dot(k_ref[0, pl.ds(start, bq), cols], qt_ref[cols, :])
            if masked:
                key = lax.broadcasted_iota(jnp.int32, (bq, bq), 0)
                qry = lax.broadcasted_iota(jnp.int32, (bq, bq), 1)
                st = jnp.where(key - qry <= (qi - j) * bq, st, NEG_BIG)
            s_ref[2 * slot + hh] = st
            mb_ref[2 * slot + hh] = jnp.max(st, axis=0, keepdims=True)

    def consume(j, slot):
        start = pl.multiple_of(j * bq, bq)
        alpha = []
        for hh in range(heads):
            m_prev = m_ref[hh]
            m_new = jnp.maximum(m_prev, mb_ref[2 * slot + hh])
            m_ref[hh] = m_new
            p_ref[hh] = jnp.exp2(s_ref[2 * slot + hh] - m_new).astype(jnp.bfloat16)
            alpha.append(jnp.exp2(m_prev - m_new))
        for hh in range(heads):
            cols = slice(hh * HEAD_AUG, (hh + 1) * HEAD_AUG)
            vt = vt_ref[cols, pl.ds(start, bq)]
            acc_ref[hh] = alpha[hh] * acc_ref[hh] + _dot(vt, p_ref[hh])

    def half_step(j, slot, masked):
        scores(j + 1, 1 - slot, masked)
        consume(j, slot)

    span = qi - j0
    pairs = jnp.maximum(span - 1, 0) // 2
    rest = span - 2 * pairs
    scores(j0, 0, True)

    def body(t, carry):
        half_step(j0 + 2 * t, 0, False)
        half_step(j0 + 2 * t + 1, 1, False)
        return carry

    lax.fori_loop(0, pairs, body, 0)

    @pl.when(rest >= 1)
    def _():
        half_step(j0 + 2 * pairs, 0, True)

    @pl.when(rest == 2)
    def _():
        half_step(j0 + 2 * pairs + 1, 1, True)

    @pl.when(rest == 1)
    def _():
        consume(qi, 1)

    @pl.when(rest != 1)
    def _():
        consume(qi, 0)

    _flash_finalize(acc_ref, o_ref, bq)


def _fox_flash_fixed_kernel(first_ref, kall_ref, q_ref, k_ref, v_ref, o_ref, *scratch, bq, seq, nsub):
    block_fn = functools.partial(_flash_fixed_shift_block, first_ref=first_ref, kall_ref=kall_ref, k_ref=k_ref,
                                 v_ref=v_ref, scratch=scratch, bq=bq, seq=seq)
    _per_query_block(block_fn, q_ref, o_ref, bq, nsub)


def _flash_fixed_shift_block(qi, q_ref, o_ref, *, first_ref, kall_ref, k_ref, v_ref, scratch, bq, seq):
    vt_ref, qt_ref, r_ref, acc_ref, p_ref = scratch
    b, hp = pl.program_id(0), pl.program_id(1)
    heads = 2
    nq = seq // bq
    head0 = (b * FOX_HEADS + heads * hp) * nq + qi
    first = (first_ref[head0], first_ref[head0 + nq])
    j0 = jnp.maximum(first[0], first[1])
    _flash_transpose_v(v_ref, vt_ref, qi, bq, seq)
    qt_ref[...] = q_ref[...].T
    acc_ref[...] = jnp.zeros_like(acc_ref)
    for hh in range(heads):
        qf = qt_ref[hh * HEAD_AUG:hh * HEAD_AUG + HEAD_DIM, :].astype(jnp.float32)
        qnorm = jnp.sqrt(jnp.sum(qf * qf, axis=0, keepdims=True))
        r_ref[hh] = qnorm * (NORM_SLACK * kall_ref[b * FOX_HEADS + heads * hp + hh])

    def blocks(j, nblk, masked, which=(0, 1)):
        start = pl.multiple_of(j * bq, bq)
        for hh in which:
            cols = slice(hh * HEAD_AUG, (hh + 1) * HEAD_AUG)
            for u in range(nblk):
                st = _dot(k_ref[0, pl.ds(start + u * bq, bq), cols], qt_ref[cols, :])
                if masked and u == nblk - 1:
                    key = lax.broadcasted_iota(jnp.int32, (bq, bq), 0)
                    qry = lax.broadcasted_iota(jnp.int32, (bq, bq), 1)
                    st = jnp.where(key <= qry, st, NEG_BIG)
                p_ref[hh, u * bq:(u + 1) * bq, :] = jnp.exp2(st - r_ref[hh]).astype(jnp.bfloat16)
        for hh in which:
            cols = slice(hh * HEAD_AUG, (hh + 1) * HEAD_AUG)
            vt = vt_ref[cols, pl.ds(start, nblk * bq)]
            acc_ref[hh] += _dot(vt, p_ref[hh, 0:nblk * bq, :])

    for hh in range(heads):
        @pl.when(first[hh] < j0)
        def _(hh=hh):
            lo = first[hh]
            count = j0 - lo

            def body1(t, carry):
                blocks(lo + 4 * t, 4, False, (hh,))
                return carry

            lax.fori_loop(0, count // 4, body1, 0)

            @pl.when(count % 4 >= 2)
            def _():
                blocks(lo + 4 * (count // 4), 2, False, (hh,))

            @pl.when(count % 2 == 1)
            def _():
                blocks(j0 - 1, 1, False, (hh,))

    span = qi - j0
    quads = span // 4
    rest = span - 4 * quads

    def body(t, carry):
        blocks(j0 + 4 * t, 4, False)
        return carry

    lax.fori_loop(0, quads, body, 0)

    @pl.when(rest >= 2)
    def _():
        blocks(j0 + 4 * quads, 2, False)

    @pl.when(rest % 2 == 1)
    def _():
        blocks(qi - 1, 2, True)

    @pl.when(rest % 2 == 0)
    def _():
        blocks(qi, 1, True)

    _flash_finalize(acc_ref, o_ref, bq)


def _first_key_block(stats):
    qmax, kmax = stats[:, :, 0, :FOX_HEADS], stats[:, :, 1, :FOX_HEADS]
    c_first, c_last = stats[:, :, 2, :FOX_HEADS], stats[:, :, 3, :FOX_HEADS]
    diag_min = stats[:, :, 4, :FOX_HEADS]
    kall = jnp.max(kmax, axis=1, keepdims=True)
    thr = c_first + SKIP_LOG2 + (NORM_SLACK + DIAG_SLACK) * qmax * kall - diag_min
    skippable = c_last[:, None, :, :] > thr[:, :, None, :]
    first = jnp.sum(skippable.astype(jnp.int32), axis=2)
    nq = stats.shape[1]
    first = jnp.minimum(first, jnp.arange(nq, dtype=jnp.int32)[None, :, None])
    return first.transpose(0, 2, 1).reshape(-1)


def _fox_flash(q, k, v, stats):
    B, S, width = q.shape
    bq = FLASH_BQ
    pair = 2 * HEAD_AUG
    first = _first_key_block(stats)
    qmax, kmax, diag_min = (stats[:, :, r, :FOX_HEADS] for r in (0, 1, 4))
    kall = jnp.max(kmax, axis=1)
    gap = NORM_SLACK * qmax * kall[:, None, :] - diag_min
    fixed_shift_ok = jnp.max(gap) < SHIFT_GAP_LOG2

    blk = lambda shape, index, **kw: pl.BlockSpec(shape, lambda b, hp, i, first: index(b, hp, i), **kw)
    tensor_specs = [
        blk((1, FLASH_NSUB * bq, pair), lambda b, hp, i: (b, i, hp)),
        blk((1, S, pair), lambda b, hp, i: (b, 0, hp)),
        blk((1, S, pair), lambda b, hp, i: (b, 0, hp), pipeline_mode=pl.Buffered(1)),
    ]
    common = dict(
        out_shape=jax.ShapeDtypeStruct((B, S, FOX_HEADS * HEAD_DIM), jnp.bfloat16),
        compiler_params=pltpu.CompilerParams(
            dimension_semantics=("arbitrary", "arbitrary", "arbitrary"),
            vmem_limit_bytes=VMEM_LIMIT))
    grid = (B, FOX_HEADS // 2, S // (FLASH_NSUB * bq))
    out_spec = blk((1, FLASH_NSUB * bq, 2 * HEAD_DIM), lambda b, hp, i: (b, i, hp))
    vt_qt = [pltpu.VMEM((pair, S), jnp.bfloat16), pltpu.VMEM((pair, bq), jnp.bfloat16)]

    def fixed_shift():
        return pl.pallas_call(
            functools.partial(_fox_flash_fixed_kernel, bq=bq, seq=S, nsub=FLASH_NSUB),
            grid_spec=pltpu.PrefetchScalarGridSpec(
                num_scalar_prefetch=1, grid=grid,
                in_specs=[pl.BlockSpec(memory_space=pltpu.SMEM)] + tensor_specs,
                out_specs=out_spec,
                scratch_shapes=vt_qt + [pltpu.VMEM((2, 1, bq), jnp.float32),
                                        pltpu.VMEM((2, HEAD_AUG, bq), jnp.float32),
                                        pltpu.VMEM((2, 4 * bq, bq), jnp.bfloat16)]),
            name="fox_flash_fixed", **common)(first, kall.reshape(-1), q, k, v)

    def running_max():
        return pl.pallas_call(
            functools.partial(_fox_flash_kernel, bq=bq, seq=S, nsub=FLASH_NSUB),
            grid_spec=pltpu.PrefetchScalarGridSpec(
                num_scalar_prefetch=1, grid=grid, in_specs=tensor_specs, out_specs=out_spec,
                scratch_shapes=vt_qt + [pltpu.VMEM((2, 1, bq), jnp.float32),
                                        pltpu.VMEM((2, HEAD_AUG, bq), jnp.float32),
                                        pltpu.VMEM((4, bq, bq), jnp.float32),
                                        pltpu.VMEM((2, bq, bq), jnp.bfloat16),
                                        pltpu.VMEM((4, 1, bq), jnp.float32)]),
            name="fox_flash", **common)(first, q, k, v)

    return lax.cond(fixed_shift_ok, fixed_shift, running_max)


def _out_mlp_kernel(o_ref, x_ref, wo_ref, g_post_ref, g_pre_ref, wu_ref, wd_ref, g_ffn_ref, y_ref,
                    x1_ref, h_ref, *, sub):
    n_sub = x_ref.shape[0] // sub
    for s in range(n_sub):
        rows = slice(s * sub, (s + 1) * sub)
        a = _dot(o_ref[rows, :], wo_ref[...])
        x1 = x_ref[rows, :] + _rms(a, g_post_ref[...])
        x1_ref[rows, :] = x1
        h_ref[rows, :] = _rms(x1, g_pre_ref[...]).astype(jnp.bfloat16)
    for s in range(n_sub):
        rows = slice(s * sub, (s + 1) * sub)
        acc = jnp.zeros((sub, x_ref.shape[1]), jnp.float32)
        for c in range(D_FF // MLP_CHUNK):
            cols = slice(c * MLP_CHUNK, (c + 1) * MLP_CHUNK)
            u = jnp.maximum(_dot(h_ref[rows, :], wu_ref[:, cols]), 0.0)
            acc = acc + _dot((u * u).astype(jnp.bfloat16), wd_ref[cols, :])
        y_ref[rows, :] = x1_ref[rows, :] + _rms(acc, g_ffn_ref[...])


def _out_mlp(o, x, w_out, g_post, g_pre, w_up, w_down, g_ffn):
    T, D = x.shape
    tm = MLP_TM
    row = lambda width: pl.BlockSpec((tm, width), lambda t: (t, 0))
    return pl.pallas_call(
        functools.partial(_out_mlp_kernel, sub=PROJ_TM),
        out_shape=jax.ShapeDtypeStruct((T, D), jnp.float32),
        grid=(T // tm,),
        in_specs=[row(o.shape[1]), row(D),
                  _const_spec(w_out.shape), _const_spec((1, D)), _const_spec((1, D)),
                  _const_spec(w_up.shape), _const_spec(w_down.shape), _const_spec((1, D))],
        out_specs=row(D),
        scratch_shapes=[pltpu.VMEM((tm, D), jnp.float32), pltpu.VMEM((tm, D), jnp.bfloat16)],
        compiler_params=pltpu.CompilerParams(
            dimension_semantics=("arbitrary",), vmem_limit_bytes=VMEM_LIMIT),
        name="out_mlp",
    )(o, x, w_out.astype(jnp.bfloat16), g_post[None, :], g_pre[None, :],
      w_up.astype(jnp.bfloat16), w_down.astype(jnp.bfloat16), g_ffn[None, :])


def _swa_proj_kernel(x_ref, pos_ref, g_ref, wq_ref, wk_ref, wv_ref, freq_ref, lo_ref, hi_ref,
                     q_out, k_out, v_out):
    h = _rms(x_ref[...], g_ref[...]).astype(jnp.bfloat16)
    ang = pos_ref[...].astype(jnp.float32) * freq_ref[...]
    cos = jnp.cos(ang)
    sin = jnp.sin(ang)
    sin_lo = sin * lo_ref[...]
    sin_hi = sin * hi_ref[...]
    half = HEAD_DIM // 2

    def rope(t):
        return t * cos + pltpu.roll(t, LANES - half, 1) * sin_lo + pltpu.roll(t, half, 1) * sin_hi

    q = _dot(h, wq_ref[...])
    for c in range(q_out.shape[1] // LANES):
        cols = slice(c * LANES, (c + 1) * LANES)
        q_out[:, cols] = (rope(q[:, cols]) * (QK_SCALE * LOG2E)).astype(jnp.bfloat16)
    k = _dot(h, wk_ref[...])
    for c in range(k_out.shape[1] // LANES):
        cols = slice(c * LANES, (c + 1) * LANES)
        k_out[:, cols] = rope(k[:, cols]).astype(jnp.bfloat16)
    v_out[...] = _dot(h, wv_ref[...]).astype(jnp.bfloat16)


def _dup_heads(w, heads):
    d = w.shape[0]
    w = w.reshape(d, heads, 1, HEAD_DIM)
    return jnp.broadcast_to(w, (d, heads, 2, HEAD_DIM)).reshape(d, heads * 2 * HEAD_DIM)


def _swa_proj(x, pos, g, w_in):
    T, D = x.shape
    tm = PROJ_TM
    qd = SWA_Q_HEADS * HEAD_DIM
    kvd = SWA_KV_HEADS * HEAD_DIM
    kw = SWA_KV_HEADS * 2 * HEAD_DIM
    bf = lambda a: a.astype(jnp.bfloat16)
    wq = w_in[:, :qd]
    wk = w_in[:, qd:qd + kvd]
    wv = w_in[:, qd + kvd:]
    half = HEAD_DIM // 2
    inv_freq = 1.0 / (ROPE_THETA ** (jnp.arange(0, HEAD_DIM, 2, dtype=jnp.float32) / HEAD_DIM))
    freq = jnp.tile(inv_freq, LANES // half)[None, :]
    first_half = (np.arange(LANES) % HEAD_DIM) < half
    lo = np.where(first_half, -1.0, 0.0).astype(np.float32)[None, :]
    hi = np.where(first_half, 0.0, 1.0).astype(np.float32)[None, :]
    row = lambda width: pl.BlockSpec((tm, width), lambda t: (t, 0))
    return pl.pallas_call(
        _swa_proj_kernel,
        out_shape=(jax.ShapeDtypeStruct((T, qd), jnp.bfloat16),
                   jax.ShapeDtypeStruct((T, kw), jnp.bfloat16),
                   jax.ShapeDtypeStruct((T, kw), jnp.bfloat16)),
        grid=(T // tm,),
        in_specs=[row(D), row(1), _const_spec((1, D)),
                  _const_spec((D, qd)), _const_spec((D, kw)), _const_spec((D, kw)),
                  _const_spec((1, LANES)), _const_spec((1, LANES)), _const_spec((1, LANES))],
        out_specs=(row(qd), row(kw), row(kw)),
        compiler_params=pltpu.CompilerParams(
            dimension_semantics=("arbitrary",), vmem_limit_bytes=VMEM_LIMIT),
        name="swa_proj",
    )(x, pos, g[None, :], bf(wq), bf(_dup_heads(wk, SWA_KV_HEADS)), bf(_dup_heads(wv, SWA_KV_HEADS)),
      freq, jnp.asarray(lo), jnp.asarray(hi))


def _swa_attn_kernel(sink_ref, q_ref, *refs):
    k_refs, v_refs = refs[:SWA_SUB + 1], refs[SWA_SUB + 1:2 * SWA_SUB + 2]
    o_ref, s_ref = refs[2 * SWA_SUB + 2:]
    n = pl.program_id(1)
    blk = WINDOW
    cols_per_group = SWA_GROUP // 2
    key = lax.broadcasted_iota(jnp.int32, (2 * blk, blk), 0)
    qry = lax.broadcasted_iota(jnp.int32, (2 * blk, blk), 1)
    diff = qry + blk - key
    in_window = (diff >= 0) & (diff < WINDOW)
    klane = lax.broadcasted_iota(jnp.int32, (2 * blk, LANES), 1)
    top = lax.broadcasted_iota(jnp.int32, (LANES, blk), 0) < HEAD_DIM
    maxes = {}
    for sub in range(SWA_SUB):
        rows = slice(sub * blk, (sub + 1) * blk)
        kb = jnp.concatenate([k_refs[sub][0], k_refs[sub + 1][0]], axis=0).astype(jnp.float32)
        mask = in_window & (key >= jnp.where(n > 0, 0, blk)) if sub == 0 else in_window
        for g in range(SWA_KV_HEADS):
            gl = slice(g * LANES, (g + 1) * LANES)
            qt = jnp.concatenate([q_ref[0, rows, c * LANES:(c + 1) * LANES].T
                                  for c in range(g * cols_per_group, (g + 1) * cols_per_group)], axis=1)
            for par in range(2):
                keep = (klane < HEAD_DIM) if par == 0 else (klane >= HEAD_DIM)
                kpar = jnp.where(keep, kb[:, gl], 0.0).astype(jnp.bfloat16)
                st = _dot(kpar, qt)
                st = jnp.concatenate([jnp.where(mask, st[:, c * blk:(c + 1) * blk], NEG_BIG)
                                      for c in range(cols_per_group)], axis=1)
                slot = (sub * SWA_KV_HEADS + g) * 2 + par
                s_ref[slot] = st
                maxes[slot] = jnp.max(st, axis=0, keepdims=True)
    for sub in range(SWA_SUB):
        rows = slice(sub * blk, (sub + 1) * blk)
        vbt = jnp.concatenate([v_refs[sub][0], v_refs[sub + 1][0]], axis=0).T
        for g in range(SWA_KV_HEADS):
            gl = slice(g * LANES, (g + 1) * LANES)
            outs = []
            for par in range(2):
                slot = (sub * SWA_KV_HEADS + g) * 2 + par
                sink = sink_ref[2 * g + par:2 * g + par + 1, :]
                m = jnp.maximum(maxes[slot], sink)
                e = jnp.exp2(s_ref[slot] - m)
                denom = jnp.sum(e, axis=0, keepdims=True) + jnp.exp2(sink - m)
                ot = _dot(vbt[gl, :], e.astype(jnp.bfloat16))
                outs.append(ot * (1.0 / denom))
            for c in range(cols_per_group):
                both = jnp.where(top, outs[0][:, c * blk:(c + 1) * blk], outs[1][:, c * blk:(c + 1) * blk])
                col = g * cols_per_group + c
                o_ref[0, rows, col * LANES:(col + 1) * LANES] = both.T.astype(jnp.bfloat16)


def _swa_attn(q, k, v, sinks):
    B, S, qd = q.shape
    kw = k.shape[2]
    blk = WINDOW
    own = lambda b, n: (b, n, 0)
    band = [lambda b, n: (b, jnp.maximum(SWA_SUB * n - 1, 0), 0)]
    band += [functools.partial(lambda b, n, i: (b, SWA_SUB * n + i, 0), i=i) for i in range(SWA_SUB)]
    band_specs = [pl.BlockSpec((1, blk, kw), index) for index in band]
    cols_per_group = SWA_GROUP // 2
    sink_rows = (sinks * LOG2E).reshape(SWA_KV_HEADS, cols_per_group, 2).transpose(0, 2, 1)
    sink_rows = jnp.repeat(sink_rows.reshape(2 * SWA_KV_HEADS, cols_per_group), blk, axis=1)
    sink_rows = jnp.pad(sink_rows, ((0, 8 - 2 * SWA_KV_HEADS), (0, 0)))
    return pl.pallas_call(
        _swa_attn_kernel,
        out_shape=jax.ShapeDtypeStruct((B, S, qd), jnp.bfloat16),
        grid=(B, S // (SWA_SUB * blk)),
        in_specs=[_const_spec(sink_rows.shape), pl.BlockSpec((1, SWA_SUB * blk, qd), own)] + band_specs + band_specs,
        out_specs=pl.BlockSpec((1, SWA_SUB * blk, qd), own),
        scratch_shapes=[pltpu.VMEM((SWA_SUB * 2 * SWA_KV_HEADS, 2 * blk, cols_per_group * blk), jnp.float32)],
        compiler_params=pltpu.CompilerParams(
            dimension_semantics=("arbitrary", "arbitrary"), vmem_limit_bytes=VMEM_LIMIT),
        name="swa_attn",
    )(sink_rows, q, *([k] * (SWA_SUB + 1)), *([v] * (SWA_SUB + 1)))


def kernel(x, positions, fox_w_in, fox_b_f, fox_w_out, swa_w_in, swa_sinks, swa_w_out,
           norm_pre_mix, norm_post_mix, norm_pre_ffn, norm_post_ffn, mlp_w_up, mlp_w_down):
    B, S, D = x.shape
    T = B * S
    q, k, v, stats = _fox_proj(x, norm_pre_mix[0], fox_w_in[0], fox_b_f[0])
    o = _fox_flash(q, k, v, stats)
    x1 = _out_mlp(o.reshape(T, -1), x.reshape(T, D), fox_w_out[0], norm_post_mix[0], norm_pre_ffn[0],
                  mlp_w_up[0], mlp_w_down[0], norm_post_ffn[0])
    q, k, v = _swa_proj(x1, positions.reshape(T, 1), norm_pre_mix[1], swa_w_in[0])
    o = _swa_attn(q.reshape(B, S, -1), k.reshape(B, S, -1), v.reshape(B, S, -1), swa_sinks[0])
    x2 = _out_mlp(o.reshape(T, -1), x1, swa_w_out[0], norm_post_mix[1], norm_pre_ffn[1],
                  mlp_w_up[1], mlp_w_down[1], norm_post_ffn[1])
    return x2.reshape(B, S, D)
```

```python
import functools

import jax
import jax.numpy as jnp
import numpy as np
from jax import lax
from jax.experimental import pallas as pl
from jax.experimental.pallas import tpu as pltpu

D_MODEL = 1024
HEAD_DIM = 64
FOX_HEADS = 16
SWA_Q_HEADS = 16
SWA_KV_HEADS = 2
SWA_GROUP = SWA_Q_HEADS // SWA_KV_HEADS
WINDOW = 128
D_FF = 4 * D_MODEL
ROPE_THETA = 10000.0
NORM_EPS = 1e-6

LANES = 128
SUBLANES = 8
HEAD_AUG = LANES
LOG2E = 1.4426950408889634
QK_SCALE = HEAD_DIM ** -0.5
NEG_BIG = -1e30
SKIP_LOG2 = 152.0
NORM_SLACK = 1.02
DIAG_SLACK = 0.02
SHIFT_GAP_LOG2 = 100.0
VMEM_LIMIT = 56 * 1024 * 1024

DEC0 = HEAD_DIM
N_SPLIT = 3
ONES_LANE = N_SPLIT * FOX_HEADS

PROJ_TM = 512
FLASH_BQ = 512
FLASH_NSUB = 4
MLP_TM = 1024
MLP_CHUNK = 1024
SWA_SUB = 4


def _rms(x, g):
    return x * lax.rsqrt(jnp.mean(x * x, axis=-1, keepdims=True) + NORM_EPS) * g


def _dot(a, b):
    return jnp.dot(a, b, preferred_element_type=jnp.float32)


def _split3(x):
    hi = x.astype(jnp.bfloat16)
    r = x - hi.astype(jnp.float32)
    mid = r.astype(jnp.bfloat16)
    lo = (r - mid.astype(jnp.float32)).astype(jnp.bfloat16)
    return hi, mid, lo


def _const_spec(shape):
    nd = len(shape)
    return pl.BlockSpec(shape, lambda *_: (0,) * nd, pipeline_mode=pl.Buffered(1))


def _fox_proj_kernel(x_ref, g_ref, wq_ref, wk_ref, wv_ref, wf_ref, b3_ref, place_ref,
                     tri_ref, sel_ref, q_out, k_out, v_out, stats_out, carry_ref, *, tm):
    @pl.when(pl.program_id(1) == 0)
    def _():
        carry_ref[...] = jnp.zeros_like(carry_ref)

    h = _rms(x_ref[0], g_ref[...]).astype(jnp.bfloat16)

    z = _dot(h, wf_ref[...]) + b3_ref[...]
    logf = jnp.minimum(z, 0.0) - jnp.log1p(jnp.exp(-jnp.abs(z)))

    tri = tri_ref[...]
    hi, mid, lo = _split3(logf)
    c = _dot(tri, hi) + _dot(tri, mid) + _dot(tri, lo) + carry_ref[...]
    carry_ref[...] = c[tm - 1:tm, :]

    chi, cmid, clo = (p.astype(jnp.float32) for p in _split3(c * LOG2E))
    lane = lax.broadcasted_iota(jnp.int32, (tm, LANES), 1)
    parts = jnp.where(lane < FOX_HEADS, chi,
                      jnp.where(lane < 2 * FOX_HEADS, cmid,
                                jnp.where(lane < ONES_LANE, clo,
                                          jnp.where(lane == ONES_LANE, 1.0, 0.0))))
    parts = parts.astype(jnp.bfloat16)

    width = FOX_HEADS * HEAD_AUG
    q = _dot(h, wq_ref[...]) * (QK_SCALE * LOG2E)
    k = _dot(h, wk_ref[...])
    v = _dot(h, wv_ref[...])
    qn2 = _dot((q * q).astype(jnp.bfloat16), sel_ref[...])
    kn2 = _dot((k * k).astype(jnp.bfloat16), sel_ref[...])

    low = lane < HEAD_DIM
    v_tail = jnp.where(lane == HEAD_DIM, 1.0, 0.0)
    for pr in range(FOX_HEADS // 2):
        pcols = slice(pr * LANES, (pr + 1) * LANES)
        acols = slice(2 * pr * HEAD_AUG, (2 * pr + 2) * HEAD_AUG)
        q_tail = _dot(parts, place_ref[:, acols])
        k_tail = _dot(parts, place_ref[:, width + 2 * pr * HEAD_AUG:width + (2 * pr + 2) * HEAD_AUG])
        for src, tail, out in ((q[:, pcols], q_tail, q_out), (k[:, pcols], k_tail, k_out),
                               (v[:, pcols], None, v_out)):
            for par, head in enumerate((src, pltpu.roll(src, HEAD_DIM, 1))):
                fill = v_tail if tail is None else tail[:, par * HEAD_AUG:(par + 1) * HEAD_AUG]
                hcols = slice((2 * pr + par) * HEAD_AUG, (2 * pr + par + 1) * HEAD_AUG)
                out[0, :, hcols] = jnp.where(low, head, fill).astype(jnp.bfloat16)

    c2 = c * LOG2E
    stats_out[0, 0] = jnp.zeros((SUBLANES, LANES), jnp.float32)
    stats_out[0, 0, 0:1, :] = jnp.sqrt(jnp.max(qn2, axis=0, keepdims=True))
    stats_out[0, 0, 1:2, :] = jnp.sqrt(jnp.max(kn2, axis=0, keepdims=True))
    stats_out[0, 0, 2:3, :] = c2[0:1, :]
    stats_out[0, 0, 3:4, :] = c2[tm - 1:tm, :]
    diag = _dot((q * k).astype(jnp.bfloat16), sel_ref[...])
    stats_out[0, 0, 4:5, :] = jnp.min(diag, axis=0, keepdims=True)


def _fox_placement():
    width = FOX_HEADS * HEAD_AUG
    p = np.zeros((LANES, 2 * width), np.float32)
    for h in range(FOX_HEADS):
        base = h * HEAD_AUG + DEC0
        for s in range(N_SPLIT):
            p[s * FOX_HEADS + h, base + s] = 1.0
            p[ONES_LANE, base + N_SPLIT + s] = 1.0
            p[ONES_LANE, width + base + s] = 1.0
            p[s * FOX_HEADS + h, width + base + N_SPLIT + s] = -1.0
    return p


def _fox_proj(x, g, w_in, b_f):
    B, S, D = x.shape
    tm = PROJ_TM
    hd_all = FOX_HEADS * HEAD_DIM
    width = FOX_HEADS * HEAD_AUG
    wq = w_in[:, :hd_all].astype(jnp.bfloat16)
    wk = w_in[:, hd_all:2 * hd_all].astype(jnp.bfloat16)
    wv = w_in[:, 2 * hd_all:3 * hd_all].astype(jnp.bfloat16)
    wf = w_in[:, 3 * hd_all:]
    wf3 = jnp.concatenate([wf] * N_SPLIT + [jnp.zeros((D, LANES - ONES_LANE), wf.dtype)], axis=1)
    b3 = jnp.concatenate([b_f] * N_SPLIT + [jnp.zeros((LANES - ONES_LANE,), b_f.dtype)])[None, :]
    place = jnp.asarray(_fox_placement(), jnp.bfloat16)
    tri = jnp.asarray(np.tril(np.ones((tm, tm), np.float32)), jnp.bfloat16)
    sel = np.zeros((hd_all, LANES), np.float32)
    for hd in range(FOX_HEADS):
        sel[hd * HEAD_DIM:(hd + 1) * HEAD_DIM, hd] = 1.0

    out_sds = jax.ShapeDtypeStruct((B, S, width), jnp.bfloat16)
    out_spec = pl.BlockSpec((1, tm, width), lambda b, t: (b, t, 0))
    stats_sds = jax.ShapeDtypeStruct((B, S // tm, SUBLANES, LANES), jnp.float32)
    stats_spec = pl.BlockSpec((1, 1, SUBLANES, LANES), lambda b, t: (b, t, 0, 0))
    return pl.pallas_call(
        functools.partial(_fox_proj_kernel, tm=tm),
        out_shape=(out_sds, out_sds, out_sds, stats_sds),
        grid=(B, S // tm),
        in_specs=[
            pl.BlockSpec((1, tm, D), lambda b, t: (b, t, 0)),
            _const_spec((1, D)),
            _const_spec((D, hd_all)), _const_spec((D, hd_all)), _const_spec((D, hd_all)),
            _const_spec((D, LANES)), _const_spec((1, LANES)),
            _const_spec((LANES, 2 * width)),
            _const_spec((tm, tm)), _const_spec((hd_all, LANES)),
        ],
        out_specs=(out_spec, out_spec, out_spec, stats_spec),
        scratch_shapes=[pltpu.VMEM((1, LANES), jnp.float32)],
        compiler_params=pltpu.CompilerParams(
            dimension_semantics=("arbitrary", "arbitrary"), vmem_limit_bytes=VMEM_LIMIT),
        name="fox_proj",
    )(x, g[None, :], wq, wk, wv, wf3.astype(jnp.bfloat16), b3, place, tri,
      jnp.asarray(sel, jnp.bfloat16))


def _flash_transpose_v(v_ref, vt_ref, qi, bq, seq):
    @pl.when(qi == 0)
    def _():
        for c in range(seq // bq):
            rows = slice(c * bq, (c + 1) * bq)
            vt_ref[:, rows] = v_ref[0, rows, :].T


def _flash_finalize(acc_ref, o_ref, bq):
    halves = []
    for hh in range(2):
        acc = acc_ref[hh]
        halves.append(acc[0:HEAD_DIM, :] * (1.0 / acc[HEAD_DIM:HEAD_DIM + 1, :]))
    o_ref[...] = jnp.concatenate(halves, axis=0).T.astype(jnp.bfloat16)


def _per_query_block(block_fn, q_ref, o_ref, bq, nsub):
    def one(u, carry):
        rows = pl.ds(pl.multiple_of(u * bq, bq), bq)
        block_fn(nsub * pl.program_id(2) + u, q_ref.at[0, rows, :], o_ref.at[0, rows, :])
        return carry

    lax.fori_loop(0, nsub, one, 0)


def _fox_flash_kernel(first_ref, q_ref, k_ref, v_ref, o_ref, *scratch, bq, seq, nsub):
    block_fn = functools.partial(_flash_running_max_block, first_ref=first_ref, k_ref=k_ref, v_ref=v_ref,
                                 scratch=scratch, bq=bq, seq=seq)
    _per_query_block(block_fn, q_ref, o_ref, bq, nsub)


def _flash_running_max_block(qi, q_ref, o_ref, *, first_ref, k_ref, v_ref, scratch, bq, seq):
    vt_ref, qt_ref, m_ref, acc_ref, s_ref, p_ref, mb_ref = scratch
    heads = 2
    nq = seq // bq
    head0 = (pl.program_id(0) * FOX_HEADS + heads * pl.program_id(1)) * nq + qi
    j0 = jnp.minimum(first_ref[head0], first_ref[head0 + nq])
    _flash_transpose_v(v_ref, vt_ref, qi, bq, seq)
    qt_ref[...] = q_ref[...].T
    m_ref[...] = jnp.full_like(m_ref, NEG_BIG)
    acc_ref[...] = jnp.zeros_like(acc_ref)

    def scores(j, slot, masked):
        start = pl.multiple_of(j * bq, bq)
        for hh in range(heads):
            cols = slice(hh * HEAD_AUG, (hh + 1) * HEAD_AUG)
            st = _dot(k_ref[0, pl.ds(start, bq), cols], qt_ref[cols, :])
            if masked:
                key = lax.broadcasted_iota(jnp.int32, (bq, bq), 0)
                qry = lax.broadcasted_iota(jnp.int32, (bq, bq), 1)
                st = jnp.where(key - qry <= (qi - j) * bq, st, NEG_BIG)
            s_ref[2 * slot + hh] = st
            mb_ref[2 * slot + hh] = jnp.max(st, axis=0, keepdims=True)

    def consume(j, slot):
        start = pl.multiple_of(j * bq, bq)
        alpha = []
        for hh in range(heads):
            m_prev = m_ref[hh]
            m_new = jnp.maximum(m_prev, mb_ref[2 * slot + hh])
            m_ref[hh] = m_new
            p_ref[hh] = jnp.exp2(s_ref[2 * slot + hh] - m_new).astype(jnp.bfloat16)
            alpha.append(jnp.exp2(m_prev - m_new))
        for hh in range(heads):
            cols = slice(hh * HEAD_AUG, (hh + 1) * HEAD_AUG)
            vt = vt_ref[cols, pl.ds(start, bq)]
            acc_ref[hh] = alpha[hh] * acc_ref[hh] + _dot(vt, p_ref[hh])

    def half_step(j, slot, masked):
        scores(j + 1, 1 - slot, masked)
        consume(j, slot)

    span = qi - j0
    pairs = jnp.maximum(span - 1, 0) // 2
    rest = span - 2 * pairs
    scores(j0, 0, True)

    def body(t, carry):
        half_step(j0 + 2 * t, 0, False)
        half_step(j0 + 2 * t + 1, 1, False)
        return carry

    lax.fori_loop(0, pairs, body, 0)

    @pl.when(rest >= 1)
    def _():
        half_step(j0 + 2 * pairs, 0, True)

    @pl.when(rest == 2)
    def _():
        half_step(j0 + 2 * pairs + 1, 1, True)

    @pl.when(rest == 1)
    def _():
        consume(qi, 1)

    @pl.when(rest != 1)
    def _():
        consume(qi, 0)

    _flash_finalize(acc_ref, o_ref, bq)


def _fox_flash_fixed_kernel(first_ref, kall_ref, q_ref, k_ref, v_ref, o_ref, *scratch, bq, seq, nsub):
    block_fn = functools.partial(_flash_fixed_shift_block, first_ref=first_ref, kall_ref=kall_ref, k_ref=k_ref,
                                 v_ref=v_ref, scratch=scratch, bq=bq, seq=seq)
    _per_query_block(block_fn, q_ref, o_ref, bq, nsub)


def _flash_fixed_shift_block(qi, q_ref, o_ref, *, first_ref, kall_ref, k_ref, v_ref, scratch, bq, seq):
    vt_ref, qt_ref, r_ref, acc_ref, p_ref = scratch
    b, hp = pl.program_id(0), pl.program_id(1)
    heads = 2
    nq = seq // bq
    head0 = (b * FOX_HEADS + heads * hp) * nq + qi
    first = (first_ref[head0], first_ref[head0 + nq])
    j0 = jnp.maximum(first[0], first[1])
    _flash_transpose_v(v_ref, vt_ref, qi, bq, seq)
    qt_ref[...] = q_ref[...].T
    acc_ref[...] = jnp.zeros_like(acc_ref)
    for hh in range(heads):
        qf = qt_ref[hh * HEAD_AUG:hh * HEAD_AUG + HEAD_DIM, :].astype(jnp.float32)
        qnorm = jnp.sqrt(jnp.sum(qf * qf, axis=0, keepdims=True))
        r_ref[hh] = qnorm * (NORM_SLACK * kall_ref[b * FOX_HEADS + heads * hp + hh])

    def blocks(j, nblk, masked, which=(0, 1)):
        start = pl.multiple_of(j * bq, bq)
        for hh in which:
            cols = slice(hh * HEAD_AUG, (hh + 1) * HEAD_AUG)
            for u in range(nblk):
                st = _dot(k_ref[0, pl.ds(start + u * bq, bq), cols], qt_ref[cols, :])
                if masked and u == nblk - 1:
                    key = lax.broadcasted_iota(jnp.int32, (bq, bq), 0)
                    qry = lax.broadcasted_iota(jnp.int32, (bq, bq), 1)
                    st = jnp.where(key <= qry, st, NEG_BIG)
                p_ref[hh, u * bq:(u + 1) * bq, :] = jnp.exp2(st - r_ref[hh]).astype(jnp.bfloat16)
        for hh in which:
            cols = slice(hh * HEAD_AUG, (hh + 1) * HEAD_AUG)
            vt = vt_ref[cols, pl.ds(start, nblk * bq)]
            acc_ref[hh] += _dot(vt, p_ref[hh, 0:nblk * bq, :])

    for hh in range(heads):
        @pl.when(first[hh] < j0)
        def _(hh=hh):
            lo = first[hh]
            count = j0 - lo

            def body1(t, carry):
                blocks(lo + 4 * t, 4, False, (hh,))
                return carry

            lax.fori_loop(0, count // 4, body1, 0)

            @pl.when(count % 4 >= 2)
            def _():
                blocks(lo + 4 * (count // 4), 2, False, (hh,))

            @pl.when(count % 2 == 1)
            def _():
                blocks(j0 - 1, 1, False, (hh,))

    span = qi - j0
    quads = span // 4
    rest = span - 4 * quads

    def body(t, carry):
        blocks(j0 + 4 * t, 4, False)
        return carry

    lax.fori_loop(0, quads, body, 0)

    @pl.when(rest >= 2)
    def _():
        blocks(j0 + 4 * quads, 2, False)

    @pl.when(rest % 2 == 1)
    def _():
        blocks(qi - 1, 2, True)

    @pl.when(rest % 2 == 0)
    def _():
        blocks(qi, 1, True)

    _flash_finalize(acc_ref, o_ref, bq)


def _first_key_block(stats):
    qmax, kmax = stats[:, :, 0, :FOX_HEADS], stats[:, :, 1, :FOX_HEADS]
    c_first, c_last = stats[:, :, 2, :FOX_HEADS], stats[:, :, 3, :FOX_HEADS]
    diag_min = stats[:, :, 4, :FOX_HEADS]
    kall = jnp.max(kmax, axis=1, keepdims=True)
    thr = c_first + SKIP_LOG2 + (NORM_SLACK + DIAG_SLACK) * qmax * kall - diag_min
    skippable = c_last[:, None, :, :] > thr[:, :, None, :]
    first = jnp.sum(skippable.astype(jnp.int32), axis=2)
    nq = stats.shape[1]
    first = jnp.minimum(first, jnp.arange(nq, dtype=jnp.int32)[None, :, None])
    return first.transpose(0, 2, 1).reshape(-1)


def _fox_flash(q, k, v, stats):
    B, S, width = q.shape
    bq = FLASH_BQ
    pair = 2 * HEAD_AUG
    first = _first_key_block(stats)
    qmax, kmax, diag_min = (stats[:, :, r, :FOX_HEADS] for r in (0, 1, 4))
    kall = jnp.max(kmax, axis=1)
    gap = NORM_SLACK * qmax * kall[:, None, :] - diag_min
    fixed_shift_ok = jnp.max(gap) < SHIFT_GAP_LOG2

    blk = lambda shape, index, **kw: pl.BlockSpec(shape, lambda b, hp, i, first: index(b, hp, i), **kw)
    tensor_specs = [
        blk((1, FLASH_NSUB * bq, pair), lambda b, hp, i: (b, i, hp)),
        blk((1, S, pair), lambda b, hp, i: (b, 0, hp)),
        blk((1, S, pair), lambda b, hp, i: (b, 0, hp), pipeline_mode=pl.Buffered(1)),
    ]
    common = dict(
        out_shape=jax.ShapeDtypeStruct((B, S, FOX_HEADS * HEAD_DIM), jnp.bfloat16),
        compiler_params=pltpu.CompilerParams(
            dimension_semantics=("arbitrary", "arbitrary", "arbitrary"),
            vmem_limit_bytes=VMEM_LIMIT))
    grid = (B, FOX_HEADS // 2, S // (FLASH_NSUB * bq))
    out_spec = blk((1, FLASH_NSUB * bq, 2 * HEAD_DIM), lambda b, hp, i: (b, i, hp))
    vt_qt = [pltpu.VMEM((pair, S), jnp.bfloat16), pltpu.VMEM((pair, bq), jnp.bfloat16)]

    def fixed_shift():
        return pl.pallas_call(
            functools.partial(_fox_flash_fixed_kernel, bq=bq, seq=S, nsub=FLASH_NSUB),
            grid_spec=pltpu.PrefetchScalarGridSpec(
                num_scalar_prefetch=1, grid=grid,
                in_specs=[pl.BlockSpec(memory_space=pltpu.SMEM)] + tensor_specs,
                out_specs=out_spec,
                scratch_shapes=vt_qt + [pltpu.VMEM((2, 1, bq), jnp.float32),
                                        pltpu.VMEM((2, HEAD_AUG, bq), jnp.float32),
                                        pltpu.VMEM((2, 4 * bq, bq), jnp.bfloat16)]),
            name="fox_flash_fixed", **common)(first, kall.reshape(-1), q, k, v)

    def running_max():
        return pl.pallas_call(
            functools.partial(_fox_flash_kernel, bq=bq, seq=S, nsub=FLASH_NSUB),
            grid_spec=pltpu.PrefetchScalarGridSpec(
                num_scalar_prefetch=1, grid=grid, in_specs=tensor_specs, out_specs=out_spec,
                scratch_shapes=vt_qt + [pltpu.VMEM((2, 1, bq), jnp.float32),
                                        pltpu.VMEM((2, HEAD_AUG, bq), jnp.float32),
                                        pltpu.VMEM((4, bq, bq), jnp.float32),
                                        pltpu.VMEM((2, bq, bq), jnp.bfloat16),
                                        pltpu.VMEM((4, 1, bq), jnp.float32)]),
            name="fox_flash", **common)(first, q, k, v)

    return lax.cond(fixed_shift_ok, fixed_shift, running_max)


def _out_mlp_kernel(o_ref, x_ref, wo_ref, g_post_ref, g_pre_ref, wu_ref, wd_ref, g_ffn_ref, y_ref,
                    x1_ref, h_ref, *, sub):
    n_sub = x_ref.shape[0] // sub
    for s in range(n_sub):
        rows = slice(s * sub, (s + 1) * sub)
        a = _dot(o_ref[rows, :], wo_ref[...])
        x1 = x_ref[rows, :] + _rms(a, g_post_ref[...])
        x1_ref[rows, :] = x1
        h_ref[rows, :] = _rms(x1, g_pre_ref[...]).astype(jnp.bfloat16)
    for s in range(n_sub):
        rows = slice(s * sub, (s + 1) * sub)
        acc = jnp.zeros((sub, x_ref.shape[1]), jnp.float32)
        for c in range(D_FF // MLP_CHUNK):
            cols = slice(c * MLP_CHUNK, (c + 1) * MLP_CHUNK)
            u = jnp.maximum(_dot(h_ref[rows, :], wu_ref[:, cols]), 0.0)
            acc = acc + _dot((u * u).astype(jnp.bfloat16), wd_ref[cols, :])
        y_ref[rows, :] = x1_ref[rows, :] + _rms(acc, g_ffn_ref[...])


def _out_mlp(o, x, w_out, g_post, g_pre, w_up, w_down, g_ffn):
    T, D = x.shape
    tm = MLP_TM
    row = lambda width: pl.BlockSpec((tm, width), lambda t: (t, 0))
    return pl.pallas_call(
        functools.partial(_out_mlp_kernel, sub=PROJ_TM),
        out_shape=jax.ShapeDtypeStruct((T, D), jnp.float32),
        grid=(T // tm,),
        in_specs=[row(o.shape[1]), row(D),
                  _const_spec(w_out.shape), _const_spec((1, D)), _const_spec((1, D)),
                  _const_spec(w_up.shape), _const_spec(w_down.shape), _const_spec((1, D))],
        out_specs=row(D),
        scratch_shapes=[pltpu.VMEM((tm, D), jnp.float32), pltpu.VMEM((tm, D), jnp.bfloat16)],
        compiler_params=pltpu.CompilerParams(
            dimension_semantics=("arbitrary",), vmem_limit_bytes=VMEM_LIMIT),
        name="out_mlp",
    )(o, x, w_out.astype(jnp.bfloat16), g_post[None, :], g_pre[None, :],
      w_up.astype(jnp.bfloat16), w_down.astype(jnp.bfloat16), g_ffn[None, :])


def _swa_proj_kernel(x_ref, pos_ref, g_ref, wq_ref, wk_ref, wv_ref, freq_ref, lo_ref, hi_ref,
                     q_out, k_out, v_out):
    h = _rms(x_ref[...], g_ref[...]).astype(jnp.bfloat16)
    ang = pos_ref[...].astype(jnp.float32) * freq_ref[...]
    cos = jnp.cos(ang)
    sin = jnp.sin(ang)
    sin_lo = sin * lo_ref[...]
    sin_hi = sin * hi_ref[...]
    half = HEAD_DIM // 2

    def rope(t):
        return t * cos + pltpu.roll(t, LANES - half, 1) * sin_lo + pltpu.roll(t, half, 1) * sin_hi

    q = _dot(h, wq_ref[...])
    for c in range(q_out.shape[1] // LANES):
        cols = slice(c * LANES, (c + 1) * LANES)
        q_out[:, cols] = (rope(q[:, cols]) * (QK_SCALE * LOG2E)).astype(jnp.bfloat16)
    k = _dot(h, wk_ref[...])
    for c in range(k_out.shape[1] // LANES):
        cols = slice(c * LANES, (c + 1) * LANES)
        k_out[:, cols] = rope(k[:, cols]).astype(jnp.bfloat16)
    v_out[...] = _dot(h, wv_ref[...]).astype(jnp.bfloat16)


def _dup_heads(w, heads):
    d = w.shape[0]
    w = w.reshape(d, heads, 1, HEAD_DIM)
    return jnp.broadcast_to(w, (d, heads, 2, HEAD_DIM)).reshape(d, heads * 2 * HEAD_DIM)


def _swa_proj(x, pos, g, w_in):
    T, D = x.shape
    tm = PROJ_TM
    qd = SWA_Q_HEADS * HEAD_DIM
    kvd = SWA_KV_HEADS * HEAD_DIM
    kw = SWA_KV_HEADS * 2 * HEAD_DIM
    bf = lambda a: a.astype(jnp.bfloat16)
    wq = w_in[:, :qd]
    wk = w_in[:, qd:qd + kvd]
    wv = w_in[:, qd + kvd:]
    half = HEAD_DIM // 2
    inv_freq = 1.0 / (ROPE_THETA ** (jnp.arange(0, HEAD_DIM, 2, dtype=jnp.float32) / HEAD_DIM))
    freq = jnp.tile(inv_freq, LANES // half)[None, :]
    first_half = (np.arange(LANES) % HEAD_DIM) < half
    lo = np.where(first_half, -1.0, 0.0).astype(np.float32)[None, :]
    hi = np.where(first_half, 0.0, 1.0).astype(np.float32)[None, :]
    row = lambda width: pl.BlockSpec((tm, width), lambda t: (t, 0))
    return pl.pallas_call(
        _swa_proj_kernel,
        out_shape=(jax.ShapeDtypeStruct((T, qd), jnp.bfloat16),
                   jax.ShapeDtypeStruct((T, kw), jnp.bfloat16),
                   jax.ShapeDtypeStruct((T, kw), jnp.bfloat16)),
        grid=(T // tm,),
        in_specs=[row(D), row(1), _const_spec((1, D)),
                  _const_spec((D, qd)), _const_spec((D, kw)), _const_spec((D, kw)),
                  _const_spec((1, LANES)), _const_spec((1, LANES)), _const_spec((1, LANES))],
        out_specs=(row(qd), row(kw), row(kw)),
        compiler_params=pltpu.CompilerParams(
            dimension_semantics=("arbitrary",), vmem_limit_bytes=VMEM_LIMIT),
        name="swa_proj",
    )(x, pos, g[None, :], bf(wq), bf(_dup_heads(wk, SWA_KV_HEADS)), bf(_dup_heads(wv, SWA_KV_HEADS)),
      freq, jnp.asarray(lo), jnp.asarray(hi))


def _swa_attn_kernel(sink_ref, q_ref, *refs):
    k_refs, v_refs = refs[:SWA_SUB + 1], refs[SWA_SUB + 1:2 * SWA_SUB + 2]
    o_ref, s_ref = refs[2 * SWA_SUB + 2:]
    n = pl.program_id(1)
    blk = WINDOW
    cols_per_group = SWA_GROUP // 2
    key = lax.broadcasted_iota(jnp.int32, (2 * blk, blk), 0)
    qry = lax.broadcasted_iota(jnp.int32, (2 * blk, blk), 1)
    diff = qry + blk - key
    in_window = (diff >= 0) & (diff < WINDOW)
    klane = lax.broadcasted_iota(jnp.int32, (2 * blk, LANES), 1)
    top = lax.broadcasted_iota(jnp.int32, (LANES, blk), 0) < HEAD_DIM
    maxes = {}
    for sub in range(SWA_SUB):
        rows = slice(sub * blk, (sub + 1) * blk)
        kb = jnp.concatenate([k_refs[sub][0], k_refs[sub + 1][0]], axis=0).astype(jnp.float32)
        mask = in_window & (key >= jnp.where(n > 0, 0, blk)) if sub == 0 else in_window
        for g in range(SWA_KV_HEADS):
            gl = slice(g * LANES, (g + 1) * LANES)
            qt = jnp.concatenate([q_ref[0, rows, c * LANES:(c + 1) * LANES].T
                                  for c in range(g * cols_per_group, (g + 1) * cols_per_group)], axis=1)
            for par in range(2):
                keep = (klane < HEAD_DIM) if par == 0 else (klane >= HEAD_DIM)
                kpar = jnp.where(keep, kb[:, gl], 0.0).astype(jnp.bfloat16)
                st = _dot(kpar, qt)
                st = jnp.concatenate([jnp.where(mask, st[:, c * blk:(c + 1) * blk], NEG_BIG)
                                      for c in range(cols_per_group)], axis=1)
                slot = (sub * SWA_KV_HEADS + g) * 2 + par
                s_ref[slot] = st
                maxes[slot] = jnp.max(st, axis=0, keepdims=True)
    for sub in range(SWA_SUB):
        rows = slice(sub * blk, (sub + 1) * blk)
        vbt = jnp.concatenate([v_refs[sub][0], v_refs[sub + 1][0]], axis=0).T
        for g in range(SWA_KV_HEADS):
            gl = slice(g * LANES, (g + 1) * LANES)
            outs = []
            for par in range(2):
                slot = (sub * SWA_KV_HEADS + g) * 2 + par
                sink = sink_ref[2 * g + par:2 * g + par + 1, :]
                m = jnp.maximum(maxes[slot], sink)
                e = jnp.exp2(s_ref[slot] - m)
                denom = jnp.sum(e, axis=0, keepdims=True) + jnp.exp2(sink - m)
                ot = _dot(vbt[gl, :], e.astype(jnp.bfloat16))
                outs.append(ot * (1.0 / denom))
            for c in range(cols_per_group):
                both = jnp.where(top, outs[0][:, c * blk:(c + 1) * blk], outs[1][:, c * blk:(c + 1) * blk])
                col = g * cols_per_group + c
                o_ref[0, rows, col * LANES:(col + 1) * LANES] = both.T.astype(jnp.bfloat16)


def _swa_attn(q, k, v, sinks):
    B, S, qd = q.shape
    kw = k.shape[2]
    blk = WINDOW
    own = lambda b, n: (b, n, 0)
    band = [lambda b, n: (b, jnp.maximum(SWA_SUB * n - 1, 0), 0)]
    band += [functools.partial(lambda b, n, i: (b, SWA_SUB * n + i, 0), i=i) for i in range(SWA_SUB)]
    band_specs = [pl.BlockSpec((1, blk, kw), index) for index in band]
    cols_per_group = SWA_GROUP // 2
    sink_rows = (sinks * LOG2E).reshape(SWA_KV_HEADS, cols_per_group, 2).transpose(0, 2, 1)
    sink_rows = jnp.repeat(sink_rows.reshape(2 * SWA_KV_HEADS, cols_per_group), blk, axis=1)
    sink_rows = jnp.pad(sink_rows, ((0, SUBLANES - 2 * SWA_KV_HEADS), (0, 0)))
    return pl.pallas_call(
        _swa_attn_kernel,
        out_shape=jax.ShapeDtypeStruct((B, S, qd), jnp.bfloat16),
        grid=(B, S // (SWA_SUB * blk)),
        in_specs=[_const_spec(sink_rows.shape), pl.BlockSpec((1, SWA_SUB * blk, qd), own)] + band_specs + band_specs,
        out_specs=pl.BlockSpec((1, SWA_SUB * blk, qd), own),
        scratch_shapes=[pltpu.VMEM((SWA_SUB * 2 * SWA_KV_HEADS, 2 * blk, cols_per_group * blk), jnp.float32)],
        compiler_params=pltpu.CompilerParams(
            dimension_semantics=("arbitrary", "arbitrary"), vmem_limit_bytes=VMEM_LIMIT),
        name="swa_attn",
    )(sink_rows, q, *([k] * (SWA_SUB + 1)), *([v] * (SWA_SUB + 1)))


def kernel(x, positions, fox_w_in, fox_b_f, fox_w_out, swa_w_in, swa_sinks, swa_w_out,
           norm_pre_mix, norm_post_mix, norm_pre_ffn, norm_post_ffn, mlp_w_up, mlp_w_down):
    B, S, D = x.shape
    T = B * S
    q, k, v, stats = _fox_proj(x, norm_pre_mix[0], fox_w_in[0], fox_b_f[0])
    o = _fox_flash(q, k, v, stats)
    x1 = _out_mlp(o.reshape(T, -1), x.reshape(T, D), fox_w_out[0], norm_post_mix[0], norm_pre_ffn[0],
                  mlp_w_up[0], mlp_w_down[0], norm_post_ffn[0])
    q, k, v = _swa_proj(x1, positions.reshape(T, 1), norm_pre_mix[1], swa_w_in[0])
    o = _swa_attn(q.reshape(B, S, -1), k.reshape(B, S, -1), v.reshape(B, S, -1), swa_sinks[0])
    x2 = _out_mlp(o.reshape(T, -1), x1, swa_w_out[0], norm_post_mix[1], norm_pre_ffn[1],
                  mlp_w_up[1], mlp_w_down[1], norm_post_ffn[1])
    return x2.reshape(B, S, D)
```

```python
import functools

import jax
import jax.numpy as jnp
import numpy as np
from jax import lax
from jax.experimental import pallas as pl
from jax.experimental.pallas import tpu as pltpu

D_MODEL = 1024
HEAD_DIM = 64
FOX_HEADS = 16
SWA_Q_HEADS = 16
SWA_KV_HEADS = 2
SWA_GROUP = SWA_Q_HEADS // SWA_KV_HEADS
WINDOW = 128
D_FF = 4 * D_MODEL
ROPE_THETA = 10000.0
NORM_EPS = 1e-6

LANES = 128
SUBLANES = 8
HEAD_AUG = LANES
LOG2E = 1.4426950408889634
QK_SCALE = HEAD_DIM ** -0.5
NEG_BIG = -1e30
SKIP_LOG2 = 152.0
NORM_SLACK = 1.02
DIAG_SLACK = 0.02
SHIFT_GAP_LOG2 = 100.0
VMEM_LIMIT = 56 * 1024 * 1024

DEC0 = HEAD_DIM
N_SPLIT = 3
ONES_LANE = N_SPLIT * FOX_HEADS

PROJ_TM = 512
FLASH_BQ = 512
FLASH_NSUB = 4
MLP_TM = 1024
MLP_CHUNK = 1024
SWA_SUB = 4


def _rms(x, g):
    return x * lax.rsqrt(jnp.mean(x * x, axis=-1, keepdims=True) + NORM_EPS) * g


def _dot(a, b):
    return jnp.dot(a, b, preferred_element_type=jnp.float32)


def _split3(x):
    hi = x.astype(jnp.bfloat16)
    r = x - hi.astype(jnp.float32)
    mid = r.astype(jnp.bfloat16)
    lo = (r - mid.astype(jnp.float32)).astype(jnp.bfloat16)
    return hi, mid, lo


def _const_spec(shape):
    nd = len(shape)
    return pl.BlockSpec(shape, lambda *_: (0,) * nd, pipeline_mode=pl.Buffered(1))


def _fox_proj_kernel(x_ref, g_ref, wq_ref, wk_ref, wv_ref, wf_ref, b3_ref, place_ref,
                     tri_ref, sel_ref, q_out, k_out, v_out, stats_out, carry_ref, *, tm):
    @pl.when(pl.program_id(1) == 0)
    def _():
        carry_ref[...] = jnp.zeros_like(carry_ref)

    h = _rms(x_ref[0], g_ref[...]).astype(jnp.bfloat16)

    z = _dot(h, wf_ref[...]) + b3_ref[...]
    logf = jnp.minimum(z, 0.0) - jnp.log1p(jnp.exp(-jnp.abs(z)))

    tri = tri_ref[...]
    hi, mid, lo = _split3(logf)
    c = _dot(tri, hi) + _dot(tri, mid) + _dot(tri, lo) + carry_ref[...]
    carry_ref[...] = c[tm - 1:tm, :]

    chi, cmid, clo = (p.astype(jnp.float32) for p in _split3(c * LOG2E))
    lane = lax.broadcasted_iota(jnp.int32, (tm, LANES), 1)
    parts = jnp.where(lane < FOX_HEADS, chi,
                      jnp.where(lane < 2 * FOX_HEADS, cmid,
                                jnp.where(lane < ONES_LANE, clo,
                                          jnp.where(lane == ONES_LANE, 1.0, 0.0))))
    parts = parts.astype(jnp.bfloat16)

    width = FOX_HEADS * HEAD_AUG
    q = _dot(h, wq_ref[...]) * (QK_SCALE * LOG2E)
    k = _dot(h, wk_ref[...])
    v = _dot(h, wv_ref[...])
    qn2 = _dot((q * q).astype(jnp.bfloat16), sel_ref[...])
    kn2 = _dot((k * k).astype(jnp.bfloat16), sel_ref[...])

    low = lane < HEAD_DIM
    v_tail = jnp.where(lane == HEAD_DIM, 1.0, 0.0)
    for pr in range(FOX_HEADS // 2):
        pcols = slice(pr * LANES, (pr + 1) * LANES)
        acols = slice(2 * pr * HEAD_AUG, (2 * pr + 2) * HEAD_AUG)
        q_tail = _dot(parts, place_ref[:, acols])
        k_tail = _dot(parts, place_ref[:, width + 2 * pr * HEAD_AUG:width + (2 * pr + 2) * HEAD_AUG])
        for src, tail, out in ((q[:, pcols], q_tail, q_out), (k[:, pcols], k_tail, k_out),
                               (v[:, pcols], None, v_out)):
            for par, head in enumerate((src, pltpu.roll(src, HEAD_DIM, 1))):
                fill = v_tail if tail is None else tail[:, par * HEAD_AUG:(par + 1) * HEAD_AUG]
                hcols = slice((2 * pr + par) * HEAD_AUG, (2 * pr + par + 1) * HEAD_AUG)
                out[0, :, hcols] = jnp.where(low, head, fill).astype(jnp.bfloat16)

    c2 = c * LOG2E
    stats_out[0, 0] = jnp.zeros((SUBLANES, LANES), jnp.float32)
    stats_out[0, 0, 0:1, :] = jnp.sqrt(jnp.max(qn2, axis=0, keepdims=True))
    stats_out[0, 0, 1:2, :] = jnp.sqrt(jnp.max(kn2, axis=0, keepdims=True))
    stats_out[0, 0, 2:3, :] = c2[0:1, :]
    stats_out[0, 0, 3:4, :] = c2[tm - 1:tm, :]
    diag = _dot((q * k).astype(jnp.bfloat16), sel_ref[...])
    stats_out[0, 0, 4:5, :] = jnp.min(diag, axis=0, keepdims=True)


def _fox_placement():
    width = FOX_HEADS * HEAD_AUG
    p = np.zeros((LANES, 2 * width), np.float32)
    for h in range(FOX_HEADS):
        base = h * HEAD_AUG + DEC0
        for s in range(N_SPLIT):
            p[s * FOX_HEADS + h, base + s] = 1.0
            p[ONES_LANE, base + N_SPLIT + s] = 1.0
            p[ONES_LANE, width + base + s] = 1.0
            p[s * FOX_HEADS + h, width + base + N_SPLIT + s] = -1.0
    return p


def _fox_proj(x, g, w_in, b_f):
    B, S, D = x.shape
    tm = PROJ_TM
    hd_all = FOX_HEADS * HEAD_DIM
    width = FOX_HEADS * HEAD_AUG
    wq = w_in[:, :hd_all].astype(jnp.bfloat16)
    wk = w_in[:, hd_all:2 * hd_all].astype(jnp.bfloat16)
    wv = w_in[:, 2 * hd_all:3 * hd_all].astype(jnp.bfloat16)
    wf = w_in[:, 3 * hd_all:]
    wf3 = jnp.concatenate([wf] * N_SPLIT + [jnp.zeros((D, LANES - ONES_LANE), wf.dtype)], axis=1)
    b3 = jnp.concatenate([b_f] * N_SPLIT + [jnp.zeros((LANES - ONES_LANE,), b_f.dtype)])[None, :]
    place = jnp.asarray(_fox_placement(), jnp.bfloat16)
    tri = jnp.asarray(np.tril(np.ones((tm, tm), np.float32)), jnp.bfloat16)
    sel = np.zeros((hd_all, LANES), np.float32)
    for hd in range(FOX_HEADS):
        sel[hd * HEAD_DIM:(hd + 1) * HEAD_DIM, hd] = 1.0

    out_sds = jax.ShapeDtypeStruct((B, S, width), jnp.bfloat16)
    out_spec = pl.BlockSpec((1, tm, width), lambda b, t: (b, t, 0))
    stats_sds = jax.ShapeDtypeStruct((B, S // tm, SUBLANES, LANES), jnp.float32)
    stats_spec = pl.BlockSpec((1, 1, SUBLANES, LANES), lambda b, t: (b, t, 0, 0))
    return pl.pallas_call(
        functools.partial(_fox_proj_kernel, tm=tm),
        out_shape=(out_sds, out_sds, out_sds, stats_sds),
        grid=(B, S // tm),
        in_specs=[
            pl.BlockSpec((1, tm, D), lambda b, t: (b, t, 0)),
            _const_spec((1, D)),
            _const_spec((D, hd_all)), _const_spec((D, hd_all)), _const_spec((D, hd_all)),
            _const_spec((D, LANES)), _const_spec((1, LANES)),
            _const_spec((LANES, 2 * width)),
            _const_spec((tm, tm)), _const_spec((hd_all, LANES)),
        ],
        out_specs=(out_spec, out_spec, out_spec, stats_spec),
        scratch_shapes=[pltpu.VMEM((1, LANES), jnp.float32)],
        compiler_params=pltpu.CompilerParams(
            dimension_semantics=("arbitrary", "arbitrary"), vmem_limit_bytes=VMEM_LIMIT),
        name="fox_proj",
    )(x, g[None, :], wq, wk, wv, wf3.astype(jnp.bfloat16), b3, place, tri,
      jnp.asarray(sel, jnp.bfloat16))


def _flash_transpose_v(v_ref, vt_ref, qi, bq, seq):
    @pl.when(qi == 0)
    def _():
        for c in range(seq // bq):
            rows = slice(c * bq, (c + 1) * bq)
            vt_ref[:, rows] = v_ref[0, rows, :].T


def _flash_finalize(acc_ref, o_ref, bq):
    halves = []
    for hh in range(2):
        acc = acc_ref[hh]
        halves.append(acc[0:HEAD_DIM, :] * (1.0 / acc[HEAD_DIM:HEAD_DIM + 1, :]))
    o_ref[...] = jnp.concatenate(halves, axis=0).T.astype(jnp.bfloat16)


def _per_query_block(block_fn, q_ref, o_ref, bq, nsub):
    def one(u, carry):
        rows = pl.ds(pl.multiple_of(u * bq, bq), bq)
        block_fn(nsub * pl.program_id(2) + u, q_ref.at[0, rows, :], o_ref.at[0, rows, :])
        return carry

    lax.fori_loop(0, nsub, one, 0)


def _fox_flash_kernel(first_ref, q_ref, k_ref, v_ref, o_ref, *scratch, bq, seq, nsub):
    block_fn = functools.partial(_flash_running_max_block, first_ref=first_ref, k_ref=k_ref, v_ref=v_ref,
                                 scratch=scratch, bq=bq, seq=seq)
    _per_query_block(block_fn, q_ref, o_ref, bq, nsub)


def _flash_running_max_block(qi, q_ref, o_ref, *, first_ref, k_ref, v_ref, scratch, bq, seq):
    vt_ref, qt_ref, m_ref, acc_ref, s_ref, p_ref, mb_ref = scratch
    heads = 2
    nq = seq // bq
    head0 = (pl.program_id(0) * FOX_HEADS + heads * pl.program_id(1)) * nq + qi
    j0 = jnp.minimum(first_ref[head0], first_ref[head0 + nq])
    _flash_transpose_v(v_ref, vt_ref, qi, bq, seq)
    qt_ref[...] = q_ref[...].T
    m_ref[...] = jnp.full_like(m_ref, NEG_BIG)
    acc_ref[...] = jnp.zeros_like(acc_ref)

    def scores(j, slot, masked):
        start = pl.multiple_of(j * bq, bq)
        for hh in range(heads):
            cols = slice(hh * HEAD_AUG, (hh + 1) * HEAD_AUG)
            st = _dot(k_ref[0, pl.ds(start, bq), cols], qt_ref[cols, :])
            if masked:
                key = lax.broadcasted_iota(jnp.int32, (bq, bq), 0)
                qry = lax.broadcasted_iota(jnp.int32, (bq, bq), 1)
                st = jnp.where(key - qry <= (qi - j) * bq, st, NEG_BIG)
            s_ref[2 * slot + hh] = st
            mb_ref[2 * slot + hh] = jnp.max(st, axis=0, keepdims=True)

    def consume(j, slot):
        start = pl.multiple_of(j * bq, bq)
        alpha = []
        for hh in range(heads):
            m_prev = m_ref[hh]
            m_new = jnp.maximum(m_prev, mb_ref[2 * slot + hh])
            m_ref[hh] = m_new
            p_ref[hh] = jnp.exp2(s_ref[2 * slot + hh] - m_new).astype(jnp.bfloat16)
            alpha.append(jnp.exp2(m_prev - m_new))
        for hh in range(heads):
            cols = slice(hh * HEAD_AUG, (hh + 1) * HEAD_AUG)
            vt = vt_ref[cols, pl.ds(start, bq)]
            acc_ref[hh] = alpha[hh] * acc_ref[hh] + _dot(vt, p_ref[hh])

    def half_step(j, slot, masked):
        scores(j + 1, 1 - slot, masked)
        consume(j, slot)

    span = qi - j0
    pairs = jnp.maximum(span - 1, 0) // 2
    rest = span - 2 * pairs
    scores(j0, 0, True)

    def body(t, carry):
        half_step(j0 + 2 * t, 0, False)
        half_step(j0 + 2 * t + 1, 1, False)
        return carry

    lax.fori_loop(0, pairs, body, 0)

    @pl.when(rest >= 1)
    def _():
        half_step(j0 + 2 * pairs, 0, True)

    @pl.when(rest == 2)
    def _():
        half_step(j0 + 2 * pairs + 1, 1, True)

    @pl.when(rest == 1)
    def _():
        consume(qi, 1)

    @pl.when(rest != 1)
    def _():
        consume(qi, 0)

    _flash_finalize(acc_ref, o_ref, bq)


def _fox_flash_fixed_kernel(first_ref, kall_ref, q_ref, k_ref, v_ref, o_ref, *scratch, bq, seq, nsub):
    block_fn = functools.partial(_flash_fixed_shift_block, first_ref=first_ref, kall_ref=kall_ref, k_ref=k_ref,
                                 v_ref=v_ref, scratch=scratch, bq=bq, seq=seq)
    _per_query_block(block_fn, q_ref, o_ref, bq, nsub)


def _flash_fixed_shift_block(qi, q_ref, o_ref, *, first_ref, kall_ref, k_ref, v_ref, scratch, bq, seq):
    vt_ref, qt_ref, r_ref, acc_ref, p_ref = scratch
    b, hp = pl.program_id(0), pl.program_id(1)
    heads = 2
    nq = seq // bq
    head0 = (b * FOX_HEADS + heads * hp) * nq + qi
    first = (first_ref[head0], first_ref[head0 + nq])
    j0 = jnp.maximum(first[0], first[1])
    _flash_transpose_v(v_ref, vt_ref, qi, bq, seq)
    qt_ref[...] = q_ref[...].T
    acc_ref[...] = jnp.zeros_like(acc_ref)
    for hh in range(heads):
        qf = qt_ref[hh * HEAD_AUG:hh * HEAD_AUG + HEAD_DIM, :].astype(jnp.float32)
        qnorm = jnp.sqrt(jnp.sum(qf * qf, axis=0, keepdims=True))
        r_ref[hh] = qnorm * (NORM_SLACK * kall_ref[b * FOX_HEADS + heads * hp + hh])

    def blocks(j, nblk, masked, which=(0, 1)):
        start = pl.multiple_of(j * bq, bq)
        for hh in which:
            cols = slice(hh * HEAD_AUG, (hh + 1) * HEAD_AUG)
            for u in range(nblk):
                st = _dot(k_ref[0, pl.ds(start + u * bq, bq), cols], qt_ref[cols, :])
                if masked and u == nblk - 1:
                    key = lax.broadcasted_iota(jnp.int32, (bq, bq), 0)
                    qry = lax.broadcasted_iota(jnp.int32, (bq, bq), 1)
                    st = jnp.where(key <= qry, st, NEG_BIG)
                p_ref[hh, u * bq:(u + 1) * bq, :] = jnp.exp2(st - r_ref[hh]).astype(jnp.bfloat16)
        for hh in which:
            cols = slice(hh * HEAD_AUG, (hh + 1) * HEAD_AUG)
            vt = vt_ref[cols, pl.ds(start, nblk * bq)]
            acc_ref[hh] += _dot(vt, p_ref[hh, 0:nblk * bq, :])

    for hh in range(heads):
        @pl.when(first[hh] < j0)
        def _(hh=hh):
            lo = first[hh]
            count = j0 - lo

            def body1(t, carry):
                blocks(lo + 4 * t, 4, False, (hh,))
                return carry

            lax.fori_loop(0, count // 4, body1, 0)
            for left in (1, 2, 3):
                @pl.when(count % 4 == left)
                def _(left=left):
                    blocks(j0 - left, left, False, (hh,))

    span = qi - j0
    quads = span // 4

    def body(t, carry):
        blocks(j0 + 4 * t, 4, False)
        return carry

    lax.fori_loop(0, quads, body, 0)
    for left in (0, 1, 2, 3):
        @pl.when(span - 4 * quads == left)
        def _(left=left):
            blocks(qi - left, left + 1, True)

    _flash_finalize(acc_ref, o_ref, bq)


def _first_key_block(stats):
    qmax, kmax = stats[:, :, 0, :FOX_HEADS], stats[:, :, 1, :FOX_HEADS]
    c_first, c_last = stats[:, :, 2, :FOX_HEADS], stats[:, :, 3, :FOX_HEADS]
    diag_min = stats[:, :, 4, :FOX_HEADS]
    kall = jnp.max(kmax, axis=1, keepdims=True)
    thr = c_first + SKIP_LOG2 + (NORM_SLACK + DIAG_SLACK) * qmax * kall - diag_min
    skippable = c_last[:, None, :, :] > thr[:, :, None, :]
    first = jnp.sum(jnp.cumprod(skippable.astype(jnp.int32), axis=2), axis=2)
    nq = stats.shape[1]
    first = jnp.minimum(first, jnp.arange(nq, dtype=jnp.int32)[None, :, None])
    return first.transpose(0, 2, 1).reshape(-1)


def _fox_flash(q, k, v, stats):
    B, S, width = q.shape
    bq = FLASH_BQ
    pair = 2 * HEAD_AUG
    first = _first_key_block(stats)
    qmax, kmax, diag_min = (stats[:, :, r, :FOX_HEADS] for r in (0, 1, 4))
    kall = jnp.max(kmax, axis=1)
    gap = NORM_SLACK * qmax * kall[:, None, :] - diag_min
    fixed_shift_ok = jnp.max(gap) < SHIFT_GAP_LOG2

    blk = lambda shape, index, **kw: pl.BlockSpec(shape, lambda b, hp, i, first: index(b, hp, i), **kw)
    tensor_specs = [
        blk((1, FLASH_NSUB * bq, pair), lambda b, hp, i: (b, i, hp)),
        blk((1, S, pair), lambda b, hp, i: (b, 0, hp)),
        blk((1, S, pair), lambda b, hp, i: (b, 0, hp), pipeline_mode=pl.Buffered(1)),
    ]
    common = dict(
        out_shape=jax.ShapeDtypeStruct((B, S, FOX_HEADS * HEAD_DIM), jnp.bfloat16),
        compiler_params=pltpu.CompilerParams(
            dimension_semantics=("arbitrary", "arbitrary", "arbitrary"),
            vmem_limit_bytes=VMEM_LIMIT))
    grid = (B, FOX_HEADS // 2, S // (FLASH_NSUB * bq))
    out_spec = blk((1, FLASH_NSUB * bq, 2 * HEAD_DIM), lambda b, hp, i: (b, i, hp))
    vt_qt = [pltpu.VMEM((pair, S), jnp.bfloat16), pltpu.VMEM((pair, bq), jnp.bfloat16)]

    def fixed_shift():
        return pl.pallas_call(
            functools.partial(_fox_flash_fixed_kernel, bq=bq, seq=S, nsub=FLASH_NSUB),
            grid_spec=pltpu.PrefetchScalarGridSpec(
                num_scalar_prefetch=1, grid=grid,
                in_specs=[pl.BlockSpec(memory_space=pltpu.SMEM)] + tensor_specs,
                out_specs=out_spec,
                scratch_shapes=vt_qt + [pltpu.VMEM((2, 1, bq), jnp.float32),
                                        pltpu.VMEM((2, HEAD_AUG, bq), jnp.float32),
                                        pltpu.VMEM((2, 4 * bq, bq), jnp.bfloat16)]),
            name="fox_flash_fixed", **common)(first, kall.reshape(-1), q, k, v)

    def running_max():
        return pl.pallas_call(
            functools.partial(_fox_flash_kernel, bq=bq, seq=S, nsub=FLASH_NSUB),
            grid_spec=pltpu.PrefetchScalarGridSpec(
                num_scalar_prefetch=1, grid=grid, in_specs=tensor_specs, out_specs=out_spec,
                scratch_shapes=vt_qt + [pltpu.VMEM((2, 1, bq), jnp.float32),
                                        pltpu.VMEM((2, HEAD_AUG, bq), jnp.float32),
                                        pltpu.VMEM((4, bq, bq), jnp.float32),
                                        pltpu.VMEM((2, bq, bq), jnp.bfloat16),
                                        pltpu.VMEM((4, 1, bq), jnp.float32)]),
            name="fox_flash", **common)(first, q, k, v)

    return lax.cond(fixed_shift_ok, fixed_shift, running_max)


def _out_mlp_kernel(o_ref, x_ref, wo_ref, g_post_ref, g_pre_ref, wu_ref, wd_ref, g_ffn_ref, y_ref,
                    x1_ref, h_ref, *, sub):
    n_sub = x_ref.shape[0] // sub
    for s in range(n_sub):
        rows = slice(s * sub, (s + 1) * sub)
        a = _dot(o_ref[rows, :], wo_ref[...])
        x1 = x_ref[rows, :] + _rms(a, g_post_ref[...])
        x1_ref[rows, :] = x1
        h_ref[rows, :] = _rms(x1, g_pre_ref[...]).astype(jnp.bfloat16)
    for s in range(n_sub):
        rows = slice(s * sub, (s + 1) * sub)
        acc = jnp.zeros((sub, x_ref.shape[1]), jnp.float32)
        for c in range(D_FF // MLP_CHUNK):
            cols = slice(c * MLP_CHUNK, (c + 1) * MLP_CHUNK)
            u = jnp.maximum(_dot(h_ref[rows, :], wu_ref[:, cols]), 0.0)
            acc = acc + _dot((u * u).astype(jnp.bfloat16), wd_ref[cols, :])
        y_ref[rows, :] = x1_ref[rows, :] + _rms(acc, g_ffn_ref[...])


def _out_mlp(o, x, w_out, g_post, g_pre, w_up, w_down, g_ffn):
    T, D = x.shape
    tm = MLP_TM
    row = lambda width: pl.BlockSpec((tm, width), lambda t: (t, 0))
    return pl.pallas_call(
        functools.partial(_out_mlp_kernel, sub=PROJ_TM),
        out_shape=jax.ShapeDtypeStruct((T, D), jnp.float32),
        grid=(T // tm,),
        in_specs=[row(o.shape[1]), row(D),
                  _const_spec(w_out.shape), _const_spec((1, D)), _const_spec((1, D)),
                  _const_spec(w_up.shape), _const_spec(w_down.shape), _const_spec((1, D))],
        out_specs=row(D),
        scratch_shapes=[pltpu.VMEM((tm, D), jnp.float32), pltpu.VMEM((tm, D), jnp.bfloat16)],
        compiler_params=pltpu.CompilerParams(
            dimension_semantics=("arbitrary",), vmem_limit_bytes=VMEM_LIMIT),
        name="out_mlp",
    )(o, x, w_out.astype(jnp.bfloat16), g_post[None, :], g_pre[None, :],
      w_up.astype(jnp.bfloat16), w_down.astype(jnp.bfloat16), g_ffn[None, :])


def _swa_proj_kernel(x_ref, pos_ref, g_ref, wq_ref, wk_ref, wv_ref, freq_ref, lo_ref, hi_ref,
                     q_out, k_out, v_out):
    h = _rms(x_ref[...], g_ref[...]).astype(jnp.bfloat16)
    ang = pos_ref[...].astype(jnp.float32) * freq_ref[...]
    cos = jnp.cos(ang)
    sin = jnp.sin(ang)
    sin_lo = sin * lo_ref[...]
    sin_hi = sin * hi_ref[...]
    half = HEAD_DIM // 2

    def rope(t):
        return t * cos + pltpu.roll(t, LANES - half, 1) * sin_lo + pltpu.roll(t, half, 1) * sin_hi

    q = _dot(h, wq_ref[...])
    for c in range(q_out.shape[1] // LANES):
        cols = slice(c * LANES, (c + 1) * LANES)
        q_out[:, cols] = (rope(q[:, cols]) * (QK_SCALE * LOG2E)).astype(jnp.bfloat16)
    k = _dot(h, wk_ref[...])
    for c in range(k_out.shape[1] // LANES):
        cols = slice(c * LANES, (c + 1) * LANES)
        k_out[:, cols] = rope(k[:, cols]).astype(jnp.bfloat16)
    v_out[...] = _dot(h, wv_ref[...]).astype(jnp.bfloat16)


def _dup_heads(w, heads):
    d = w.shape[0]
    w = w.reshape(d, heads, 1, HEAD_DIM)
    return jnp.broadcast_to(w, (d, heads, 2, HEAD_DIM)).reshape(d, heads * 2 * HEAD_DIM)


def _swa_proj(x, pos, g, w_in):
    T, D = x.shape
    tm = PROJ_TM
    qd = SWA_Q_HEADS * HEAD_DIM
    kvd = SWA_KV_HEADS * HEAD_DIM
    kw = SWA_KV_HEADS * 2 * HEAD_DIM
    bf = lambda a: a.astype(jnp.bfloat16)
    wq = w_in[:, :qd]
    wk = w_in[:, qd:qd + kvd]
    wv = w_in[:, qd + kvd:]
    half = HEAD_DIM // 2
    inv_freq = 1.0 / (ROPE_THETA ** (jnp.arange(0, HEAD_DIM, 2, dtype=jnp.float32) / HEAD_DIM))
    freq = jnp.tile(inv_freq, LANES // half)[None, :]
    first_half = (np.arange(LANES) % HEAD_DIM) < half
    lo = np.where(first_half, -1.0, 0.0).astype(np.float32)[None, :]
    hi = np.where(first_half, 0.0, 1.0).astype(np.float32)[None, :]
    row = lambda width: pl.BlockSpec((tm, width), lambda t: (t, 0))
    return pl.pallas_call(
        _swa_proj_kernel,
        out_shape=(jax.ShapeDtypeStruct((T, qd), jnp.bfloat16),
                   jax.ShapeDtypeStruct((T, kw), jnp.bfloat16),
                   jax.ShapeDtypeStruct((T, kw), jnp.bfloat16)),
        grid=(T // tm,),
        in_specs=[row(D), row(1), _const_spec((1, D)),
                  _const_spec((D, qd)), _const_spec((D, kw)), _const_spec((D, kw)),
                  _const_spec((1, LANES)), _const_spec((1, LANES)), _const_spec((1, LANES))],
        out_specs=(row(qd), row(kw), row(kw)),
        compiler_params=pltpu.CompilerParams(
            dimension_semantics=("arbitrary",), vmem_limit_bytes=VMEM_LIMIT),
        name="swa_proj",
    )(x, pos, g[None, :], bf(wq), bf(_dup_heads(wk, SWA_KV_HEADS)), bf(_dup_heads(wv, SWA_KV_HEADS)),
      freq, jnp.asarray(lo), jnp.asarray(hi))


def _swa_attn_kernel(sink_ref, q_ref, *refs):
    k_refs, v_refs = refs[:SWA_SUB + 1], refs[SWA_SUB + 1:2 * SWA_SUB + 2]
    o_ref, s_ref = refs[2 * SWA_SUB + 2:]
    n = pl.program_id(1)
    blk = WINDOW
    cols_per_group = SWA_GROUP // 2
    key = lax.broadcasted_iota(jnp.int32, (2 * blk, blk), 0)
    qry = lax.broadcasted_iota(jnp.int32, (2 * blk, blk), 1)
    diff = qry + blk - key
    in_window = (diff >= 0) & (diff < WINDOW)
    klane = lax.broadcasted_iota(jnp.int32, (2 * blk, LANES), 1)
    top = lax.broadcasted_iota(jnp.int32, (LANES, blk), 0) < HEAD_DIM
    maxes = {}
    for sub in range(SWA_SUB):
        rows = slice(sub * blk, (sub + 1) * blk)
        kb = jnp.concatenate([k_refs[sub][0], k_refs[sub + 1][0]], axis=0).astype(jnp.float32)
        mask = in_window & (key >= jnp.where(n > 0, 0, blk)) if sub == 0 else in_window
        for g in range(SWA_KV_HEADS):
            gl = slice(g * LANES, (g + 1) * LANES)
            qt = jnp.concatenate([q_ref[0, rows, c * LANES:(c + 1) * LANES].T
                                  for c in range(g * cols_per_group, (g + 1) * cols_per_group)], axis=1)
            for par in range(2):
                keep = (klane < HEAD_DIM) if par == 0 else (klane >= HEAD_DIM)
                kpar = jnp.where(keep, kb[:, gl], 0.0).astype(jnp.bfloat16)
                st = _dot(kpar, qt)
                st = jnp.concatenate([jnp.where(mask, st[:, c * blk:(c + 1) * blk], NEG_BIG)
                                      for c in range(cols_per_group)], axis=1)
                slot = (sub * SWA_KV_HEADS + g) * 2 + par
                s_ref[slot] = st
                maxes[slot] = jnp.max(st, axis=0, keepdims=True)
    for sub in range(SWA_SUB):
        rows = slice(sub * blk, (sub + 1) * blk)
        vbt = jnp.concatenate([v_refs[sub][0], v_refs[sub + 1][0]], axis=0).T
        for g in range(SWA_KV_HEADS):
            gl = slice(g * LANES, (g + 1) * LANES)
            outs = []
            for par in range(2):
                slot = (sub * SWA_KV_HEADS + g) * 2 + par
                sink = sink_ref[2 * g + par:2 * g + par + 1, :]
                m = jnp.maximum(maxes[slot], sink)
                e = jnp.exp2(s_ref[slot] - m)
                denom = jnp.sum(e, axis=0, keepdims=True) + jnp.exp2(sink - m)
                ot = _dot(vbt[gl, :], e.astype(jnp.bfloat16))
                outs.append(ot * (1.0 / denom))
            for c in range(cols_per_group):
                both = jnp.where(top, outs[0][:, c * blk:(c + 1) * blk], outs[1][:, c * blk:(c + 1) * blk])
                col = g * cols_per_group + c
                o_ref[0, rows, col * LANES:(col + 1) * LANES] = both.T.astype(jnp.bfloat16)


def _swa_attn(q, k, v, sinks):
    B, S, qd = q.shape
    kw = k.shape[2]
    blk = WINDOW
    own = lambda b, n: (b, n, 0)
    band = [lambda b, n: (b, jnp.maximum(SWA_SUB * n - 1, 0), 0)]
    band += [functools.partial(lambda b, n, i: (b, SWA_SUB * n + i, 0), i=i) for i in range(SWA_SUB)]
    band_specs = [pl.BlockSpec((1, blk, kw), index) for index in band]
    cols_per_group = SWA_GROUP // 2
    sink_rows = (sinks * LOG2E).reshape(SWA_KV_HEADS, cols_per_group, 2).transpose(0, 2, 1)
    sink_rows = jnp.repeat(sink_rows.reshape(2 * SWA_KV_HEADS, cols_per_group), blk, axis=1)
    sink_rows = jnp.pad(sink_rows, ((0, SUBLANES - 2 * SWA_KV_HEADS), (0, 0)))
    return pl.pallas_call(
        _swa_attn_kernel,
        out_shape=jax.ShapeDtypeStruct((B, S, qd), jnp.bfloat16),
        grid=(B, S // (SWA_SUB * blk)),
        in_specs=[_const_spec(sink_rows.shape), pl.BlockSpec((1, SWA_SUB * blk, qd), own)] + band_specs + band_specs,
        out_specs=pl.BlockSpec((1, SWA_SUB * blk, qd), own),
        scratch_shapes=[pltpu.VMEM((SWA_SUB * 2 * SWA_KV_HEADS, 2 * blk, cols_per_group * blk), jnp.float32)],
        compiler_params=pltpu.CompilerParams(
            dimension_semantics=("arbitrary", "arbitrary"), vmem_limit_bytes=VMEM_LIMIT),
        name="swa_attn",
    )(sink_rows, q, *([k] * (SWA_SUB + 1)), *([v] * (SWA_SUB + 1)))


def kernel(x, positions, fox_w_in, fox_b_f, fox_w_out, swa_w_in, swa_sinks, swa_w_out,
           norm_pre_mix, norm_post_mix, norm_pre_ffn, norm_post_ffn, mlp_w_up, mlp_w_down):
    B, S, D = x.shape
    T = B * S
    q, k, v, stats = _fox_proj(x, norm_pre_mix[0], fox_w_in[0], fox_b_f[0])
    o = _fox_flash(q, k, v, stats)
    x1 = _out_mlp(o.reshape(T, -1), x.reshape(T, D), fox_w_out[0], norm_post_mix[0], norm_pre_ffn[0],
                  mlp_w_up[0], mlp_w_down[0], norm_post_ffn[0])
    q, k, v = _swa_proj(x1, positions.reshape(T, 1), norm_pre_mix[1], swa_w_in[0])
    o = _swa_attn(q.reshape(B, S, -1), k.reshape(B, S, -1), v.reshape(B, S, -1), swa_sinks[0])
    x2 = _out_mlp(o.reshape(T, -1), x1, swa_w_out[0], norm_post_mix[1], norm_pre_ffn[1],
                  mlp_w_up[1], mlp_w_down[1], norm_post_ffn[1])
    return x2.reshape(B, S, D)
```

```python
import functools

import jax
import jax.numpy as jnp
import numpy as np
from jax import lax
from jax.experimental import pallas as pl
from jax.experimental.pallas import tpu as pltpu

D_MODEL = 1024
HEAD_DIM = 64
FOX_HEADS = 16
SWA_Q_HEADS = 16
SWA_KV_HEADS = 2
SWA_GROUP = SWA_Q_HEADS // SWA_KV_HEADS
WINDOW = 128
D_FF = 4 * D_MODEL
ROPE_THETA = 10000.0
NORM_EPS = 1e-6

LANES = 128
SUBLANES = 8
HEAD_AUG = LANES
LOG2E = 1.4426950408889634
QK_SCALE = HEAD_DIM ** -0.5
NEG_BIG = -1e30
SKIP_LOG2 = 152.0
NORM_SLACK = 1.02
DIAG_SLACK = 0.02
SHIFT_GAP_LOG2 = 100.0
VMEM_LIMIT = 56 * 1024 * 1024

DEC0 = HEAD_DIM
N_SPLIT = 3
ONES_LANE = N_SPLIT * FOX_HEADS

PROJ_TM = 512
FLASH_BQ = 512
FLASH_NSUB = 4
MLP_TM = 1024
MLP_CHUNK = 1024
SWA_SUB = 4


def _rms(x, g):
    return x * lax.rsqrt(jnp.mean(x * x, axis=-1, keepdims=True) + NORM_EPS) * g


def _dot(a, b):
    return jnp.dot(a, b, preferred_element_type=jnp.float32)


def _split3(x):
    hi = x.astype(jnp.bfloat16)
    r = x - hi.astype(jnp.float32)
    mid = r.astype(jnp.bfloat16)
    lo = (r - mid.astype(jnp.float32)).astype(jnp.bfloat16)
    return hi, mid, lo


def _const_spec(shape):
    nd = len(shape)
    return pl.BlockSpec(shape, lambda *_: (0,) * nd, pipeline_mode=pl.Buffered(1))


def _fox_proj_kernel(x_ref, g_ref, wq_ref, wk_ref, wv_ref, wf_ref, b3_ref, place_ref,
                     tri_ref, sel_ref, q_out, k_out, v_out, stats_out, carry_ref, *, tm):
    @pl.when(pl.program_id(1) == 0)
    def _():
        carry_ref[...] = jnp.zeros_like(carry_ref)

    h = _rms(x_ref[0], g_ref[...]).astype(jnp.bfloat16)

    z = _dot(h, wf_ref[...]) + b3_ref[...]
    logf = jnp.minimum(z, 0.0) - jnp.log1p(jnp.exp(-jnp.abs(z)))

    tri = tri_ref[...]
    hi, mid, lo = _split3(logf)
    c = _dot(tri, hi) + _dot(tri, mid) + _dot(tri, lo) + carry_ref[...]
    carry_ref[...] = c[tm - 1:tm, :]

    chi, cmid, clo = (p.astype(jnp.float32) for p in _split3(c * LOG2E))
    lane = lax.broadcasted_iota(jnp.int32, (tm, LANES), 1)
    parts = jnp.where(lane < FOX_HEADS, chi,
                      jnp.where(lane < 2 * FOX_HEADS, cmid,
                                jnp.where(lane < ONES_LANE, clo,
                                          jnp.where(lane == ONES_LANE, 1.0, 0.0))))
    parts = parts.astype(jnp.bfloat16)

    width = FOX_HEADS * HEAD_AUG
    q = _dot(h, wq_ref[...]) * (QK_SCALE * LOG2E)
    k = _dot(h, wk_ref[...])
    v = _dot(h, wv_ref[...])
    qn2 = _dot((q * q).astype(jnp.bfloat16), sel_ref[...])
    kn2 = _dot((k * k).astype(jnp.bfloat16), sel_ref[...])

    low = lane < HEAD_DIM
    v_tail = jnp.where(lane == HEAD_DIM, 1.0, 0.0)
    for pr in range(FOX_HEADS // 2):
        pcols = slice(pr * LANES, (pr + 1) * LANES)
        acols = slice(2 * pr * HEAD_AUG, (2 * pr + 2) * HEAD_AUG)
        q_tail = _dot(parts, place_ref[:, acols])
        k_tail = _dot(parts, place_ref[:, width + 2 * pr * HEAD_AUG:width + (2 * pr + 2) * HEAD_AUG])
        for src, tail, out in ((q[:, pcols], q_tail, q_out), (k[:, pcols], k_tail, k_out),
                               (v[:, pcols], None, v_out)):
            for par, head in enumerate((src, pltpu.roll(src, HEAD_DIM, 1))):
                fill = v_tail if tail is None else tail[:, par * HEAD_AUG:(par + 1) * HEAD_AUG]
                hcols = slice((2 * pr + par) * HEAD_AUG, (2 * pr + par + 1) * HEAD_AUG)
                out[0, :, hcols] = jnp.where(low, head, fill).astype(jnp.bfloat16)

    c2 = c * LOG2E
    stats_out[0, 0] = jnp.zeros((SUBLANES, LANES), jnp.float32)
    stats_out[0, 0, 0:1, :] = jnp.sqrt(jnp.max(qn2, axis=0, keepdims=True))
    stats_out[0, 0, 1:2, :] = jnp.sqrt(jnp.max(kn2, axis=0, keepdims=True))
    stats_out[0, 0, 2:3, :] = c2[0:1, :]
    stats_out[0, 0, 3:4, :] = c2[tm - 1:tm, :]
    diag = _dot((q * k).astype(jnp.bfloat16), sel_ref[...])
    stats_out[0, 0, 4:5, :] = jnp.min(diag, axis=0, keepdims=True)


def _fox_placement():
    width = FOX_HEADS * HEAD_AUG
    p = np.zeros((LANES, 2 * width), np.float32)
    for h in range(FOX_HEADS):
        base = h * HEAD_AUG + DEC0
        for s in range(N_SPLIT):
            p[s * FOX_HEADS + h, base + s] = 1.0
            p[ONES_LANE, base + N_SPLIT + s] = 1.0
            p[ONES_LANE, width + base + s] = 1.0
            p[s * FOX_HEADS + h, width + base + N_SPLIT + s] = -1.0
    return p


def _fox_proj(x, g, w_in, b_f):
    B, S, D = x.shape
    tm = PROJ_TM
    hd_all = FOX_HEADS * HEAD_DIM
    width = FOX_HEADS * HEAD_AUG
    wq = w_in[:, :hd_all].astype(jnp.bfloat16)
    wk = w_in[:, hd_all:2 * hd_all].astype(jnp.bfloat16)
    wv = w_in[:, 2 * hd_all:3 * hd_all].astype(jnp.bfloat16)
    wf = w_in[:, 3 * hd_all:]
    wf3 = jnp.concatenate([wf] * N_SPLIT + [jnp.zeros((D, LANES - ONES_LANE), wf.dtype)], axis=1)
    b3 = jnp.concatenate([b_f] * N_SPLIT + [jnp.zeros((LANES - ONES_LANE,), b_f.dtype)])[None, :]
    place = jnp.asarray(_fox_placement(), jnp.bfloat16)
    tri = jnp.asarray(np.tril(np.ones((tm, tm), np.float32)), jnp.bfloat16)
    sel = np.zeros((hd_all, LANES), np.float32)
    for hd in range(FOX_HEADS):
        sel[hd * HEAD_DIM:(hd + 1) * HEAD_DIM, hd] = 1.0

    out_sds = jax.ShapeDtypeStruct((B, S, width), jnp.bfloat16)
    out_spec = pl.BlockSpec((1, tm, width), lambda b, t: (b, t, 0))
    stats_sds = jax.ShapeDtypeStruct((B, S // tm, SUBLANES, LANES), jnp.float32)
    stats_spec = pl.BlockSpec((1, 1, SUBLANES, LANES), lambda b, t: (b, t, 0, 0))
    return pl.pallas_call(
        functools.partial(_fox_proj_kernel, tm=tm),
        out_shape=(out_sds, out_sds, out_sds, stats_sds),
        grid=(B, S // tm),
        in_specs=[
            pl.BlockSpec((1, tm, D), lambda b, t: (b, t, 0)),
            _const_spec((1, D)),
            _const_spec((D, hd_all)), _const_spec((D, hd_all)), _const_spec((D, hd_all)),
            _const_spec((D, LANES)), _const_spec((1, LANES)),
            _const_spec((LANES, 2 * width)),
            _const_spec((tm, tm)), _const_spec((hd_all, LANES)),
        ],
        out_specs=(out_spec, out_spec, out_spec, stats_spec),
        scratch_shapes=[pltpu.VMEM((1, LANES), jnp.float32)],
        compiler_params=pltpu.CompilerParams(
            dimension_semantics=("arbitrary", "arbitrary"), vmem_limit_bytes=VMEM_LIMIT),
        name="fox_proj",
    )(x, g[None, :], wq, wk, wv, wf3.astype(jnp.bfloat16), b3, place, tri,
      jnp.asarray(sel, jnp.bfloat16))


def _flash_transpose_v(v_ref, vt_ref, qi, bq, seq):
    @pl.when(qi == 0)
    def _():
        for c in range(seq // bq):
            rows = slice(c * bq, (c + 1) * bq)
            vt_ref[:, rows] = v_ref[0, rows, :].T


def _flash_finalize(acc_ref, o_ref, bq):
    halves = []
    for hh in range(2):
        acc = acc_ref[hh]
        halves.append(acc[0:HEAD_DIM, :] * (1.0 / acc[HEAD_DIM:HEAD_DIM + 1, :]))
    o_ref[...] = jnp.concatenate(halves, axis=0).T.astype(jnp.bfloat16)


def _per_query_block(block_fn, q_ref, o_ref, bq, nsub):
    def one(u, carry):
        rows = pl.ds(pl.multiple_of(u * bq, bq), bq)
        block_fn(nsub * pl.program_id(2) + u, q_ref.at[0, rows, :], o_ref.at[0, rows, :])
        return carry

    lax.fori_loop(0, nsub, one, 0)


def _fox_flash_kernel(first_ref, q_ref, k_ref, v_ref, o_ref, *scratch, bq, seq, nsub):
    block_fn = functools.partial(_flash_running_max_block, first_ref=first_ref, k_ref=k_ref, v_ref=v_ref,
                                 scratch=scratch, bq=bq, seq=seq)
    _per_query_block(block_fn, q_ref, o_ref, bq, nsub)


def _flash_running_max_block(qi, q_ref, o_ref, *, first_ref, k_ref, v_ref, scratch, bq, seq):
    vt_ref, qt_ref, m_ref, acc_ref, s_ref, p_ref, mb_ref = scratch
    heads = 2
    nq = seq // bq
    head0 = (pl.program_id(0) * FOX_HEADS + heads * pl.program_id(1)) * nq + qi
    j0 = jnp.minimum(first_ref[head0], first_ref[head0 + nq])
    _flash_transpose_v(v_ref, vt_ref, qi, bq, seq)
    qt_ref[...] = q_ref[...].T
    m_ref[...] = jnp.full_like(m_ref, NEG_BIG)
    acc_ref[...] = jnp.zeros_like(acc_ref)

    def scores(j, slot, masked):
        start = pl.multiple_of(j * bq, bq)
        for hh in range(heads):
            cols = slice(hh * HEAD_AUG, (hh + 1) * HEAD_AUG)
            st = _dot(k_ref[0, pl.ds(start, bq), cols], qt_ref[cols, :])
            if masked:
                key = lax.broadcasted_iota(jnp.int32, (bq, bq), 0)
                qry = lax.broadcasted_iota(jnp.int32, (bq, bq), 1)
                st = jnp.where(key - qry <= (qi - j) * bq, st, NEG_BIG)
            s_ref[2 * slot + hh] = st
            mb_ref[2 * slot + hh] = jnp.max(st, axis=0, keepdims=True)

    def consume(j, slot):
        start = pl.multiple_of(j * bq, bq)
        alpha = []
        for hh in range(heads):
            m_prev = m_ref[hh]
            m_new = jnp.maximum(m_prev, mb_ref[2 * slot + hh])
            m_ref[hh] = m_new
            p_ref[hh] = jnp.exp2(s_ref[2 * slot + hh] - m_new).astype(jnp.bfloat16)
            alpha.append(jnp.exp2(m_prev - m_new))
        for hh in range(heads):
            cols = slice(hh * HEAD_AUG, (hh + 1) * HEAD_AUG)
            vt = vt_ref[cols, pl.ds(start, bq)]
            acc_ref[hh] = alpha[hh] * acc_ref[hh] + _dot(vt, p_ref[hh])

    def half_step(j, slot, masked):
        scores(j + 1, 1 - slot, masked)
        consume(j, slot)

    span = qi - j0
    pairs = jnp.maximum(span - 1, 0) // 2
    rest = span - 2 * pairs
    scores(j0, 0, True)

    def body(t, carry):
        half_step(j0 + 2 * t, 0, False)
        half_step(j0 + 2 * t + 1, 1, False)
        return carry

    lax.fori_loop(0, pairs, body, 0)

    @pl.when(rest >= 1)
    def _():
        half_step(j0 + 2 * pairs, 0, True)

    @pl.when(rest == 2)
    def _():
        half_step(j0 + 2 * pairs + 1, 1, True)

    @pl.when(rest == 1)
    def _():
        consume(qi, 1)

    @pl.when(rest != 1)
    def _():
        consume(qi, 0)

    _flash_finalize(acc_ref, o_ref, bq)


def _fox_flash_fixed_kernel(first_ref, kall_ref, q_ref, k_ref, v_ref, o_ref, *scratch, bq, seq, nsub):
    block_fn = functools.partial(_flash_fixed_shift_block, first_ref=first_ref, kall_ref=kall_ref, k_ref=k_ref,
                                 v_ref=v_ref, scratch=scratch, bq=bq, seq=seq)
    _per_query_block(block_fn, q_ref, o_ref, bq, nsub)


def _flash_fixed_shift_block(qi, q_ref, o_ref, *, first_ref, kall_ref, k_ref, v_ref, scratch, bq, seq):
    vt_ref, qt_ref, r_ref, acc_ref, p_ref = scratch
    b, hp = pl.program_id(0), pl.program_id(1)
    heads = 2
    nq = seq // bq
    head0 = (b * FOX_HEADS + heads * hp) * nq + qi
    first = (first_ref[head0], first_ref[head0 + nq])
    j0 = jnp.maximum(first[0], first[1])
    _flash_transpose_v(v_ref, vt_ref, qi, bq, seq)
    qt_ref[...] = q_ref[...].T
    acc_ref[...] = jnp.zeros_like(acc_ref)
    for hh in range(heads):
        qf = qt_ref[hh * HEAD_AUG:hh * HEAD_AUG + HEAD_DIM, :].astype(jnp.float32)
        qnorm = jnp.sqrt(jnp.sum(qf * qf, axis=0, keepdims=True))
        r_ref[hh] = qnorm * (NORM_SLACK * kall_ref[b * FOX_HEADS + heads * hp + hh])

    def blocks(j, nblk, masked, which=(0, 1)):
        start = pl.multiple_of(j * bq, bq)
        for hh in which:
            cols = slice(hh * HEAD_AUG, (hh + 1) * HEAD_AUG)
            for u in range(nblk):
                st = _dot(k_ref[0, pl.ds(start + u * bq, bq), cols], qt_ref[cols, :])
                if masked and u == nblk - 1:
                    key = lax.broadcasted_iota(jnp.int32, (bq, bq), 0)
                    qry = lax.broadcasted_iota(jnp.int32, (bq, bq), 1)
                    st = jnp.where(key <= qry, st, NEG_BIG)
                p_ref[hh, u * bq:(u + 1) * bq, :] = jnp.exp2(st - r_ref[hh]).astype(jnp.bfloat16)
        for hh in which:
            cols = slice(hh * HEAD_AUG, (hh + 1) * HEAD_AUG)
            vt = vt_ref[cols, pl.ds(start, nblk * bq)]
            acc_ref[hh] += _dot(vt, p_ref[hh, 0:nblk * bq, :])

    for hh in range(heads):
        @pl.when(first[hh] < j0)
        def _(hh=hh):
            lo = first[hh]
            count = j0 - lo

            def body1(t, carry):
                blocks(lo + 4 * t, 4, False, (hh,))
                return carry

            lax.fori_loop(0, count // 4, body1, 0)
            for left in (1, 2, 3):
                @pl.when(count % 4 == left)
                def _(left=left):
                    blocks(j0 - left, left, False, (hh,))

    span = qi - j0
    quads = span // 4

    def body(t, carry):
        blocks(j0 + 4 * t, 4, False)
        return carry

    lax.fori_loop(0, quads, body, 0)
    for left in (0, 1, 2, 3):
        @pl.when(span - 4 * quads == left)
        def _(left=left):
            blocks(qi - left, left + 1, True)

    _flash_finalize(acc_ref, o_ref, bq)


def _first_key_block(stats):
    qmax, kmax = stats[:, :, 0, :FOX_HEADS], stats[:, :, 1, :FOX_HEADS]
    c_first, c_last = stats[:, :, 2, :FOX_HEADS], stats[:, :, 3, :FOX_HEADS]
    diag_min = stats[:, :, 4, :FOX_HEADS]
    kall = jnp.max(kmax, axis=1, keepdims=True)
    thr = c_first + SKIP_LOG2 + (NORM_SLACK + DIAG_SLACK) * qmax * kall - diag_min
    skippable = c_last[:, None, :, :] > thr[:, :, None, :]
    nq = stats.shape[1]
    blocks = jnp.arange(nq, dtype=jnp.int32)
    first = jnp.min(jnp.where(skippable, nq, blocks[None, None, :, None]), axis=2)
    first = jnp.minimum(first, blocks[None, :, None])
    return first.transpose(0, 2, 1).reshape(-1)


def _fox_flash(q, k, v, stats):
    B, S, width = q.shape
    bq = FLASH_BQ
    pair = 2 * HEAD_AUG
    first = _first_key_block(stats)
    qmax, kmax, diag_min = (stats[:, :, r, :FOX_HEADS] for r in (0, 1, 4))
    kall = jnp.max(kmax, axis=1)
    gap = NORM_SLACK * qmax * kall[:, None, :] - diag_min
    fixed_shift_ok = jnp.max(gap) < SHIFT_GAP_LOG2

    blk = lambda shape, index, **kw: pl.BlockSpec(shape, lambda b, hp, i, first: index(b, hp, i), **kw)
    tensor_specs = [
        blk((1, FLASH_NSUB * bq, pair), lambda b, hp, i: (b, i, hp)),
        blk((1, S, pair), lambda b, hp, i: (b, 0, hp)),
        blk((1, S, pair), lambda b, hp, i: (b, 0, hp), pipeline_mode=pl.Buffered(1)),
    ]
    common = dict(
        out_shape=jax.ShapeDtypeStruct((B, S, FOX_HEADS * HEAD_DIM), jnp.bfloat16),
        compiler_params=pltpu.CompilerParams(
            dimension_semantics=("arbitrary", "arbitrary", "arbitrary"),
            vmem_limit_bytes=VMEM_LIMIT))
    grid = (B, FOX_HEADS // 2, S // (FLASH_NSUB * bq))
    out_spec = blk((1, FLASH_NSUB * bq, 2 * HEAD_DIM), lambda b, hp, i: (b, i, hp))
    vt_qt = [pltpu.VMEM((pair, S), jnp.bfloat16), pltpu.VMEM((pair, bq), jnp.bfloat16)]

    def fixed_shift():
        return pl.pallas_call(
            functools.partial(_fox_flash_fixed_kernel, bq=bq, seq=S, nsub=FLASH_NSUB),
            grid_spec=pltpu.PrefetchScalarGridSpec(
                num_scalar_prefetch=1, grid=grid,
                in_specs=[pl.BlockSpec(memory_space=pltpu.SMEM)] + tensor_specs,
                out_specs=out_spec,
                scratch_shapes=vt_qt + [pltpu.VMEM((2, 1, bq), jnp.float32),
                                        pltpu.VMEM((2, HEAD_AUG, bq), jnp.float32),
                                        pltpu.VMEM((2, 4 * bq, bq), jnp.bfloat16)]),
            name="fox_flash_fixed", **common)(first, kall.reshape(-1), q, k, v)

    def running_max():
        return pl.pallas_call(
            functools.partial(_fox_flash_kernel, bq=bq, seq=S, nsub=FLASH_NSUB),
            grid_spec=pltpu.PrefetchScalarGridSpec(
                num_scalar_prefetch=1, grid=grid, in_specs=tensor_specs, out_specs=out_spec,
                scratch_shapes=vt_qt + [pltpu.VMEM((2, 1, bq), jnp.float32),
                                        pltpu.VMEM((2, HEAD_AUG, bq), jnp.float32),
                                        pltpu.VMEM((4, bq, bq), jnp.float32),
                                        pltpu.VMEM((2, bq, bq), jnp.bfloat16),
                                        pltpu.VMEM((4, 1, bq), jnp.float32)]),
            name="fox_flash", **common)(first, q, k, v)

    return lax.cond(fixed_shift_ok, fixed_shift, running_max)


def _out_mlp_kernel(o_ref, x_ref, wo_ref, g_post_ref, g_pre_ref, wu_ref, wd_ref, g_ffn_ref, y_ref,
                    x1_ref, h_ref, *, sub):
    n_sub = x_ref.shape[0] // sub
    for s in range(n_sub):
        rows = slice(s * sub, (s + 1) * sub)
        a = _dot(o_ref[rows, :], wo_ref[...])
        x1 = x_ref[rows, :] + _rms(a, g_post_ref[...])
        x1_ref[rows, :] = x1
        h_ref[rows, :] = _rms(x1, g_pre_ref[...]).astype(jnp.bfloat16)
    for s in range(n_sub):
        rows = slice(s * sub, (s + 1) * sub)
        acc = jnp.zeros((sub, x_ref.shape[1]), jnp.float32)
        for c in range(D_FF // MLP_CHUNK):
            cols = slice(c * MLP_CHUNK, (c + 1) * MLP_CHUNK)
            u = jnp.maximum(_dot(h_ref[rows, :], wu_ref[:, cols]), 0.0)
            acc = acc + _dot((u * u).astype(jnp.bfloat16), wd_ref[cols, :])
        y_ref[rows, :] = x1_ref[rows, :] + _rms(acc, g_ffn_ref[...])


def _out_mlp(o, x, w_out, g_post, g_pre, w_up, w_down, g_ffn):
    T, D = x.shape
    tm = MLP_TM
    row = lambda width: pl.BlockSpec((tm, width), lambda t: (t, 0))
    return pl.pallas_call(
        functools.partial(_out_mlp_kernel, sub=PROJ_TM),
        out_shape=jax.ShapeDtypeStruct((T, D), jnp.float32),
        grid=(T // tm,),
        in_specs=[row(o.shape[1]), row(D),
                  _const_spec(w_out.shape), _const_spec((1, D)), _const_spec((1, D)),
                  _const_spec(w_up.shape), _const_spec(w_down.shape), _const_spec((1, D))],
        out_specs=row(D),
        scratch_shapes=[pltpu.VMEM((tm, D), jnp.float32), pltpu.VMEM((tm, D), jnp.bfloat16)],
        compiler_params=pltpu.CompilerParams(
            dimension_semantics=("arbitrary",), vmem_limit_bytes=VMEM_LIMIT),
        name="out_mlp",
    )(o, x, w_out.astype(jnp.bfloat16), g_post[None, :], g_pre[None, :],
      w_up.astype(jnp.bfloat16), w_down.astype(jnp.bfloat16), g_ffn[None, :])


def _swa_proj_kernel(x_ref, pos_ref, g_ref, wq_ref, wk_ref, wv_ref, freq_ref, lo_ref, hi_ref,
                     q_out, k_out, v_out):
    h = _rms(x_ref[...], g_ref[...]).astype(jnp.bfloat16)
    ang = pos_ref[...].astype(jnp.float32) * freq_ref[...]
    cos = jnp.cos(ang)
    sin = jnp.sin(ang)
    sin_lo = sin * lo_ref[...]
    sin_hi = sin * hi_ref[...]
    half = HEAD_DIM // 2

    def rope(t):
        return t * cos + pltpu.roll(t, LANES - half, 1) * sin_lo + pltpu.roll(t, half, 1) * sin_hi

    q = _dot(h, wq_ref[...])
    for c in range(q_out.shape[1] // LANES):
        cols = slice(c * LANES, (c + 1) * LANES)
        q_out[:, cols] = (rope(q[:, cols]) * (QK_SCALE * LOG2E)).astype(jnp.bfloat16)
    k = _dot(h, wk_ref[...])
    for c in range(k_out.shape[1] // LANES):
        cols = slice(c * LANES, (c + 1) * LANES)
        k_out[:, cols] = rope(k[:, cols]).astype(jnp.bfloat16)
    v_out[...] = _dot(h, wv_ref[...]).astype(jnp.bfloat16)


def _dup_heads(w, heads):
    d = w.shape[0]
    w = w.reshape(d, heads, 1, HEAD_DIM)
    return jnp.broadcast_to(w, (d, heads, 2, HEAD_DIM)).reshape(d, heads * 2 * HEAD_DIM)


def _swa_proj(x, pos, g, w_in):
    T, D = x.shape
    tm = PROJ_TM
    qd = SWA_Q_HEADS * HEAD_DIM
    kvd = SWA_KV_HEADS * HEAD_DIM
    kw = SWA_KV_HEADS * 2 * HEAD_DIM
    bf = lambda a: a.astype(jnp.bfloat16)
    wq = w_in[:, :qd]
    wk = w_in[:, qd:qd + kvd]
    wv = w_in[:, qd + kvd:]
    half = HEAD_DIM // 2
    inv_freq = 1.0 / (ROPE_THETA ** (jnp.arange(0, HEAD_DIM, 2, dtype=jnp.float32) / HEAD_DIM))
    freq = jnp.tile(inv_freq, LANES // half)[None, :]
    first_half = (np.arange(LANES) % HEAD_DIM) < half
    lo = np.where(first_half, -1.0, 0.0).astype(np.float32)[None, :]
    hi = np.where(first_half, 0.0, 1.0).astype(np.float32)[None, :]
    row = lambda width: pl.BlockSpec((tm, width), lambda t: (t, 0))
    return pl.pallas_call(
        _swa_proj_kernel,
        out_shape=(jax.ShapeDtypeStruct((T, qd), jnp.bfloat16),
                   jax.ShapeDtypeStruct((T, kw), jnp.bfloat16),
                   jax.ShapeDtypeStruct((T, kw), jnp.bfloat16)),
        grid=(T // tm,),
        in_specs=[row(D), row(1), _const_spec((1, D)),
                  _const_spec((D, qd)), _const_spec((D, kw)), _const_spec((D, kw)),
                  _const_spec((1, LANES)), _const_spec((1, LANES)), _const_spec((1, LANES))],
        out_specs=(row(qd), row(kw), row(kw)),
        compiler_params=pltpu.CompilerParams(
            dimension_semantics=("arbitrary",), vmem_limit_bytes=VMEM_LIMIT),
        name="swa_proj",
    )(x, pos, g[None, :], bf(wq), bf(_dup_heads(wk, SWA_KV_HEADS)), bf(_dup_heads(wv, SWA_KV_HEADS)),
      freq, jnp.asarray(lo), jnp.asarray(hi))


def _swa_attn_kernel(sink_ref, q_ref, *refs):
    k_refs, v_refs = refs[:SWA_SUB + 1], refs[SWA_SUB + 1:2 * SWA_SUB + 2]
    o_ref, s_ref = refs[2 * SWA_SUB + 2:]
    n = pl.program_id(1)
    blk = WINDOW
    cols_per_group = SWA_GROUP // 2
    key = lax.broadcasted_iota(jnp.int32, (2 * blk, blk), 0)
    qry = lax.broadcasted_iota(jnp.int32, (2 * blk, blk), 1)
    diff = qry + blk - key
    in_window = (diff >= 0) & (diff < WINDOW)
    klane = lax.broadcasted_iota(jnp.int32, (2 * blk, LANES), 1)
    top = lax.broadcasted_iota(jnp.int32, (LANES, blk), 0) < HEAD_DIM
    maxes = {}
    for sub in range(SWA_SUB):
        rows = slice(sub * blk, (sub + 1) * blk)
        kb = jnp.concatenate([k_refs[sub][0], k_refs[sub + 1][0]], axis=0).astype(jnp.float32)
        mask = in_window & (key >= jnp.where(n > 0, 0, blk)) if sub == 0 else in_window
        for g in range(SWA_KV_HEADS):
            gl = slice(g * LANES, (g + 1) * LANES)
            qt = jnp.concatenate([q_ref[0, rows, c * LANES:(c + 1) * LANES].T
                                  for c in range(g * cols_per_group, (g + 1) * cols_per_group)], axis=1)
            for par in range(2):
                keep = (klane < HEAD_DIM) if par == 0 else (klane >= HEAD_DIM)
                kpar = jnp.where(keep, kb[:, gl], 0.0).astype(jnp.bfloat16)
                st = _dot(kpar, qt)
                st = jnp.concatenate([jnp.where(mask, st[:, c * blk:(c + 1) * blk], NEG_BIG)
                                      for c in range(cols_per_group)], axis=1)
                slot = (sub * SWA_KV_HEADS + g) * 2 + par
                s_ref[slot] = st
                maxes[slot] = jnp.max(st, axis=0, keepdims=True)
    for sub in range(SWA_SUB):
        rows = slice(sub * blk, (sub + 1) * blk)
        vbt = jnp.concatenate([v_refs[sub][0], v_refs[sub + 1][0]], axis=0).T
        for g in range(SWA_KV_HEADS):
            gl = slice(g * LANES, (g + 1) * LANES)
            outs = []
            for par in range(2):
                slot = (sub * SWA_KV_HEADS + g) * 2 + par
                sink = sink_ref[2 * g + par:2 * g + par + 1, :]
                m = jnp.maximum(maxes[slot], sink)
                e = jnp.exp2(s_ref[slot] - m)
                denom = jnp.sum(e, axis=0, keepdims=True) + jnp.exp2(sink - m)
                ot = _dot(vbt[gl, :], e.astype(jnp.bfloat16))
                outs.append(ot * (1.0 / denom))
            for c in range(cols_per_group):
                both = jnp.where(top, outs[0][:, c * blk:(c + 1) * blk], outs[1][:, c * blk:(c + 1) * blk])
                col = g * cols_per_group + c
                o_ref[0, rows, col * LANES:(col + 1) * LANES] = both.T.astype(jnp.bfloat16)


def _swa_attn(q, k, v, sinks):
    B, S, qd = q.shape
    kw = k.shape[2]
    blk = WINDOW
    own = lambda b, n: (b, n, 0)
    band = [lambda b, n: (b, jnp.maximum(SWA_SUB * n - 1, 0), 0)]
    band += [functools.partial(lambda b, n, i: (b, SWA_SUB * n + i, 0), i=i) for i in range(SWA_SUB)]
    band_specs = [pl.BlockSpec((1, blk, kw), index) for index in band]
    cols_per_group = SWA_GROUP // 2
    sink_rows = (sinks * LOG2E).reshape(SWA_KV_HEADS, cols_per_group, 2).transpose(0, 2, 1)
    sink_rows = jnp.repeat(sink_rows.reshape(2 * SWA_KV_HEADS, cols_per_group), blk, axis=1)
    sink_rows = jnp.pad(sink_rows, ((0, SUBLANES - 2 * SWA_KV_HEADS), (0, 0)))
    return pl.pallas_call(
        _swa_attn_kernel,
        out_shape=jax.ShapeDtypeStruct((B, S, qd), jnp.bfloat16),
        grid=(B, S // (SWA_SUB * blk)),
        in_specs=[_const_spec(sink_rows.shape), pl.BlockSpec((1, SWA_SUB * blk, qd), own)] + band_specs + band_specs,
        out_specs=pl.BlockSpec((1, SWA_SUB * blk, qd), own),
        scratch_shapes=[pltpu.VMEM((SWA_SUB * 2 * SWA_KV_HEADS, 2 * blk, cols_per_group * blk), jnp.float32)],
        compiler_params=pltpu.CompilerParams(
            dimension_semantics=("arbitrary", "arbitrary"), vmem_limit_bytes=VMEM_LIMIT),
        name="swa_attn",
    )(sink_rows, q, *([k] * (SWA_SUB + 1)), *([v] * (SWA_SUB + 1)))


def kernel(x, positions, fox_w_in, fox_b_f, fox_w_out, swa_w_in, swa_sinks, swa_w_out,
           norm_pre_mix, norm_post_mix, norm_pre_ffn, norm_post_ffn, mlp_w_up, mlp_w_down):
    B, S, D = x.shape
    T = B * S
    q, k, v, stats = _fox_proj(x, norm_pre_mix[0], fox_w_in[0], fox_b_f[0])
    o = _fox_flash(q, k, v, stats)
    x1 = _out_mlp(o.reshape(T, -1), x.reshape(T, D), fox_w_out[0], norm_post_mix[0], norm_pre_ffn[0],
                  mlp_w_up[0], mlp_w_down[0], norm_post_ffn[0])
    q, k, v = _swa_proj(x1, positions.reshape(T, 1), norm_pre_mix[1], swa_w_in[0])
    o = _swa_attn(q.reshape(B, S, -1), k.reshape(B, S, -1), v.reshape(B, S, -1), swa_sinks[0])
    x2 = _out_mlp(o.reshape(T, -1), x1, swa_w_out[0], norm_post_mix[1], norm_pre_ffn[1],
                  mlp_w_up[1], mlp_w_down[1], norm_post_ffn[1])
    return x2.reshape(B, S, D)
```

```python
import functools

import jax
import jax.numpy as jnp
import numpy as np
from jax import lax
from jax.experimental import pallas as pl
from jax.experimental.pallas import tpu as pltpu

D_MODEL = 1024
HEAD_DIM = 64
FOX_HEADS = 16
SWA_Q_HEADS = 16
SWA_KV_HEADS = 2
SWA_GROUP = SWA_Q_HEADS // SWA_KV_HEADS
WINDOW = 128
D_FF = 4 * D_MODEL
ROPE_THETA = 10000.0
NORM_EPS = 1e-6

LANES = 128
SUBLANES = 8
HEAD_AUG = LANES
LOG2E = 1.4426950408889634
QK_SCALE = HEAD_DIM ** -0.5
NEG_BIG = -1e30
SKIP_LOG2 = 152.0
NORM_SLACK = 1.02
DIAG_SLACK = 0.02
SHIFT_GAP_LOG2 = 100.0
VMEM_LIMIT = 56 * 1024 * 1024

DEC0 = HEAD_DIM
N_SPLIT = 3
ONES_LANE = N_SPLIT * FOX_HEADS

PROJ_TM = 512
FLASH_BQ = 512
FLASH_NSUB = 4
MLP_TM = 1024
MLP_CHUNK = 1024
SWA_SUB = 4


def _rms(x, g):
    return x * lax.rsqrt(jnp.mean(x * x, axis=-1, keepdims=True) + NORM_EPS) * g


def _dot(a, b):
    return jnp.dot(a, b, preferred_element_type=jnp.float32)


def _split3(x):
    hi = x.astype(jnp.bfloat16)
    r = x - hi.astype(jnp.float32)
    mid = r.astype(jnp.bfloat16)
    lo = (r - mid.astype(jnp.float32)).astype(jnp.bfloat16)
    return hi, mid, lo


def _const_spec(shape):
    nd = len(shape)
    return pl.BlockSpec(shape, lambda *_: (0,) * nd, pipeline_mode=pl.Buffered(1))


def _fox_proj_kernel(x_ref, g_ref, wq_ref, wk_ref, wv_ref, wf_ref, b3_ref, place_ref,
                     tri_ref, sel_ref, q_out, k_out, v_out, stats_out, carry_ref, *, tm):
    @pl.when(pl.program_id(1) == 0)
    def _():
        carry_ref[...] = jnp.zeros_like(carry_ref)

    h = _rms(x_ref[0], g_ref[...]).astype(jnp.bfloat16)

    z = _dot(h, wf_ref[...]) + b3_ref[...]
    logf = jnp.minimum(z, 0.0) - jnp.log1p(jnp.exp(-jnp.abs(z)))

    tri = tri_ref[...]
    hi, mid, lo = _split3(logf)
    c = _dot(tri, hi) + _dot(tri, mid) + _dot(tri, lo) + carry_ref[...]
    carry_ref[...] = c[tm - 1:tm, :]

    chi, cmid, clo = (p.astype(jnp.float32) for p in _split3(c * LOG2E))
    lane = lax.broadcasted_iota(jnp.int32, (tm, LANES), 1)
    parts = jnp.where(lane < FOX_HEADS, chi,
                      jnp.where(lane < 2 * FOX_HEADS, cmid,
                                jnp.where(lane < ONES_LANE, clo,
                                          jnp.where(lane == ONES_LANE, 1.0, 0.0))))
    parts = parts.astype(jnp.bfloat16)

    width = FOX_HEADS * HEAD_AUG
    q = _dot(h, wq_ref[...]) * (QK_SCALE * LOG2E)
    k = _dot(h, wk_ref[...])
    v = _dot(h, wv_ref[...])
    qn2 = _dot((q * q).astype(jnp.bfloat16), sel_ref[...])
    kn2 = _dot((k * k).astype(jnp.bfloat16), sel_ref[...])

    low = lane < HEAD_DIM
    v_tail = jnp.where(lane == HEAD_DIM, 1.0, 0.0)
    for pr in range(FOX_HEADS // 2):
        pcols = slice(pr * LANES, (pr + 1) * LANES)
        acols = slice(2 * pr * HEAD_AUG, (2 * pr + 2) * HEAD_AUG)
        q_tail = _dot(parts, place_ref[:, acols])
        k_tail = _dot(parts, place_ref[:, width + 2 * pr * HEAD_AUG:width + (2 * pr + 2) * HEAD_AUG])
        for src, tail, out in ((q[:, pcols], q_tail, q_out), (k[:, pcols], k_tail, k_out),
                               (v[:, pcols], None, v_out)):
            for par, head in enumerate((src, pltpu.roll(src, HEAD_DIM, 1))):
                fill = v_tail if tail is None else tail[:, par * HEAD_AUG:(par + 1) * HEAD_AUG]
                hcols = slice((2 * pr + par) * HEAD_AUG, (2 * pr + par + 1) * HEAD_AUG)
                out[0, :, hcols] = jnp.where(low, head, fill).astype(jnp.bfloat16)

    c2 = c * LOG2E
    stats_out[0, 0] = jnp.zeros((SUBLANES, LANES), jnp.float32)
    stats_out[0, 0, 0:1, :] = jnp.sqrt(jnp.max(qn2, axis=0, keepdims=True))
    stats_out[0, 0, 1:2, :] = jnp.sqrt(jnp.max(kn2, axis=0, keepdims=True))
    stats_out[0, 0, 2:3, :] = c2[0:1, :]
    stats_out[0, 0, 3:4, :] = c2[tm - 1:tm, :]
    diag = _dot((q * k).astype(jnp.bfloat16), sel_ref[...])
    stats_out[0, 0, 4:5, :] = jnp.min(diag, axis=0, keepdims=True)


def _fox_placement():
    width = FOX_HEADS * HEAD_AUG
    p = np.zeros((LANES, 2 * width), np.float32)
    for h in range(FOX_HEADS):
        base = h * HEAD_AUG + DEC0
        for s in range(N_SPLIT):
            p[s * FOX_HEADS + h, base + s] = 1.0
            p[ONES_LANE, base + N_SPLIT + s] = 1.0
            p[ONES_LANE, width + base + s] = 1.0
            p[s * FOX_HEADS + h, width + base + N_SPLIT + s] = -1.0
    return p


def _fox_proj(x, g, w_in, b_f):
    B, S, D = x.shape
    tm = PROJ_TM
    hd_all = FOX_HEADS * HEAD_DIM
    width = FOX_HEADS * HEAD_AUG
    wq = w_in[:, :hd_all].astype(jnp.bfloat16)
    wk = w_in[:, hd_all:2 * hd_all].astype(jnp.bfloat16)
    wv = w_in[:, 2 * hd_all:3 * hd_all].astype(jnp.bfloat16)
    wf = w_in[:, 3 * hd_all:]
    wf3 = jnp.concatenate([wf] * N_SPLIT + [jnp.zeros((D, LANES - ONES_LANE), wf.dtype)], axis=1)
    b3 = jnp.concatenate([b_f] * N_SPLIT + [jnp.zeros((LANES - ONES_LANE,), b_f.dtype)])[None, :]
    place = jnp.asarray(_fox_placement(), jnp.bfloat16)
    tri = jnp.asarray(np.tril(np.ones((tm, tm), np.float32)), jnp.bfloat16)
    sel = np.zeros((hd_all, LANES), np.float32)
    for hd in range(FOX_HEADS):
        sel[hd * HEAD_DIM:(hd + 1) * HEAD_DIM, hd] = 1.0

    out_sds = jax.ShapeDtypeStruct((B, S, width), jnp.bfloat16)
    out_spec = pl.BlockSpec((1, tm, width), lambda b, t: (b, t, 0))
    stats_sds = jax.ShapeDtypeStruct((B, S // tm, SUBLANES, LANES), jnp.float32)
    stats_spec = pl.BlockSpec((1, 1, SUBLANES, LANES), lambda b, t: (b, t, 0, 0))
    return pl.pallas_call(
        functools.partial(_fox_proj_kernel, tm=tm),
        out_shape=(out_sds, out_sds, out_sds, stats_sds),
        grid=(B, S // tm),
        in_specs=[
            pl.BlockSpec((1, tm, D), lambda b, t: (b, t, 0)),
            _const_spec((1, D)),
            _const_spec((D, hd_all)), _const_spec((D, hd_all)), _const_spec((D, hd_all)),
            _const_spec((D, LANES)), _const_spec((1, LANES)),
            _const_spec((LANES, 2 * width)),
            _const_spec((tm, tm)), _const_spec((hd_all, LANES)),
        ],
        out_specs=(out_spec, out_spec, out_spec, stats_spec),
        scratch_shapes=[pltpu.VMEM((1, LANES), jnp.float32)],
        compiler_params=pltpu.CompilerParams(
            dimension_semantics=("arbitrary", "arbitrary"), vmem_limit_bytes=VMEM_LIMIT),
        name="fox_proj",
    )(x, g[None, :], wq, wk, wv, wf3.astype(jnp.bfloat16), b3, place, tri,
      jnp.asarray(sel, jnp.bfloat16))


def _flash_transpose_v(v_ref, vt_ref, qi, bq, seq):
    @pl.when(qi == 0)
    def _():
        for c in range(seq // bq):
            rows = slice(c * bq, (c + 1) * bq)
            vt_ref[:, rows] = v_ref[0, rows, :].T


def _flash_finalize(acc_ref, o_ref, bq):
    halves = []
    for hh in range(2):
        acc = acc_ref[hh]
        halves.append(acc[0:HEAD_DIM, :] * (1.0 / acc[HEAD_DIM:HEAD_DIM + 1, :]))
    o_ref[...] = jnp.concatenate(halves, axis=0).T.astype(jnp.bfloat16)


def _per_query_block(block_fn, q_ref, o_ref, bq, nsub):
    def one(u, carry):
        rows = pl.ds(pl.multiple_of(u * bq, bq), bq)
        block_fn(nsub * pl.program_id(2) + u, q_ref.at[0, rows, :], o_ref.at[0, rows, :])
        return carry

    lax.fori_loop(0, nsub, one, 0)


def _fox_flash_kernel(first_ref, q_ref, k_ref, v_ref, o_ref, *scratch, bq, seq, nsub):
    block_fn = functools.partial(_flash_running_max_block, first_ref=first_ref, k_ref=k_ref, v_ref=v_ref,
                                 scratch=scratch, bq=bq, seq=seq)
    _per_query_block(block_fn, q_ref, o_ref, bq, nsub)


def _flash_running_max_block(qi, q_ref, o_ref, *, first_ref, k_ref, v_ref, scratch, bq, seq):
    vt_ref, qt_ref, m_ref, acc_ref, s_ref, p_ref, mb_ref = scratch
    heads = 2
    nq = seq // bq
    head0 = (pl.program_id(0) * FOX_HEADS + heads * pl.program_id(1)) * nq + qi
    j0 = jnp.minimum(first_ref[head0], first_ref[head0 + nq])
    _flash_transpose_v(v_ref, vt_ref, qi, bq, seq)
    qt_ref[...] = q_ref[...].T
    m_ref[...] = jnp.full_like(m_ref, NEG_BIG)
    acc_ref[...] = jnp.zeros_like(acc_ref)

    def scores(j, slot, masked):
        start = pl.multiple_of(j * bq, bq)
        for hh in range(heads):
            cols = slice(hh * HEAD_AUG, (hh + 1) * HEAD_AUG)
            st = _dot(k_ref[0, pl.ds(start, bq), cols], qt_ref[cols, :])
            if masked:
                key = lax.broadcasted_iota(jnp.int32, (bq, bq), 0)
                qry = lax.broadcasted_iota(jnp.int32, (bq, bq), 1)
                st = jnp.where(key - qry <= (qi - j) * bq, st, NEG_BIG)
            s_ref[2 * slot + hh] = st
            mb_ref[2 * slot + hh] = jnp.max(st, axis=0, keepdims=True)

    def consume(j, slot):
        start = pl.multiple_of(j * bq, bq)
        alpha = []
        for hh in range(heads):
            m_prev = m_ref[hh]
            m_new = jnp.maximum(m_prev, mb_ref[2 * slot + hh])
            m_ref[hh] = m_new
            p_ref[hh] = jnp.exp2(s_ref[2 * slot + hh] - m_new).astype(jnp.bfloat16)
            alpha.append(jnp.exp2(m_prev - m_new))
        for hh in range(heads):
            cols = slice(hh * HEAD_AUG, (hh + 1) * HEAD_AUG)
            vt = vt_ref[cols, pl.ds(start, bq)]
            acc_ref[hh] = alpha[hh] * acc_ref[hh] + _dot(vt, p_ref[hh])

    def half_step(j, slot, masked):
        scores(j + 1, 1 - slot, masked)
        consume(j, slot)

    span = qi - j0
    pairs = jnp.maximum(span - 1, 0) // 2
    rest = span - 2 * pairs
    scores(j0, 0, True)

    def body(t, carry):
        half_step(j0 + 2 * t, 0, False)
        half_step(j0 + 2 * t + 1, 1, False)
        return carry

    lax.fori_loop(0, pairs, body, 0)

    @pl.when(rest >= 1)
    def _():
        half_step(j0 + 2 * pairs, 0, True)

    @pl.when(rest == 2)
    def _():
        half_step(j0 + 2 * pairs + 1, 1, True)

    @pl.when(rest == 1)
    def _():
        consume(qi, 1)

    @pl.when(rest != 1)
    def _():
        consume(qi, 0)

    _flash_finalize(acc_ref, o_ref, bq)


def _fox_flash_fixed_kernel(first_ref, kall_ref, q_ref, k_ref, v_ref, o_ref, *scratch, bq, seq, nsub):
    block_fn = functools.partial(_flash_fixed_shift_block, first_ref=first_ref, kall_ref=kall_ref, k_ref=k_ref,
                                 v_ref=v_ref, scratch=scratch, bq=bq, seq=seq)
    _per_query_block(block_fn, q_ref, o_ref, bq, nsub)


def _flash_fixed_shift_block(qi, q_ref, o_ref, *, first_ref, kall_ref, k_ref, v_ref, scratch, bq, seq):
    vt_ref, qt_ref, r_ref, acc_ref, p_ref = scratch
    b, hp = pl.program_id(0), pl.program_id(1)
    heads = 2
    nq = seq // bq
    head0 = (b * FOX_HEADS + heads * hp) * nq + qi
    first = (first_ref[head0], first_ref[head0 + nq])
    j0 = jnp.maximum(first[0], first[1])
    _flash_transpose_v(v_ref, vt_ref, qi, bq, seq)
    qt_ref[...] = q_ref[...].T
    acc_ref[...] = jnp.zeros_like(acc_ref)
    for hh in range(heads):
        qf = qt_ref[hh * HEAD_AUG:hh * HEAD_AUG + HEAD_DIM, :].astype(jnp.float32)
        qnorm = jnp.sqrt(jnp.sum(qf * qf, axis=0, keepdims=True))
        r_ref[hh] = qnorm * (NORM_SLACK * kall_ref[b * FOX_HEADS + heads * hp + hh])

    def blocks(j, nblk, masked, which=(0, 1)):
        start = pl.multiple_of(j * bq, bq)
        for hh in which:
            cols = slice(hh * HEAD_AUG, (hh + 1) * HEAD_AUG)
            for u in range(nblk):
                st = _dot(k_ref[0, pl.ds(start + u * bq, bq), cols], qt_ref[cols, :])
                if masked and u == nblk - 1:
                    key = lax.broadcasted_iota(jnp.int32, (bq, bq), 0)
                    qry = lax.broadcasted_iota(jnp.int32, (bq, bq), 1)
                    st = jnp.where(key <= qry, st, NEG_BIG)
                p_ref[hh, u * bq:(u + 1) * bq, :] = jnp.exp2(st - r_ref[hh]).astype(jnp.bfloat16)
        for hh in which:
            cols = slice(hh * HEAD_AUG, (hh + 1) * HEAD_AUG)
            vt = vt_ref[cols, pl.ds(start, nblk * bq)]
            acc_ref[hh] += _dot(vt, p_ref[hh, 0:nblk * bq, :])

    for hh in range(heads):
        @pl.when(first[hh] < j0)
        def _(hh=hh):
            lo = first[hh]
            count = j0 - lo

            def body1(t, carry):
                blocks(lo + 4 * t, 4, False, (hh,))
                return carry

            lax.fori_loop(0, count // 4, body1, 0)
            for left in (1, 2, 3):
                @pl.when(count % 4 == left)
                def _(left=left):
                    blocks(j0 - left, left, False, (hh,))

    span = qi - j0
    quads = span // 4

    def body(t, carry):
        blocks(j0 + 4 * t, 4, False)
        return carry

    lax.fori_loop(0, quads, body, 0)
    for left in (0, 1, 2, 3):
        @pl.when(span - 4 * quads == left)
        def _(left=left):
            blocks(qi - left, left + 1, True)

    _flash_finalize(acc_ref, o_ref, bq)


def _first_key_block(stats):
    qmax, kmax = stats[:, :, 0, :FOX_HEADS], stats[:, :, 1, :FOX_HEADS]
    c_first, c_last = stats[:, :, 2, :FOX_HEADS], stats[:, :, 3, :FOX_HEADS]
    diag_min = stats[:, :, 4, :FOX_HEADS]
    kall = jnp.max(kmax, axis=1, keepdims=True)
    thr = c_first + SKIP_LOG2 + (NORM_SLACK + DIAG_SLACK) * qmax * kall - diag_min
    skippable = c_last[:, None, :, :] > thr[:, :, None, :]
    nq = stats.shape[1]
    blocks = jnp.arange(nq, dtype=jnp.int32)
    first = jnp.min(jnp.where(skippable, nq, blocks[None, None, :, None]), axis=2)
    first = jnp.minimum(first, blocks[None, :, None])
    return first.transpose(0, 2, 1).reshape(-1)


def _fox_flash(q, k, v, stats):
    B, S, width = q.shape
    bq = FLASH_BQ
    pair = 2 * HEAD_AUG
    first = _first_key_block(stats)
    qmax, kmax, diag_min = (stats[:, :, r, :FOX_HEADS] for r in (0, 1, 4))
    kall = jnp.max(kmax, axis=1)
    gap = NORM_SLACK * qmax * kall[:, None, :] - diag_min
    fixed_shift_ok = jnp.max(gap) < SHIFT_GAP_LOG2

    blk = lambda shape, index, **kw: pl.BlockSpec(shape, lambda b, hp, i, first: index(b, hp, i), **kw)
    tensor_specs = [
        blk((1, FLASH_NSUB * bq, pair), lambda b, hp, i: (b, i, hp)),
        blk((1, S, pair), lambda b, hp, i: (b, 0, hp)),
        blk((1, S, pair), lambda b, hp, i: (b, 0, hp)),
    ]
    common = dict(
        out_shape=jax.ShapeDtypeStruct((B, S, FOX_HEADS * HEAD_DIM), jnp.bfloat16),
        compiler_params=pltpu.CompilerParams(
            dimension_semantics=("arbitrary", "arbitrary", "arbitrary"),
            vmem_limit_bytes=VMEM_LIMIT))
    grid = (B, FOX_HEADS // 2, S // (FLASH_NSUB * bq))
    out_spec = blk((1, FLASH_NSUB * bq, 2 * HEAD_DIM), lambda b, hp, i: (b, i, hp))
    vt_qt = [pltpu.VMEM((pair, S), jnp.bfloat16), pltpu.VMEM((pair, bq), jnp.bfloat16)]

    def fixed_shift():
        return pl.pallas_call(
            functools.partial(_fox_flash_fixed_kernel, bq=bq, seq=S, nsub=FLASH_NSUB),
            grid_spec=pltpu.PrefetchScalarGridSpec(
                num_scalar_prefetch=1, grid=grid,
                in_specs=[pl.BlockSpec(memory_space=pltpu.SMEM)] + tensor_specs,
                out_specs=out_spec,
                scratch_shapes=vt_qt + [pltpu.VMEM((2, 1, bq), jnp.float32),
                                        pltpu.VMEM((2, HEAD_AUG, bq), jnp.float32),
                                        pltpu.VMEM((2, 4 * bq, bq), jnp.bfloat16)]),
            name="fox_flash_fixed", **common)(first, kall.reshape(-1), q, k, v)

    def running_max():
        return pl.pallas_call(
            functools.partial(_fox_flash_kernel, bq=bq, seq=S, nsub=FLASH_NSUB),
            grid_spec=pltpu.PrefetchScalarGridSpec(
                num_scalar_prefetch=1, grid=grid, in_specs=tensor_specs, out_specs=out_spec,
                scratch_shapes=vt_qt + [pltpu.VMEM((2, 1, bq), jnp.float32),
                                        pltpu.VMEM((2, HEAD_AUG, bq), jnp.float32),
                                        pltpu.VMEM((4, bq, bq), jnp.float32),
                                        pltpu.VMEM((2, bq, bq), jnp.bfloat16),
                                        pltpu.VMEM((4, 1, bq), jnp.float32)]),
            name="fox_flash", **common)(first, q, k, v)

    return lax.cond(fixed_shift_ok, fixed_shift, running_max)


def _out_mlp_kernel(o_ref, x_ref, wo_ref, g_post_ref, g_pre_ref, wu_ref, wd_ref, g_ffn_ref, y_ref,
                    x1_ref, h_ref, *, sub):
    n_sub = x_ref.shape[0] // sub
    for s in range(n_sub):
        rows = slice(s * sub, (s + 1) * sub)
        a = _dot(o_ref[rows, :], wo_ref[...])
        x1 = x_ref[rows, :] + _rms(a, g_post_ref[...])
        x1_ref[rows, :] = x1
        h_ref[rows, :] = _rms(x1, g_pre_ref[...]).astype(jnp.bfloat16)
    for s in range(n_sub):
        rows = slice(s * sub, (s + 1) * sub)
        acc = jnp.zeros((sub, x_ref.shape[1]), jnp.float32)
        for c in range(D_FF // MLP_CHUNK):
            cols = slice(c * MLP_CHUNK, (c + 1) * MLP_CHUNK)
            u = jnp.maximum(_dot(h_ref[rows, :], wu_ref[:, cols]), 0.0)
            acc = acc + _dot((u * u).astype(jnp.bfloat16), wd_ref[cols, :])
        y_ref[rows, :] = x1_ref[rows, :] + _rms(acc, g_ffn_ref[...])


def _out_mlp(o, x, w_out, g_post, g_pre, w_up, w_down, g_ffn):
    T, D = x.shape
    tm = MLP_TM
    row = lambda width: pl.BlockSpec((tm, width), lambda t: (t, 0))
    return pl.pallas_call(
        functools.partial(_out_mlp_kernel, sub=PROJ_TM),
        out_shape=jax.ShapeDtypeStruct((T, D), jnp.float32),
        grid=(T // tm,),
        in_specs=[row(o.shape[1]), row(D),
                  _const_spec(w_out.shape), _const_spec((1, D)), _const_spec((1, D)),
                  _const_spec(w_up.shape), _const_spec(w_down.shape), _const_spec((1, D))],
        out_specs=row(D),
        scratch_shapes=[pltpu.VMEM((tm, D), jnp.float32), pltpu.VMEM((tm, D), jnp.bfloat16)],
        compiler_params=pltpu.CompilerParams(
            dimension_semantics=("arbitrary",), vmem_limit_bytes=VMEM_LIMIT),
        name="out_mlp",
    )(o, x, w_out.astype(jnp.bfloat16), g_post[None, :], g_pre[None, :],
      w_up.astype(jnp.bfloat16), w_down.astype(jnp.bfloat16), g_ffn[None, :])


def _swa_proj_kernel(x_ref, pos_ref, g_ref, wq_ref, wk_ref, wv_ref, freq_ref, lo_ref, hi_ref,
                     q_out, k_out, v_out):
    h = _rms(x_ref[...], g_ref[...]).astype(jnp.bfloat16)
    ang = pos_ref[...].astype(jnp.float32) * freq_ref[...]
    cos = jnp.cos(ang)
    sin = jnp.sin(ang)
    sin_lo = sin * lo_ref[...]
    sin_hi = sin * hi_ref[...]
    half = HEAD_DIM // 2

    def rope(t):
        return t * cos + pltpu.roll(t, LANES - half, 1) * sin_lo + pltpu.roll(t, half, 1) * sin_hi

    q = _dot(h, wq_ref[...])
    for c in range(q_out.shape[1] // LANES):
        cols = slice(c * LANES, (c + 1) * LANES)
        q_out[:, cols] = (rope(q[:, cols]) * (QK_SCALE * LOG2E)).astype(jnp.bfloat16)
    k = _dot(h, wk_ref[...])
    for c in range(k_out.shape[1] // LANES):
        cols = slice(c * LANES, (c + 1) * LANES)
        k_out[:, cols] = rope(k[:, cols]).astype(jnp.bfloat16)
    v_out[...] = _dot(h, wv_ref[...]).astype(jnp.bfloat16)


def _dup_heads(w, heads):
    d = w.shape[0]
    w = w.reshape(d, heads, 1, HEAD_DIM)
    return jnp.broadcast_to(w, (d, heads, 2, HEAD_DIM)).reshape(d, heads * 2 * HEAD_DIM)


def _swa_proj(x, pos, g, w_in):
    T, D = x.shape
    tm = PROJ_TM
    qd = SWA_Q_HEADS * HEAD_DIM
    kvd = SWA_KV_HEADS * HEAD_DIM
    kw = SWA_KV_HEADS * 2 * HEAD_DIM
    bf = lambda a: a.astype(jnp.bfloat16)
    wq = w_in[:, :qd]
    wk = w_in[:, qd:qd + kvd]
    wv = w_in[:, qd + kvd:]
    half = HEAD_DIM // 2
    inv_freq = 1.0 / (ROPE_THETA ** (jnp.arange(0, HEAD_DIM, 2, dtype=jnp.float32) / HEAD_DIM))
    freq = jnp.tile(inv_freq, LANES // half)[None, :]
    first_half = (np.arange(LANES) % HEAD_DIM) < half
    lo = np.where(first_half, -1.0, 0.0).astype(np.float32)[None, :]
    hi = np.where(first_half, 0.0, 1.0).astype(np.float32)[None, :]
    row = lambda width: pl.BlockSpec((tm, width), lambda t: (t, 0))
    return pl.pallas_call(
        _swa_proj_kernel,
        out_shape=(jax.ShapeDtypeStruct((T, qd), jnp.bfloat16),
                   jax.ShapeDtypeStruct((T, kw), jnp.bfloat16),
                   jax.ShapeDtypeStruct((T, kw), jnp.bfloat16)),
        grid=(T // tm,),
        in_specs=[row(D), row(1), _const_spec((1, D)),
                  _const_spec((D, qd)), _const_spec((D, kw)), _const_spec((D, kw)),
                  _const_spec((1, LANES)), _const_spec((1, LANES)), _const_spec((1, LANES))],
        out_specs=(row(qd), row(kw), row(kw)),
        compiler_params=pltpu.CompilerParams(
            dimension_semantics=("arbitrary",), vmem_limit_bytes=VMEM_LIMIT),
        name="swa_proj",
    )(x, pos, g[None, :], bf(wq), bf(_dup_heads(wk, SWA_KV_HEADS)), bf(_dup_heads(wv, SWA_KV_HEADS)),
      freq, jnp.asarray(lo), jnp.asarray(hi))


def _swa_attn_kernel(sink_ref, q_ref, *refs):
    k_refs, v_refs = refs[:SWA_SUB + 1], refs[SWA_SUB + 1:2 * SWA_SUB + 2]
    o_ref, s_ref = refs[2 * SWA_SUB + 2:]
    n = pl.program_id(1)
    blk = WINDOW
    cols_per_group = SWA_GROUP // 2
    key = lax.broadcasted_iota(jnp.int32, (2 * blk, blk), 0)
    qry = lax.broadcasted_iota(jnp.int32, (2 * blk, blk), 1)
    diff = qry + blk - key
    in_window = (diff >= 0) & (diff < WINDOW)
    klane = lax.broadcasted_iota(jnp.int32, (2 * blk, LANES), 1)
    top = lax.broadcasted_iota(jnp.int32, (LANES, blk), 0) < HEAD_DIM
    maxes = {}
    for sub in range(SWA_SUB):
        rows = slice(sub * blk, (sub + 1) * blk)
        kb = jnp.concatenate([k_refs[sub][0], k_refs[sub + 1][0]], axis=0).astype(jnp.float32)
        mask = in_window & (key >= jnp.where(n > 0, 0, blk)) if sub == 0 else in_window
        for g in range(SWA_KV_HEADS):
            gl = slice(g * LANES, (g + 1) * LANES)
            qt = jnp.concatenate([q_ref[0, rows, c * LANES:(c + 1) * LANES].T
                                  for c in range(g * cols_per_group, (g + 1) * cols_per_group)], axis=1)
            for par in range(2):
                keep = (klane < HEAD_DIM) if par == 0 else (klane >= HEAD_DIM)
                kpar = jnp.where(keep, kb[:, gl], 0.0).astype(jnp.bfloat16)
                st = _dot(kpar, qt)
                st = jnp.concatenate([jnp.where(mask, st[:, c * blk:(c + 1) * blk], NEG_BIG)
                                      for c in range(cols_per_group)], axis=1)
                slot = (sub * SWA_KV_HEADS + g) * 2 + par
                s_ref[slot] = st
                maxes[slot] = jnp.max(st, axis=0, keepdims=True)
    for sub in range(SWA_SUB):
        rows = slice(sub * blk, (sub + 1) * blk)
        vbt = jnp.concatenate([v_refs[sub][0], v_refs[sub + 1][0]], axis=0).T
        for g in range(SWA_KV_HEADS):
            gl = slice(g * LANES, (g + 1) * LANES)
            outs = []
            for par in range(2):
                slot = (sub * SWA_KV_HEADS + g) * 2 + par
                sink = sink_ref[2 * g + par:2 * g + par + 1, :]
                m = jnp.maximum(maxes[slot], sink)
                e = jnp.exp2(s_ref[slot] - m)
                denom = jnp.sum(e, axis=0, keepdims=True) + jnp.exp2(sink - m)
                ot = _dot(vbt[gl, :], e.astype(jnp.bfloat16))
                outs.append(ot * (1.0 / denom))
            for c in range(cols_per_group):
                both = jnp.where(top, outs[0][:, c * blk:(c + 1) * blk], outs[1][:, c * blk:(c + 1) * blk])
                col = g * cols_per_group + c
                o_ref[0, rows, col * LANES:(col + 1) * LANES] = both.T.astype(jnp.bfloat16)


def _swa_attn(q, k, v, sinks):
    B, S, qd = q.shape
    kw = k.shape[2]
    blk = WINDOW
    own = lambda b, n: (b, n, 0)
    band = [lambda b, n: (b, jnp.maximum(SWA_SUB * n - 1, 0), 0)]
    band += [functools.partial(lambda b, n, i: (b, SWA_SUB * n + i, 0), i=i) for i in range(SWA_SUB)]
    band_specs = [pl.BlockSpec((1, blk, kw), index) for index in band]
    cols_per_group = SWA_GROUP // 2
    sink_rows = (sinks * LOG2E).reshape(SWA_KV_HEADS, cols_per_group, 2).transpose(0, 2, 1)
    sink_rows = jnp.repeat(sink_rows.reshape(2 * SWA_KV_HEADS, cols_per_group), blk, axis=1)
    sink_rows = jnp.pad(sink_rows, ((0, SUBLANES - 2 * SWA_KV_HEADS), (0, 0)))
    return pl.pallas_call(
        _swa_attn_kernel,
        out_shape=jax.ShapeDtypeStruct((B, S, qd), jnp.bfloat16),
        grid=(B, S // (SWA_SUB * blk)),
        in_specs=[_const_spec(sink_rows.shape), pl.BlockSpec((1, SWA_SUB * blk, qd), own)] + band_specs + band_specs,
        out_specs=pl.BlockSpec((1, SWA_SUB * blk, qd), own),
        scratch_shapes=[pltpu.VMEM((SWA_SUB * 2 * SWA_KV_HEADS, 2 * blk, cols_per_group * blk), jnp.float32)],
        compiler_params=pltpu.CompilerParams(
            dimension_semantics=("arbitrary", "arbitrary"), vmem_limit_bytes=VMEM_LIMIT),
        name="swa_attn",
    )(sink_rows, q, *([k] * (SWA_SUB + 1)), *([v] * (SWA_SUB + 1)))


def kernel(x, positions, fox_w_in, fox_b_f, fox_w_out, swa_w_in, swa_sinks, swa_w_out,
           norm_pre_mix, norm_post_mix, norm_pre_ffn, norm_post_ffn, mlp_w_up, mlp_w_down):
    B, S, D = x.shape
    T = B * S
    q, k, v, stats = _fox_proj(x, norm_pre_mix[0], fox_w_in[0], fox_b_f[0])
    o = _fox_flash(q, k, v, stats)
    x1 = _out_mlp(o.reshape(T, -1), x.reshape(T, D), fox_w_out[0], norm_post_mix[0], norm_pre_ffn[0],
                  mlp_w_up[0], mlp_w_down[0], norm_post_ffn[0])
    q, k, v = _swa_proj(x1, positions.reshape(T, 1), norm_pre_mix[1], swa_w_in[0])
    o = _swa_attn(q.reshape(B, S, -1), k.reshape(B, S, -1), v.reshape(B, S, -1), swa_sinks[0])
    x2 = _out_mlp(o.reshape(T, -1), x1, swa_w_out[0], norm_post_mix[1], norm_pre_ffn[1],
                  mlp_w_up[1], mlp_w_down[1], norm_post_ffn[1])
    return x2.reshape(B, S, D)
```

```python
import functools

import jax
import jax.numpy as jnp
import numpy as np
from jax import lax
from jax.experimental import pallas as pl
from jax.experimental.pallas import tpu as pltpu

D_MODEL = 1024
HEAD_DIM = 64
FOX_HEADS = 16
SWA_Q_HEADS = 16
SWA_KV_HEADS = 2
SWA_GROUP = SWA_Q_HEADS // SWA_KV_HEADS
WINDOW = 128
D_FF = 4 * D_MODEL
ROPE_THETA = 10000.0
NORM_EPS = 1e-6

LANES = 128
SUBLANES = 8
HEAD_AUG = LANES
LOG2E = 1.4426950408889634
QK_SCALE = HEAD_DIM ** -0.5
NEG_BIG = -1e30
SKIP_LOG2 = 152.0
NORM_SLACK = 1.02
DIAG_SLACK = 0.02
SHIFT_GAP_LOG2 = 100.0
VMEM_LIMIT = 56 * 1024 * 1024

DEC0 = HEAD_DIM
N_SPLIT = 3
ONES_LANE = N_SPLIT * FOX_HEADS

PROJ_TM = 512
FLASH_BQ = 512
FLASH_NSUB = 8
MLP_TM = 1024
MLP_CHUNK = 1024
SWA_SUB = 8


def _rms(x, g):
    return x * lax.rsqrt(jnp.mean(x * x, axis=-1, keepdims=True) + NORM_EPS) * g


def _dot(a, b):
    return jnp.dot(a, b, preferred_element_type=jnp.float32)


def _split3(x):
    hi = x.astype(jnp.bfloat16)
    r = x - hi.astype(jnp.float32)
    mid = r.astype(jnp.bfloat16)
    lo = (r - mid.astype(jnp.float32)).astype(jnp.bfloat16)
    return hi, mid, lo


def _const_spec(shape):
    nd = len(shape)
    return pl.BlockSpec(shape, lambda *_: (0,) * nd, pipeline_mode=pl.Buffered(1))


def _fox_proj_kernel(x_ref, g_ref, wq_ref, wk_ref, wv_ref, wf_ref, b3_ref, place_ref,
                     tri_ref, sel_ref, q_out, k_out, v_out, stats_out, carry_ref, *, tm):
    @pl.when(pl.program_id(1) == 0)
    def _():
        carry_ref[...] = jnp.zeros_like(carry_ref)

    h = _rms(x_ref[0], g_ref[...]).astype(jnp.bfloat16)

    z = _dot(h, wf_ref[...]) + b3_ref[...]
    logf = jnp.minimum(z, 0.0) - jnp.log1p(jnp.exp(-jnp.abs(z)))

    tri = tri_ref[...]
    hi, mid, lo = _split3(logf)
    c = _dot(tri, hi) + _dot(tri, mid) + _dot(tri, lo) + carry_ref[...]
    carry_ref[...] = c[tm - 1:tm, :]

    chi, cmid, clo = (p.astype(jnp.float32) for p in _split3(c * LOG2E))
    lane = lax.broadcasted_iota(jnp.int32, (tm, LANES), 1)
    parts = jnp.where(lane < FOX_HEADS, chi,
                      jnp.where(lane < 2 * FOX_HEADS, cmid,
                                jnp.where(lane < ONES_LANE, clo,
                                          jnp.where(lane == ONES_LANE, 1.0, 0.0))))
    parts = parts.astype(jnp.bfloat16)

    width = FOX_HEADS * HEAD_AUG
    q = _dot(h, wq_ref[...]) * (QK_SCALE * LOG2E)
    k = _dot(h, wk_ref[...])
    v = _dot(h, wv_ref[...])
    qn2 = _dot((q * q).astype(jnp.bfloat16), sel_ref[...])
    kn2 = _dot((k * k).astype(jnp.bfloat16), sel_ref[...])

    low = lane < HEAD_DIM
    v_tail = jnp.where(lane == HEAD_DIM, 1.0, 0.0)
    for pr in range(FOX_HEADS // 2):
        pcols = slice(pr * LANES, (pr + 1) * LANES)
        acols = slice(2 * pr * HEAD_AUG, (2 * pr + 2) * HEAD_AUG)
        q_tail = _dot(parts, place_ref[:, acols])
        k_tail = _dot(parts, place_ref[:, width + 2 * pr * HEAD_AUG:width + (2 * pr + 2) * HEAD_AUG])
        for src, tail, out in ((q[:, pcols], q_tail, q_out), (k[:, pcols], k_tail, k_out),
                               (v[:, pcols], None, v_out)):
            for par, head in enumerate((src, pltpu.roll(src, HEAD_DIM, 1))):
                fill = v_tail if tail is None else tail[:, par * HEAD_AUG:(par + 1) * HEAD_AUG]
                hcols = slice((2 * pr + par) * HEAD_AUG, (2 * pr + par + 1) * HEAD_AUG)
                out[0, :, hcols] = jnp.where(low, head, fill).astype(jnp.bfloat16)

    c2 = c * LOG2E
    stats_out[0, 0] = jnp.zeros((SUBLANES, LANES), jnp.float32)
    stats_out[0, 0, 0:1, :] = jnp.sqrt(jnp.max(qn2, axis=0, keepdims=True))
    stats_out[0, 0, 1:2, :] = jnp.sqrt(jnp.max(kn2, axis=0, keepdims=True))
    stats_out[0, 0, 2:3, :] = c2[0:1, :]
    stats_out[0, 0, 3:4, :] = c2[tm - 1:tm, :]
    diag = _dot((q * k).astype(jnp.bfloat16), sel_ref[...])
    stats_out[0, 0, 4:5, :] = jnp.min(diag, axis=0, keepdims=True)


def _fox_placement():
    width = FOX_HEADS * HEAD_AUG
    p = np.zeros((LANES, 2 * width), np.float32)
    for h in range(FOX_HEADS):
        base = h * HEAD_AUG + DEC0
        for s in range(N_SPLIT):
            p[s * FOX_HEADS + h, base + s] = 1.0
            p[ONES_LANE, base + N_SPLIT + s] = 1.0
            p[ONES_LANE, width + base + s] = 1.0
            p[s * FOX_HEADS + h, width + base + N_SPLIT + s] = -1.0
    return p


def _fox_proj(x, g, w_in, b_f):
    B, S, D = x.shape
    tm = PROJ_TM
    hd_all = FOX_HEADS * HEAD_DIM
    width = FOX_HEADS * HEAD_AUG
    wq = w_in[:, :hd_all].astype(jnp.bfloat16)
    wk = w_in[:, hd_all:2 * hd_all].astype(jnp.bfloat16)
    wv = w_in[:, 2 * hd_all:3 * hd_all].astype(jnp.bfloat16)
    wf = w_in[:, 3 * hd_all:]
    wf3 = jnp.concatenate([wf] * N_SPLIT + [jnp.zeros((D, LANES - ONES_LANE), wf.dtype)], axis=1)
    b3 = jnp.concatenate([b_f] * N_SPLIT + [jnp.zeros((LANES - ONES_LANE,), b_f.dtype)])[None, :]
    place = jnp.asarray(_fox_placement(), jnp.bfloat16)
    tri = jnp.asarray(np.tril(np.ones((tm, tm), np.float32)), jnp.bfloat16)
    sel = np.zeros((hd_all, LANES), np.float32)
    for hd in range(FOX_HEADS):
        sel[hd * HEAD_DIM:(hd + 1) * HEAD_DIM, hd] = 1.0

    out_sds = jax.ShapeDtypeStruct((B, S, width), jnp.bfloat16)
    out_spec = pl.BlockSpec((1, tm, width), lambda b, t: (b, t, 0))
    stats_sds = jax.ShapeDtypeStruct((B, S // tm, SUBLANES, LANES), jnp.float32)
    stats_spec = pl.BlockSpec((1, 1, SUBLANES, LANES), lambda b, t: (b, t, 0, 0))
    return pl.pallas_call(
        functools.partial(_fox_proj_kernel, tm=tm),
        out_shape=(out_sds, out_sds, out_sds, stats_sds),
        grid=(B, S // tm),
        in_specs=[
            pl.BlockSpec((1, tm, D), lambda b, t: (b, t, 0)),
            _const_spec((1, D)),
            _const_spec((D, hd_all)), _const_spec((D, hd_all)), _const_spec((D, hd_all)),
            _const_spec((D, LANES)), _const_spec((1, LANES)),
            _const_spec((LANES, 2 * width)),
            _const_spec((tm, tm)), _const_spec((hd_all, LANES)),
        ],
        out_specs=(out_spec, out_spec, out_spec, stats_spec),
        scratch_shapes=[pltpu.VMEM((1, LANES), jnp.float32)],
        compiler_params=pltpu.CompilerParams(
            dimension_semantics=("arbitrary", "arbitrary"), vmem_limit_bytes=VMEM_LIMIT),
        name="fox_proj",
    )(x, g[None, :], wq, wk, wv, wf3.astype(jnp.bfloat16), b3, place, tri,
      jnp.asarray(sel, jnp.bfloat16))


def _flash_transpose_v(v_ref, vt_ref, qi, bq, seq):
    @pl.when(qi == 0)
    def _():
        for c in range(seq // bq):
            rows = slice(c * bq, (c + 1) * bq)
            vt_ref[:, rows] = v_ref[0, rows, :].T


def _flash_finalize(acc_ref, o_ref, bq):
    halves = []
    for hh in range(2):
        acc = acc_ref[hh]
        halves.append(acc[0:HEAD_DIM, :] * (1.0 / acc[HEAD_DIM:HEAD_DIM + 1, :]))
    o_ref[...] = jnp.concatenate(halves, axis=0).T.astype(jnp.bfloat16)


def _per_query_block(block_fn, q_ref, o_ref, bq, nsub):
    def one(u, carry):
        rows = pl.ds(pl.multiple_of(u * bq, bq), bq)
        block_fn(nsub * pl.program_id(2) + u, q_ref.at[0, rows, :], o_ref.at[0, rows, :])
        return carry

    lax.fori_loop(0, nsub, one, 0)


def _fox_flash_kernel(first_ref, q_ref, k_ref, v_ref, o_ref, *scratch, bq, seq, nsub):
    block_fn = functools.partial(_flash_running_max_block, first_ref=first_ref, k_ref=k_ref, v_ref=v_ref,
                                 scratch=scratch, bq=bq, seq=seq)
    _per_query_block(block_fn, q_ref, o_ref, bq, nsub)


def _flash_running_max_block(qi, q_ref, o_ref, *, first_ref, k_ref, v_ref, scratch, bq, seq):
    vt_ref, qt_ref, m_ref, acc_ref, s_ref, p_ref, mb_ref = scratch
    heads = 2
    nq = seq // bq
    head0 = (pl.program_id(0) * FOX_HEADS + heads * pl.program_id(1)) * nq + qi
    j0 = jnp.minimum(first_ref[head0], first_ref[head0 + nq])
    _flash_transpose_v(v_ref, vt_ref, qi, bq, seq)
    qt_ref[...] = q_ref[...].T
    m_ref[...] = jnp.full_like(m_ref, NEG_BIG)
    acc_ref[...] = jnp.zeros_like(acc_ref)

    def scores(j, slot, masked):
        start = pl.multiple_of(j * bq, bq)
        for hh in range(heads):
            cols = slice(hh * HEAD_AUG, (hh + 1) * HEAD_AUG)
            st = _dot(k_ref[0, pl.ds(start, bq), cols], qt_ref[cols, :])
            if masked:
                key = lax.broadcasted_iota(jnp.int32, (bq, bq), 0)
                qry = lax.broadcasted_iota(jnp.int32, (bq, bq), 1)
                st = jnp.where(key - qry <= (qi - j) * bq, st, NEG_BIG)
            s_ref[2 * slot + hh] = st
            mb_ref[2 * slot + hh] = jnp.max(st, axis=0, keepdims=True)

    def consume(j, slot):
        start = pl.multiple_of(j * bq, bq)
        alpha = []
        for hh in range(heads):
            m_prev = m_ref[hh]
            m_new = jnp.maximum(m_prev, mb_ref[2 * slot + hh])
            m_ref[hh] = m_new
            p_ref[hh] = jnp.exp2(s_ref[2 * slot + hh] - m_new).astype(jnp.bfloat16)
            alpha.append(jnp.exp2(m_prev - m_new))
        for hh in range(heads):
            cols = slice(hh * HEAD_AUG, (hh + 1) * HEAD_AUG)
            vt = vt_ref[cols, pl.ds(start, bq)]
            acc_ref[hh] = alpha[hh] * acc_ref[hh] + _dot(vt, p_ref[hh])

    def half_step(j, slot, masked):
        scores(j + 1, 1 - slot, masked)
        consume(j, slot)

    span = qi - j0
    pairs = jnp.maximum(span - 1, 0) // 2
    rest = span - 2 * pairs
    scores(j0, 0, True)

    def body(t, carry):
        half_step(j0 + 2 * t, 0, False)
        half_step(j0 + 2 * t + 1, 1, False)
        return carry

    lax.fori_loop(0, pairs, body, 0)

    @pl.when(rest >= 1)
    def _():
        half_step(j0 + 2 * pairs, 0, True)

    @pl.when(rest == 2)
    def _():
        half_step(j0 + 2 * pairs + 1, 1, True)

    @pl.when(rest == 1)
    def _():
        consume(qi, 1)

    @pl.when(rest != 1)
    def _():
        consume(qi, 0)

    _flash_finalize(acc_ref, o_ref, bq)


def _fox_flash_fixed_kernel(first_ref, kall_ref, q_ref, k_ref, v_ref, o_ref, *scratch, bq, seq, nsub):
    block_fn = functools.partial(_flash_fixed_shift_block, first_ref=first_ref, kall_ref=kall_ref, k_ref=k_ref,
                                 v_ref=v_ref, scratch=scratch, bq=bq, seq=seq)
    _per_query_block(block_fn, q_ref, o_ref, bq, nsub)


def _flash_fixed_shift_block(qi, q_ref, o_ref, *, first_ref, kall_ref, k_ref, v_ref, scratch, bq, seq):
    vt_ref, qt_ref, r_ref, acc_ref, p_ref = scratch
    b, hp = pl.program_id(0), pl.program_id(1)
    heads = 2
    nq = seq // bq
    head0 = (b * FOX_HEADS + heads * hp) * nq + qi
    first = (first_ref[head0], first_ref[head0 + nq])
    j0 = jnp.maximum(first[0], first[1])
    _flash_transpose_v(v_ref, vt_ref, qi, bq, seq)
    qt_ref[...] = q_ref[...].T
    acc_ref[...] = jnp.zeros_like(acc_ref)
    for hh in range(heads):
        qf = qt_ref[hh * HEAD_AUG:hh * HEAD_AUG + HEAD_DIM, :].astype(jnp.float32)
        qnorm = jnp.sqrt(jnp.sum(qf * qf, axis=0, keepdims=True))
        r_ref[hh] = qnorm * (NORM_SLACK * kall_ref[b * FOX_HEADS + heads * hp + hh])

    def blocks(j, nblk, masked, which=(0, 1)):
        start = pl.multiple_of(j * bq, bq)
        for hh in which:
            cols = slice(hh * HEAD_AUG, (hh + 1) * HEAD_AUG)
            for u in range(nblk):
                st = _dot(k_ref[0, pl.ds(start + u * bq, bq), cols], qt_ref[cols, :])
                if masked and u == nblk - 1:
                    key = lax.broadcasted_iota(jnp.int32, (bq, bq), 0)
                    qry = lax.broadcasted_iota(jnp.int32, (bq, bq), 1)
                    st = jnp.where(key <= qry, st, NEG_BIG)
                p_ref[hh, u * bq:(u + 1) * bq, :] = jnp.exp2(st - r_ref[hh]).astype(jnp.bfloat16)
        for hh in which:
            cols = slice(hh * HEAD_AUG, (hh + 1) * HEAD_AUG)
            vt = vt_ref[cols, pl.ds(start, nblk * bq)]
            acc_ref[hh] += _dot(vt, p_ref[hh, 0:nblk * bq, :])

    for hh in range(heads):
        @pl.when(first[hh] < j0)
        def _(hh=hh):
            lo = first[hh]
            count = j0 - lo

            def body1(t, carry):
                blocks(lo + 4 * t, 4, False, (hh,))
                return carry

            lax.fori_loop(0, count // 4, body1, 0)
            for left in (1, 2, 3):
                @pl.when(count % 4 == left)
                def _(left=left):
                    blocks(j0 - left, left, False, (hh,))

    span = qi - j0
    quads = span // 4

    def body(t, carry):
        blocks(j0 + 4 * t, 4, False)
        return carry

    lax.fori_loop(0, quads, body, 0)
    for left in (0, 1, 2, 3):
        @pl.when(span - 4 * quads == left)
        def _(left=left):
            blocks(qi - left, left + 1, True)

    _flash_finalize(acc_ref, o_ref, bq)


def _first_key_block(stats):
    qmax, kmax = stats[:, :, 0, :FOX_HEADS], stats[:, :, 1, :FOX_HEADS]
    c_first, c_last = stats[:, :, 2, :FOX_HEADS], stats[:, :, 3, :FOX_HEADS]
    diag_min = stats[:, :, 4, :FOX_HEADS]
    kall = jnp.max(kmax, axis=1, keepdims=True)
    thr = c_first + SKIP_LOG2 + (NORM_SLACK + DIAG_SLACK) * qmax * kall - diag_min
    skippable = c_last[:, None, :, :] > thr[:, :, None, :]
    nq = stats.shape[1]
    blocks = jnp.arange(nq, dtype=jnp.int32)
    first = jnp.min(jnp.where(skippable, nq, blocks[None, None, :, None]), axis=2)
    first = jnp.minimum(first, blocks[None, :, None])
    return first.transpose(0, 2, 1).reshape(-1)


def _fox_flash(q, k, v, stats):
    B, S, width = q.shape
    bq = FLASH_BQ
    pair = 2 * HEAD_AUG
    first = _first_key_block(stats)
    qmax, kmax, diag_min = (stats[:, :, r, :FOX_HEADS] for r in (0, 1, 4))
    kall = jnp.max(kmax, axis=1)
    gap = NORM_SLACK * qmax * kall[:, None, :] - diag_min
    fixed_shift_ok = jnp.max(gap) < SHIFT_GAP_LOG2

    blk = lambda shape, index, **kw: pl.BlockSpec(shape, lambda b, hp, i, first: index(b, hp, i), **kw)
    tensor_specs = [
        blk((1, FLASH_NSUB * bq, pair), lambda b, hp, i: (b, i, hp)),
        blk((1, S, pair), lambda b, hp, i: (b, 0, hp)),
        blk((1, S, pair), lambda b, hp, i: (b, 0, hp)),
    ]
    common = dict(
        out_shape=jax.ShapeDtypeStruct((B, S, FOX_HEADS * HEAD_DIM), jnp.bfloat16),
        compiler_params=pltpu.CompilerParams(
            dimension_semantics=("arbitrary", "arbitrary", "arbitrary"),
            vmem_limit_bytes=VMEM_LIMIT))
    grid = (B, FOX_HEADS // 2, S // (FLASH_NSUB * bq))
    out_spec = blk((1, FLASH_NSUB * bq, 2 * HEAD_DIM), lambda b, hp, i: (b, i, hp))
    vt_qt = [pltpu.VMEM((pair, S), jnp.bfloat16), pltpu.VMEM((pair, bq), jnp.bfloat16)]

    def fixed_shift():
        return pl.pallas_call(
            functools.partial(_fox_flash_fixed_kernel, bq=bq, seq=S, nsub=FLASH_NSUB),
            grid_spec=pltpu.PrefetchScalarGridSpec(
                num_scalar_prefetch=1, grid=grid,
                in_specs=[pl.BlockSpec(memory_space=pltpu.SMEM)] + tensor_specs,
                out_specs=out_spec,
                scratch_shapes=vt_qt + [pltpu.VMEM((2, 1, bq), jnp.float32),
                                        pltpu.VMEM((2, HEAD_AUG, bq), jnp.float32),
                                        pltpu.VMEM((2, 4 * bq, bq), jnp.bfloat16)]),
            name="fox_flash_fixed", **common)(first, kall.reshape(-1), q, k, v)

    def running_max():
        return pl.pallas_call(
            functools.partial(_fox_flash_kernel, bq=bq, seq=S, nsub=FLASH_NSUB),
            grid_spec=pltpu.PrefetchScalarGridSpec(
                num_scalar_prefetch=1, grid=grid, in_specs=tensor_specs, out_specs=out_spec,
                scratch_shapes=vt_qt + [pltpu.VMEM((2, 1, bq), jnp.float32),
                                        pltpu.VMEM((2, HEAD_AUG, bq), jnp.float32),
                                        pltpu.VMEM((4, bq, bq), jnp.float32),
                                        pltpu.VMEM((2, bq, bq), jnp.bfloat16),
                                        pltpu.VMEM((4, 1, bq), jnp.float32)]),
            name="fox_flash", **common)(first, q, k, v)

    return lax.cond(fixed_shift_ok, fixed_shift, running_max)


def _out_mlp_kernel(o_ref, x_ref, wo_ref, g_post_ref, g_pre_ref, wu_ref, wd_ref, g_ffn_ref, y_ref,
                    x1_ref, h_ref, *, sub):
    n_sub = x_ref.shape[0] // sub
    for s in range(n_sub):
        rows = slice(s * sub, (s + 1) * sub)
        a = _dot(o_ref[rows, :], wo_ref[...])
        x1 = x_ref[rows, :] + _rms(a, g_post_ref[...])
        x1_ref[rows, :] = x1
        h_ref[rows, :] = _rms(x1, g_pre_ref[...]).astype(jnp.bfloat16)
    for s in range(n_sub):
        rows = slice(s * sub, (s + 1) * sub)
        acc = jnp.zeros((sub, x_ref.shape[1]), jnp.float32)
        for c in range(D_FF // MLP_CHUNK):
            cols = slice(c * MLP_CHUNK, (c + 1) * MLP_CHUNK)
            u = jnp.maximum(_dot(h_ref[rows, :], wu_ref[:, cols]), 0.0)
            acc = acc + _dot((u * u).astype(jnp.bfloat16), wd_ref[cols, :])
        y_ref[rows, :] = x1_ref[rows, :] + _rms(acc, g_ffn_ref[...])


def _out_mlp(o, x, w_out, g_post, g_pre, w_up, w_down, g_ffn):
    T, D = x.shape
    tm = MLP_TM
    row = lambda width: pl.BlockSpec((tm, width), lambda t: (t, 0))
    return pl.pallas_call(
        functools.partial(_out_mlp_kernel, sub=PROJ_TM),
        out_shape=jax.ShapeDtypeStruct((T, D), jnp.float32),
        grid=(T // tm,),
        in_specs=[row(o.shape[1]), row(D),
                  _const_spec(w_out.shape), _const_spec((1, D)), _const_spec((1, D)),
                  _const_spec(w_up.shape), _const_spec(w_down.shape), _const_spec((1, D))],
        out_specs=row(D),
        scratch_shapes=[pltpu.VMEM((tm, D), jnp.float32), pltpu.VMEM((tm, D), jnp.bfloat16)],
        compiler_params=pltpu.CompilerParams(
            dimension_semantics=("arbitrary",), vmem_limit_bytes=VMEM_LIMIT),
        name="out_mlp",
    )(o, x, w_out.astype(jnp.bfloat16), g_post[None, :], g_pre[None, :],
      w_up.astype(jnp.bfloat16), w_down.astype(jnp.bfloat16), g_ffn[None, :])


def _swa_proj_kernel(x_ref, pos_ref, g_ref, wq_ref, wk_ref, wv_ref, freq_ref, lo_ref, hi_ref,
                     q_out, k_out, v_out):
    h = _rms(x_ref[...], g_ref[...]).astype(jnp.bfloat16)
    ang = pos_ref[...].astype(jnp.float32) * freq_ref[...]
    cos = jnp.cos(ang)
    sin = jnp.sin(ang)
    sin_lo = sin * lo_ref[...]
    sin_hi = sin * hi_ref[...]
    half = HEAD_DIM // 2

    def rope(t):
        return t * cos + pltpu.roll(t, LANES - half, 1) * sin_lo + pltpu.roll(t, half, 1) * sin_hi

    q = _dot(h, wq_ref[...])
    for c in range(q_out.shape[1] // LANES):
        cols = slice(c * LANES, (c + 1) * LANES)
        q_out[:, cols] = (rope(q[:, cols]) * (QK_SCALE * LOG2E)).astype(jnp.bfloat16)
    k = _dot(h, wk_ref[...])
    for c in range(k_out.shape[1] // LANES):
        cols = slice(c * LANES, (c + 1) * LANES)
        k_out[:, cols] = rope(k[:, cols]).astype(jnp.bfloat16)
    v_out[...] = _dot(h, wv_ref[...]).astype(jnp.bfloat16)


def _dup_heads(w, heads):
    d = w.shape[0]
    w = w.reshape(d, heads, 1, HEAD_DIM)
    return jnp.broadcast_to(w, (d, heads, 2, HEAD_DIM)).reshape(d, heads * 2 * HEAD_DIM)


def _swa_proj(x, pos, g, w_in):
    T, D = x.shape
    tm = PROJ_TM
    qd = SWA_Q_HEADS * HEAD_DIM
    kvd = SWA_KV_HEADS * HEAD_DIM
    kw = SWA_KV_HEADS * 2 * HEAD_DIM
    bf = lambda a: a.astype(jnp.bfloat16)
    wq = w_in[:, :qd]
    wk = w_in[:, qd:qd + kvd]
    wv = w_in[:, qd + kvd:]
    half = HEAD_DIM // 2
    inv_freq = 1.0 / (ROPE_THETA ** (jnp.arange(0, HEAD_DIM, 2, dtype=jnp.float32) / HEAD_DIM))
    freq = jnp.tile(inv_freq, LANES // half)[None, :]
    first_half = (np.arange(LANES) % HEAD_DIM) < half
    lo = np.where(first_half, -1.0, 0.0).astype(np.float32)[None, :]
    hi = np.where(first_half, 0.0, 1.0).astype(np.float32)[None, :]
    row = lambda width: pl.BlockSpec((tm, width), lambda t: (t, 0))
    return pl.pallas_call(
        _swa_proj_kernel,
        out_shape=(jax.ShapeDtypeStruct((T, qd), jnp.bfloat16),
                   jax.ShapeDtypeStruct((T, kw), jnp.bfloat16),
                   jax.ShapeDtypeStruct((T, kw), jnp.bfloat16)),
        grid=(T // tm,),
        in_specs=[row(D), row(1), _const_spec((1, D)),
                  _const_spec((D, qd)), _const_spec((D, kw)), _const_spec((D, kw)),
                  _const_spec((1, LANES)), _const_spec((1, LANES)), _const_spec((1, LANES))],
        out_specs=(row(qd), row(kw), row(kw)),
        compiler_params=pltpu.CompilerParams(
            dimension_semantics=("arbitrary",), vmem_limit_bytes=VMEM_LIMIT),
        name="swa_proj",
    )(x, pos, g[None, :], bf(wq), bf(_dup_heads(wk, SWA_KV_HEADS)), bf(_dup_heads(wv, SWA_KV_HEADS)),
      freq, jnp.asarray(lo), jnp.asarray(hi))


def _swa_attn_kernel(sink_ref, q_ref, *refs):
    k_refs, v_refs = refs[:SWA_SUB + 1], refs[SWA_SUB + 1:2 * SWA_SUB + 2]
    o_ref, s_ref = refs[2 * SWA_SUB + 2:]
    n = pl.program_id(1)
    blk = WINDOW
    cols_per_group = SWA_GROUP // 2
    key = lax.broadcasted_iota(jnp.int32, (2 * blk, blk), 0)
    qry = lax.broadcasted_iota(jnp.int32, (2 * blk, blk), 1)
    diff = qry + blk - key
    in_window = (diff >= 0) & (diff < WINDOW)
    klane = lax.broadcasted_iota(jnp.int32, (2 * blk, LANES), 1)
    top = lax.broadcasted_iota(jnp.int32, (LANES, blk), 0) < HEAD_DIM
    maxes = {}
    for sub in range(SWA_SUB):
        rows = slice(sub * blk, (sub + 1) * blk)
        kb = jnp.concatenate([k_refs[sub][0], k_refs[sub + 1][0]], axis=0).astype(jnp.float32)
        mask = in_window & (key >= jnp.where(n > 0, 0, blk)) if sub == 0 else in_window
        for g in range(SWA_KV_HEADS):
            gl = slice(g * LANES, (g + 1) * LANES)
            qt = jnp.concatenate([q_ref[0, rows, c * LANES:(c + 1) * LANES].T
                                  for c in range(g * cols_per_group, (g + 1) * cols_per_group)], axis=1)
            for par in range(2):
                keep = (klane < HEAD_DIM) if par == 0 else (klane >= HEAD_DIM)
                kpar = jnp.where(keep, kb[:, gl], 0.0).astype(jnp.bfloat16)
                st = _dot(kpar, qt)
                st = jnp.concatenate([jnp.where(mask, st[:, c * blk:(c + 1) * blk], NEG_BIG)
                                      for c in range(cols_per_group)], axis=1)
                slot = (sub * SWA_KV_HEADS + g) * 2 + par
                s_ref[slot] = st
                maxes[slot] = jnp.max(st, axis=0, keepdims=True)
    for sub in range(SWA_SUB):
        rows = slice(sub * blk, (sub + 1) * blk)
        vbt = jnp.concatenate([v_refs[sub][0], v_refs[sub + 1][0]], axis=0).T
        for g in range(SWA_KV_HEADS):
            gl = slice(g * LANES, (g + 1) * LANES)
            outs = []
            for par in range(2):
                slot = (sub * SWA_KV_HEADS + g) * 2 + par
                sink = sink_ref[2 * g + par:2 * g + par + 1, :]
                m = jnp.maximum(maxes[slot], sink)
                e = jnp.exp2(s_ref[slot] - m)
                denom = jnp.sum(e, axis=0, keepdims=True) + jnp.exp2(sink - m)
                ot = _dot(vbt[gl, :], e.astype(jnp.bfloat16))
                outs.append(ot * (1.0 / denom))
            for c in range(cols_per_group):
                both = jnp.where(top, outs[0][:, c * blk:(c + 1) * blk], outs[1][:, c * blk:(c + 1) * blk])
                col = g * cols_per_group + c
                o_ref[0, rows, col * LANES:(col + 1) * LANES] = both.T.astype(jnp.bfloat16)


def _swa_attn(q, k, v, sinks):
    B, S, qd = q.shape
    kw = k.shape[2]
    blk = WINDOW
    own = lambda b, n: (b, n, 0)
    band = [lambda b, n: (b, jnp.maximum(SWA_SUB * n - 1, 0), 0)]
    band += [functools.partial(lambda b, n, i: (b, SWA_SUB * n + i, 0), i=i) for i in range(SWA_SUB)]
    band_specs = [pl.BlockSpec((1, blk, kw), index) for index in band]
    cols_per_group = SWA_GROUP // 2
    sink_rows = (sinks * LOG2E).reshape(SWA_KV_HEADS, cols_per_group, 2).transpose(0, 2, 1)
    sink_rows = jnp.repeat(sink_rows.reshape(2 * SWA_KV_HEADS, cols_per_group), blk, axis=1)
    sink_rows = jnp.pad(sink_rows, ((0, SUBLANES - 2 * SWA_KV_HEADS), (0, 0)))
    return pl.pallas_call(
        _swa_attn_kernel,
        out_shape=jax.ShapeDtypeStruct((B, S, qd), jnp.bfloat16),
        grid=(B, S // (SWA_SUB * blk)),
        in_specs=[_const_spec(sink_rows.shape), pl.BlockSpec((1, SWA_SUB * blk, qd), own)] + band_specs + band_specs,
        out_specs=pl.BlockSpec((1, SWA_SUB * blk, qd), own),
        scratch_shapes=[pltpu.VMEM((SWA_SUB * 2 * SWA_KV_HEADS, 2 * blk, cols_per_group * blk), jnp.float32)],
        compiler_params=pltpu.CompilerParams(
            dimension_semantics=("arbitrary", "arbitrary"), vmem_limit_bytes=VMEM_LIMIT),
        name="swa_attn",
    )(sink_rows, q, *([k] * (SWA_SUB + 1)), *([v] * (SWA_SUB + 1)))


def kernel(x, positions, fox_w_in, fox_b_f, fox_w_out, swa_w_in, swa_sinks, swa_w_out,
           norm_pre_mix, norm_post_mix, norm_pre_ffn, norm_post_ffn, mlp_w_up, mlp_w_down):
    B, S, D = x.shape
    T = B * S
    q, k, v, stats = _fox_proj(x, norm_pre_mix[0], fox_w_in[0], fox_b_f[0])
    o = _fox_flash(q, k, v, stats)
    x1 = _out_mlp(o.reshape(T, -1), x.reshape(T, D), fox_w_out[0], norm_post_mix[0], norm_pre_ffn[0],
                  mlp_w_up[0], mlp_w_down[0], norm_post_ffn[0])
    q, k, v = _swa_proj(x1, positions.reshape(T, 1), norm_pre_mix[1], swa_w_in[0])
    o = _swa_attn(q.reshape(B, S, -1), k.reshape(B, S, -1), v.reshape(B, S, -1), swa_sinks[0])
    x2 = _out_mlp(o.reshape(T, -1), x1, swa_w_out[0], norm_post_mix[1], norm_pre_ffn[1],
                  mlp_w_up[1], mlp_w_down[1], norm_post_ffn[1])
    return x2.reshape(B, S, D)
```
